```python
import jax, jax.numpy as jnp
from jax import lax
import numpy as np

D_MODEL = 2048
BATCH = 1
SEQ = 8192
DEPTH = 4

N_MIXERS = 3
N_HGRN_LAYERS = (DEPTH + N_MIXERS - 1) // N_MIXERS
DEEPNORM_ALPHA = (2 * DEPTH) ** 0.25
DEEPNORM_BETA = (8 * DEPTH) ** -0.25
LN_EPS = 1e-5
RMS_EPS = 1e-6
ROPE_THETA = 500000.0
ROPE_FRACTION = 4

HG_HEAD_DIM = 128
HG_HEADS = D_MODEL // HG_HEAD_DIM
HG_CHUNK = 64

NSA_HEAD_DIM = 128
NSA_HEADS = D_MODEL // NSA_HEAD_DIM
NSA_KV_GROUPS = 4
CMP_BLOCK = 32
CMP_STRIDE = 16
CMP_HIDDEN = 256
SLC_BLOCK = 64
SLC_TOPK = 16
NSA_WINDOW = 512
NSA_Q_BLOCK = 64
FORCE_BONUS = 1.0e4

SWA_HEAD_DIM = 64
SWA_HEADS = D_MODEL // SWA_HEAD_DIM
SWA_KV_HEADS = 4
SWA_WINDOW = 128

HG_IN = 4 * D_MODEL
NSA_IN = NSA_HEADS * NSA_HEAD_DIM + 6 * NSA_KV_GROUPS * NSA_HEAD_DIM + 3 * NSA_HEADS + D_MODEL
SWA_IN = SWA_HEADS * SWA_HEAD_DIM + 2 * SWA_KV_HEADS * SWA_HEAD_DIM + D_MODEL

kernel_name = "hybrid_hgrn2_nsa_swa_sink_deepnorm"

F32 = jnp.float32


def _layer_norm(x, g, b):
    xf = x.astype(F32)
    mu = jnp.mean(xf, -1, keepdims=True)
    xc = xf - mu
    var = jnp.mean(xc * xc, -1, keepdims=True)
    return (xc * lax.rsqrt(var + LN_EPS) * g.astype(F32) + b.astype(F32)).astype(x.dtype)


def _rotary_tables(positions, head_dim):
    rot = head_dim // ROPE_FRACTION
    inv_freq = ROPE_THETA ** (-jnp.arange(0, rot, 2, dtype=F32) / rot)
    ang = positions.astype(F32)[..., None] * inv_freq
    return jnp.cos(ang), jnp.sin(ang)


def _partial_rotary(t, cos, sin):
    half = cos.shape[-1]
    tf = t.astype(F32)
    c, s = cos[:, :, None, :], sin[:, :, None, :]
    x1, x2 = tf[..., :half], tf[..., half:2 * half]
    return jnp.concatenate([x1 * c - x2 * s, x2 * c + x1 * s, tf[..., 2 * half:]], -1).astype(t.dtype)


def _masked_softmax(scores, mask):
    s = jnp.where(mask, scores.astype(F32), -jnp.inf)
    m = jnp.max(s, -1, keepdims=True)
    m = jnp.where(jnp.isfinite(m), m, 0.0)
    e = jnp.exp(s - m)
    return e / jnp.maximum(jnp.sum(e, -1, keepdims=True), 1e-30)


def _hgrn_lower_bounds(lb_logits):
    lb = jnp.cumsum(jax.nn.softmax(lb_logits.astype(F32), axis=0), axis=0)
    return lb - lb[0:1]


def _hgrn2_mixer(x, w_in, g_norm, w_out, lower_bound):
    B, S, D = x.shape
    H, Dk, C = HG_HEADS, HG_HEAD_DIM, HG_CHUNK
    n_chunks = S // C
    q, f_logit, v, z = jnp.split(x @ w_in, 4, axis=-1)
    log_f = jnp.logaddexp(jnp.log(lower_bound),
                          jnp.log1p(-lower_bound) + jax.nn.log_sigmoid(f_logit.astype(F32)))
    k = -jnp.expm1(log_f)

    def chunks(t):
        return t.astype(F32).reshape(B, n_chunks, C, H, Dk).transpose(1, 0, 3, 2, 4)

    causal = jnp.tril(jnp.ones((C, C), dtype=bool))[:, :, None]

    def step(state, inp):
        q_c, k_c, v_c, g_c = inp
        b = jnp.cumsum(g_c, axis=2)
        o_inter = jnp.einsum('bhtk,bhkv->bhtv', q_c * jnp.exp(b), state)
        rel = jnp.where(causal, b[:, :, :, None, :] - b[:, :, None, :, :], -jnp.inf)
        a = jnp.einsum('bhtk,bhtsk->bhts', q_c, jnp.exp(rel) * k_c[:, :, None, :, :])
        o = o_inter + jnp.einsum('bhts,bhsv->bhtv', a, v_c)
        b_end = b[:, :, -1:, :]
        state = state * jnp.exp(b_end[:, :, 0, :, None]) + jnp.einsum(
            'bhsk,bhsv->bhkv', k_c * jnp.exp(b_end - b), v_c)
        return state, o

    state0 = jnp.zeros((B, H, Dk, Dk), F32)
    _, o = lax.scan(step, state0, (chunks(q), chunks(k), chunks(v), chunks(log_f)))
    o = o.transpose(1, 0, 3, 2, 4).reshape(B, S, H, Dk)
    o = o * lax.rsqrt(jnp.mean(o * o, -1, keepdims=True) + RMS_EPS)
    o = o.reshape(B, S, D) * g_norm.astype(F32) * jax.nn.silu(z.astype(F32))
    return o.astype(x.dtype) @ w_out


def _nsa_mixer(x, cos, sin, w_in, cmp_pos_k, cmp_w1_k, cmp_w2_k,
               cmp_pos_v, cmp_w1_v, cmp_w2_v, w_out):
    B, S, D = x.shape
    H, G, Dh = NSA_HEADS, NSA_KV_GROUPS, NSA_HEAD_DIM
    R = H // G
    QB = NSA_Q_BLOCK
    widths = [H * Dh] + [G * Dh] * 6 + [3 * H]
    q, k_c, v_c, k_s, v_s, k_w, v_w, gate_logit, z = jnp.split(
        x @ w_in, [int(c) for c in np.cumsum(widths)], axis=-1)
    q = _partial_rotary(q.reshape(B, S, H, Dh), cos, sin)
    k_c, k_s, k_w = (_partial_rotary(t.reshape(B, S, G, Dh), cos, sin) for t in (k_c, k_s, k_w))
    v_c, v_s, v_w = (t.reshape(B, S, G, Dh) for t in (v_c, v_s, v_w))

    n_cmp = S // CMP_STRIDE - 1

    def compress(t, pos, w1, w2):
        c = t.reshape(B, S // CMP_STRIDE, CMP_STRIDE, G, Dh)
        blk = jnp.concatenate([c[:, :-1], c[:, 1:]], axis=2) + pos[:, None, :]
        flat = blk.transpose(0, 1, 3, 2, 4).reshape(B, n_cmp, G, CMP_BLOCK * Dh)
        return jax.nn.silu(flat @ w1) @ w2

    kc_blk = compress(k_c, cmp_pos_k, cmp_w1_k, cmp_w2_k)
    vc_blk = compress(v_c, cmp_pos_v, cmp_w1_v, cmp_w2_v)
    cmp_end = jnp.arange(n_cmp) * CMP_STRIDE + CMP_BLOCK - 1

    n_slc = S // SLC_BLOCK
    top_k = min(SLC_TOPK, n_slc)
    ratio = SLC_BLOCK // CMP_STRIDE
    i_idx = np.arange(n_cmp)[:, None]
    j_idx = np.arange(n_slc)[None, :]
    agg = jnp.asarray((i_idx >= ratio * j_idx - CMP_BLOCK // CMP_STRIDE + 1)
                      & (i_idx <= ratio * j_idx + ratio - 1), F32)
    ks_blk = k_s.reshape(B, n_slc, SLC_BLOCK, G, Dh).transpose(0, 3, 1, 2, 4)
    vs_blk = v_s.reshape(B, n_slc, SLC_BLOCK, G, Dh).transpose(0, 3, 1, 2, 4)
    gather = jax.vmap(jax.vmap(lambda t, ix: t[ix]))

    kw_pad = jnp.pad(k_w, ((0, 0), (NSA_WINDOW, 0), (0, 0), (0, 0)))
    vw_pad = jnp.pad(v_w, ((0, 0), (NSA_WINDOW, 0), (0, 0), (0, 0)))
    scale = Dh ** -0.5
    n_qb = S // QB
    q_blocks = q.reshape(B, n_qb, QB, G, R, Dh).transpose(1, 0, 2, 3, 4, 5)
    g_blocks = jax.nn.sigmoid(gate_logit.astype(F32)).reshape(B, n_qb, QB, 3, H).transpose(1, 0, 2, 3, 4)
    j = jnp.arange(n_slc)
    w_off = jnp.arange(QB + NSA_WINDOW)

    def attend_block(args):
        blk, qb, gb = args
        start = blk * QB
        t_pos = start + jnp.arange(QB)
        s_c = jnp.einsum('bqgrd,bngd->bgrqn', qb, kc_blk) * scale
        p_c = _masked_softmax(s_c, cmp_end[None, :] <= t_pos[:, None])
        o_cmp = jnp.einsum('bgrqn,bngd->bqgrd', p_c, vc_blk)
        imp = jnp.einsum('bgrqn,nj->bgqj', p_c, agg)
        cur = t_pos[:, None] // SLC_BLOCK
        allowed = j[None, :] * SLC_BLOCK <= t_pos[:, None]
        forced = (j[None, :] == 0) | (j[None, :] == cur) | (j[None, :] == cur - 1)
        imp = jnp.where(allowed, imp + jnp.where(forced, FORCE_BONUS, 0.0), -jnp.inf)
        _, sel = lax.top_k(imp, top_k)
        k_sel = gather(ks_blk, sel)
        v_sel = gather(vs_blk, sel)
        kpos = sel[..., None] * SLC_BLOCK + jnp.arange(SLC_BLOCK)
        m_s = (kpos <= t_pos[:, None, None])[:, :, None]
        s_s = jnp.einsum('bqgrd,bgqkld->bgrqkl', qb, k_sel) * scale
        p_s = _masked_softmax(s_s.reshape(B, G, R, QB, -1), m_s.reshape(B, G, 1, QB, -1))
        o_slc = jnp.einsum('bgrqkl,bgqkld->bqgrd', p_s.reshape(s_s.shape), v_sel)
        kw = lax.dynamic_slice_in_dim(kw_pad, start, QB + NSA_WINDOW, axis=1)
        vw = lax.dynamic_slice_in_dim(vw_pad, start, QB + NSA_WINDOW, axis=1)
        wpos = start - NSA_WINDOW + w_off
        diff = t_pos[:, None] - wpos[None, :]
        m_w = (diff >= 0) & (diff < NSA_WINDOW) & (wpos[None, :] >= 0)
        s_w = jnp.einsum('bqgrd,bkgd->bgrqk', qb, kw) * scale
        o_win = jnp.einsum('bgrqk,bkgd->bqgrd', _masked_softmax(s_w, m_w), vw)
        gate = lambda i: gb[:, :, i].reshape(B, QB, G, R, 1)
        o = gate(0) * o_cmp + gate(1) * o_slc + gate(2) * o_win
        return o.reshape(B, QB, H * Dh)

    o = lax.map(attend_block, (jnp.arange(n_qb), q_blocks, g_blocks))
    o = o.transpose(1, 0, 2, 3).reshape(B, S, H * Dh)
    return (o * jax.nn.silu(z.astype(F32))).astype(x.dtype) @ w_out


def _swa_sink_mixer(x, cos, sin, w_in, sinks, w_out):
    B, S, D = x.shape
    H, KV, Dh, W = SWA_HEADS, SWA_KV_HEADS, SWA_HEAD_DIM, SWA_WINDOW
    R = H // KV
    n_blk = S // W
    q, k, v, z = jnp.split(x @ w_in, [H * Dh, H * Dh + KV * Dh, H * Dh + 2 * KV * Dh], axis=-1)
    q = _partial_rotary(q.reshape(B, S, H, Dh), cos, sin).reshape(B, n_blk, W, KV, R, Dh)
    k = _partial_rotary(k.reshape(B, S, KV, Dh), cos, sin).reshape(B, n_blk, W, KV, Dh)
    v = v.reshape(B, n_blk, W, KV, Dh)

    def with_prev(t):
        prev = jnp.concatenate([jnp.zeros_like(t[:, :1]), t[:, :-1]], axis=1)
        return jnp.concatenate([prev, t], axis=2)

    kk, vv = with_prev(k), with_prev(v)
    q_off = jnp.arange(W)[:, None]
    k_off = jnp.arange(2 * W)[None, :] - W
    diff = q_off - k_off
    band = (diff >= 0) & (diff < W)
    real = (jnp.arange(n_blk) > 0)[:, None, None] | (k_off >= 0)[None]
    mask = (band[None] & real)[None, :, None, None]
    s = jnp.einsum('bnqgrd,bnkgd->bngrqk', q, kk).astype(F32) * (Dh ** -0.5)
    s = jnp.where(mask, s, -jnp.inf)
    sink = jnp.broadcast_to(sinks.astype(F32).reshape(1, 1, KV, R, 1, 1), s.shape[:-1] + (1,))
    p = jax.nn.softmax(jnp.concatenate([s, sink], axis=-1), axis=-1)[..., :-1]
    o = jnp.einsum('bngrqk,bnkgd->bnqgrd', p, vv).reshape(B, S, H * Dh)
    return (o * jax.nn.silu(z.astype(F32))).astype(x.dtype) @ w_out


def setup_inputs(seed: int = 0) -> dict:
    key = jax.random.key(seed)
    ks = iter(jax.random.split(key, 48))
    D = D_MODEL

    def dense(fan_in, fan_out, scale=1.0):
        return jax.random.normal(next(ks), (fan_in, fan_out), F32) * (fan_in ** -0.5) * scale

    def gain(n):
        return 1.0 + 0.02 * jax.random.normal(next(ks), (n,), F32)

    def small(shape):
        return 0.02 * jax.random.normal(next(ks), shape, F32)

    x = jax.random.normal(next(ks), (BATCH, SEQ, D), F32)
    positions = (jax.random.randint(next(ks), (BATCH, 1), 0, 4096, dtype=jnp.int32)
                 + jnp.arange(SEQ, dtype=jnp.int32)[None, :])
    hgrn_lb_logits = 0.5 * jax.random.normal(next(ks), (N_HGRN_LAYERS, D), F32)
    cmp_in = CMP_BLOCK * NSA_HEAD_DIM
    return {
        "x": x,
        "positions": positions,
        "hgrn_lb_logits": hgrn_lb_logits,
        "l0_w_in": dense(D, HG_IN),
        "l0_g_norm": gain(D),
        "l0_w_out": dense(D, D, DEEPNORM_BETA),
        "l0_ln_g": gain(D),
        "l0_ln_b": small((D,)),
        "l1_w_in": dense(D, NSA_IN),
        "l1_cmp_pos_k": small((CMP_BLOCK, NSA_HEAD_DIM)),
        "l1_cmp_w1_k": dense(cmp_in, CMP_HIDDEN),
        "l1_cmp_w2_k": dense(CMP_HIDDEN, NSA_HEAD_DIM),
        "l1_cmp_pos_v": small((CMP_BLOCK, NSA_HEAD_DIM)),
        "l1_cmp_w1_v": dense(cmp_in, CMP_HIDDEN),
        "l1_cmp_w2_v": dense(CMP_HIDDEN, NSA_HEAD_DIM),
        "l1_w_out": dense(D, D, DEEPNORM_BETA),
        "l1_ln_g": gain(D),
        "l1_ln_b": small((D,)),
        "l2_w_in": dense(D, SWA_IN),
        "l2_sinks": 0.5 * jax.random.normal(next(ks), (SWA_HEADS,), F32),
        "l2_w_out": dense(D, D, DEEPNORM_BETA),
        "l2_ln_g": gain(D),
        "l2_ln_b": small((D,)),
        "l3_w_in": dense(D, HG_IN),
        "l3_g_norm": gain(D),
        "l3_w_out": dense(D, D, DEEPNORM_BETA),
        "l3_ln_g": gain(D),
        "l3_ln_b": small((D,)),
    }


def reference(x, positions, hgrn_lb_logits,
              l0_w_in, l0_g_norm, l0_w_out, l0_ln_g, l0_ln_b,
              l1_w_in, l1_cmp_pos_k, l1_cmp_w1_k, l1_cmp_w2_k,
              l1_cmp_pos_v, l1_cmp_w1_v, l1_cmp_w2_v, l1_w_out, l1_ln_g, l1_ln_b,
              l2_w_in, l2_sinks, l2_w_out, l2_ln_g, l2_ln_b,
              l3_w_in, l3_g_norm, l3_w_out, l3_ln_g, l3_ln_b):
    layer_params = (
        (l0_w_in, l0_g_norm, l0_w_out, l0_ln_g, l0_ln_b),
        (l1_w_in, l1_cmp_pos_k, l1_cmp_w1_k, l1_cmp_w2_k,
         l1_cmp_pos_v, l1_cmp_w1_v, l1_cmp_w2_v, l1_w_out, l1_ln_g, l1_ln_b),
        (l2_w_in, l2_sinks, l2_w_out, l2_ln_g, l2_ln_b),
        (l3_w_in, l3_g_norm, l3_w_out, l3_ln_g, l3_ln_b),
    )
    lower_bounds = _hgrn_lower_bounds(hgrn_lb_logits)
    cos_nsa, sin_nsa = _rotary_tables(positions, NSA_HEAD_DIM)
    cos_swa, sin_swa = _rotary_tables(positions, SWA_HEAD_DIM)
    h = x
    for i in range(DEPTH):
        p = layer_params[i]
        kind = i % N_MIXERS
        if kind == 0:
            y = _hgrn2_mixer(h, p[0], p[1], p[2], lower_bounds[i // N_MIXERS])
        elif kind == 1:
            y = _nsa_mixer(h, cos_nsa, sin_nsa, *p[:-2])
        else:
            y = _swa_sink_mixer(h, cos_swa, sin_swa, *p[:-2])
        h = _layer_norm(DEEPNORM_ALPHA * h + y, p[-2], p[-1])
    return h
```

```python
import functools

import numpy as np
import jax
import jax.numpy as jnp
from jax import lax
from jax.experimental import pallas as pl
from jax.experimental.pallas import tpu as pltpu

F32 = jnp.float32
BF16 = jnp.bfloat16

D_MODEL = 2048
DEPTH = 4
N_MIXERS = 3
DEEPNORM_ALPHA = (2 * DEPTH) ** 0.25
LN_EPS = 1e-5
RMS_EPS = 1e-6
ROPE_THETA = 500000.0
ROPE_FRACTION = 4

HG_HEAD_DIM = 128
HG_HEADS = D_MODEL // HG_HEAD_DIM
HG_CHUNK = 64
HG_SUB = 16
HG_TIME_BLOCK = 512

NSA_HEAD_DIM = 128
NSA_HEADS = D_MODEL // NSA_HEAD_DIM
NSA_KV_GROUPS = 4
NSA_REP = NSA_HEADS // NSA_KV_GROUPS
CMP_BLOCK = 32
CMP_STRIDE = 16
CMP_HIDDEN = 256
SLC_BLOCK = 64
SLC_TOPK = 16
NSA_WINDOW = 512
NSA_TQ = 256
FORCE_BONUS = 1.0e4
MASK_NEG = -1.0e30

SWA_HEAD_DIM = 64
SWA_HEADS = D_MODEL // SWA_HEAD_DIM
SWA_KV_HEADS = 4
SWA_REP = SWA_HEADS // SWA_KV_HEADS
SWA_WINDOW = 128

LANES = 128
VMEM_LIMIT_BYTES = 48 * 1024 * 1024

_NT = (((1,), (1,)), ((), ()))
_TN = (((0,), (0,)), ((), ()))


def _params(*sem):
    return pltpu.CompilerParams(dimension_semantics=sem, vmem_limit_bytes=VMEM_LIMIT_BYTES)


def _silu(x):
    return x * (1.0 / (1.0 + jnp.exp(-x)))


def _mm_kernel(*refs, rope_half, rope_period, head_major, n_chunks):
    if rope_half:
        x_ref, w_ref, c_ref, s_ref, o_ref = refs
    else:
        x_ref, w_ref, o_ref = refs
    acc = jnp.dot(x_ref[...], w_ref[...], preferred_element_type=F32)
    if rope_half:
        cos = c_ref[...]
        sin = s_ref[...]
        lane = lax.broadcasted_iota(jnp.int32, cos.shape, 1)
        first_half = (lane & (rope_period - 1)) < rope_half
    for j in range(n_chunks):
        a = acc[:, j * LANES:(j + 1) * LANES]
        if rope_half:
            up = pltpu.roll(a, LANES - rope_half, 1)
            dn = pltpu.roll(a, rope_half, 1)
            a = a * cos + jnp.where(first_half, up, dn) * sin
        if head_major:
            o_ref[j] = a.astype(o_ref.dtype)
        else:
            o_ref[:, j * LANES:(j + 1) * LANES] = a.astype(o_ref.dtype)


def _mm(x, w, *, out_dtype, rope=None, head_major=False, tn=512):
    S, K = x.shape
    N = w.shape[1]
    tm = min(1024, S)
    tn = min(tn, N)
    n_chunks = tn // LANES
    in_specs = [pl.BlockSpec((tm, K), lambda i, j: (i, 0)),
                pl.BlockSpec((K, tn), lambda i, j: (0, j))]
    args = [x, w]
    half = period = 0
    if rope is not None:
        cos_t, sin_t, half, period = rope
        in_specs += [pl.BlockSpec((tm, LANES), lambda i, j: (i, 0))] * 2
        args += [cos_t, sin_t]
    if head_major:
        out_shape = jax.ShapeDtypeStruct((N // LANES, S, LANES), out_dtype)
        out_spec = pl.BlockSpec((n_chunks, tm, LANES), lambda i, j: (j, i, 0))
    else:
        out_shape = jax.ShapeDtypeStruct((S, N), out_dtype)
        out_spec = pl.BlockSpec((tm, tn), lambda i, j: (i, j))
    return pl.pallas_call(
        functools.partial(_mm_kernel, rope_half=half, rope_period=period,
                          head_major=head_major, n_chunks=n_chunks),
        grid=(S // tm, N // tn),
        in_specs=in_specs,
        out_specs=out_spec,
        out_shape=out_shape,
        compiler_params=_params("parallel", "arbitrary"),
        name="in_proj",
    )(*args)


def _outproj_ln_kernel(a_ref, w_ref, h_ref, g_ref, b_ref, o_ref, ob_ref):
    y = jnp.dot(a_ref[...], w_ref[...], preferred_element_type=F32)
    u = DEEPNORM_ALPHA * h_ref[...] + y
    mu = jnp.mean(u, axis=-1, keepdims=True)
    xc = u - mu
    var = jnp.mean(xc * xc, axis=-1, keepdims=True)
    out = xc * lax.rsqrt(var + LN_EPS) * g_ref[...] + b_ref[...]
    o_ref[...] = out
    ob_ref[...] = out.astype(BF16)


def _outproj_ln(a, w, h, g, b):
    S, D = h.shape
    tm = min(256, S)
    row = pl.BlockSpec((tm, D), lambda i: (i, 0))
    vec = pl.BlockSpec((1, D), lambda i: (0, 0))
    return pl.pallas_call(
        _outproj_ln_kernel,
        grid=(S // tm,),
        in_specs=[row, pl.BlockSpec((D, D), lambda i: (0, 0)), row, vec, vec],
        out_specs=[row, row],
        out_shape=[jax.ShapeDtypeStruct((S, D), F32), jax.ShapeDtypeStruct((S, D), BF16)],
        compiler_params=_params("parallel"),
        name="outproj_ln",
    )(a, w, h, g.reshape(1, D), b.reshape(1, D))


def _hgrn_kernel(q_ref, f_ref, v_ref, z_ref, llb_ref, l1m_ref, gn_ref, esum_ref,
                 o_ref, st_ref, ycat_ref, *, n_chunks):
    C, SUB, DK = HG_CHUNK, HG_SUB, HG_HEAD_DIM
    n_sub = C // SUB

    @pl.when(pl.program_id(1) == 0)
    def _():
        st_ref[...] = jnp.zeros_like(st_ref)

    r = lax.broadcasted_iota(jnp.int32, (C, C), 0)
    c = lax.broadcasted_iota(jnp.int32, (C, C), 1)
    same_sub = (r // SUB) == (c // SUB)
    causal = c <= r
    cum_mat = jnp.concatenate([
        causal.astype(F32), (causal & same_sub).astype(F32),
        jnp.ones((C, C), F32), same_sub.astype(F32)], axis=0)
    diag_mask = causal & same_sub
    row_k = lax.broadcasted_iota(jnp.int32, (C, DK), 0)
    log_lb = llb_ref[...]
    log_1m_lb = l1m_ref[...]
    g_norm = gn_ref[...]
    esum = esum_ref[...]

    def chunk(ci, carry):
        rows = pl.ds(pl.multiple_of(ci * C, C), C)
        q = q_ref[rows, :]
        fl = f_ref[rows, :]
        v = v_ref[rows, :]
        z = z_ref[rows, :]
        log_sig = jnp.minimum(fl, 0.0) - jnp.log1p(jnp.exp(-jnp.abs(fl)))
        t = log_1m_lb + log_sig
        log_f = jnp.maximum(log_lb, t) + jnp.log1p(jnp.exp(-jnp.abs(log_lb - t)))
        k = jnp.exp(t - fl)
        cums = jnp.dot(cum_mat, log_f, precision=lax.Precision.HIGHEST,
                       preferred_element_type=F32)
        b_full = cums[0:C]
        b_loc = cums[C:2 * C]
        b_tot = cums[2 * C:3 * C]
        b_loc_tot = cums[3 * C:4 * C]
        b_start = b_full - b_loc

        for i in range(n_sub):
            sl = slice(i * SUB, (i + 1) * SUB)
            bi, qi, ki = b_loc[sl], q[sl], k[sl]
            for s in range(SUB):
                y = qi * jnp.exp(jnp.minimum(bi - bi[s:s + 1, :], 0.0)) * ki[s:s + 1, :]
                ycat_ref[sl, s * DK:(s + 1) * DK] = y.astype(BF16)
        a_diag = jnp.dot(ycat_ref[...], esum, preferred_element_type=F32)

        qt = q * jnp.exp(b_loc)
        kt = k * jnp.exp(b_loc_tot - b_loc)
        lhs, rhs = [], []
        for j in range(n_sub - 1):
            b_next = jnp.tile(b_start[(j + 1) * SUB:(j + 1) * SUB + 8], (C // 8, 1))
            e = jnp.exp(jnp.minimum(b_start - b_next, 0.0))
            lhs.append(jnp.where(row_k >= (j + 1) * SUB, qt * e, 0.0).astype(BF16))
            rhs.append(jnp.where((row_k >= j * SUB) & (row_k < (j + 1) * SUB), kt, 0.0).astype(BF16))
        a_off = lax.dot_general(jnp.concatenate(lhs, axis=1), jnp.concatenate(rhs, axis=1), _NT,
                                preferred_element_type=F32)
        a = jnp.where(diag_mask, a_diag, 0.0) + a_off

        st = st_ref[...]
        qe = (q * jnp.exp(b_full)).astype(BF16)
        o = lax.dot_general(qe, st.astype(BF16), _NT, preferred_element_type=F32)
        o = o + jnp.dot(a.astype(BF16), v.astype(BF16), preferred_element_type=F32)
        kd = (k * jnp.exp(b_tot - b_full)).astype(BF16)
        decay = jnp.exp(jnp.tile(b_tot[0:8], (DK // 8, 1)))
        st_ref[...] = st * decay + lax.dot_general(v.astype(BF16), kd, _TN,
                                                   preferred_element_type=F32)

        ms = jnp.mean(o * o, axis=-1, keepdims=True)
        out = o * lax.rsqrt(ms + RMS_EPS) * g_norm * _silu(z)
        o_ref[rows, :] = out.astype(o_ref.dtype)
        return carry

    lax.fori_loop(0, n_chunks, chunk, 0)


def _hgrn_rec(proj, log_lb, log_1m_lb, g_norm, esum):
    S = proj.shape[0]
    D, DK, H = D_MODEL, HG_HEAD_DIM, HG_HEADS
    T = min(HG_TIME_BLOCK, S)

    def col(off):
        return pl.BlockSpec((T, DK), lambda h, t, off=off: (t, off + h))

    vec = pl.BlockSpec((1, DK), lambda h, t: (0, h))
    return pl.pallas_call(
        functools.partial(_hgrn_kernel, n_chunks=T // HG_CHUNK),
        grid=(H, S // T),
        in_specs=[col(0), col(H), col(2 * H), col(3 * H), vec, vec, vec,
                  pl.BlockSpec(esum.shape, lambda h, t: (0, 0))],
        out_specs=pl.BlockSpec((T, DK), lambda h, t: (t, h)),
        out_shape=jax.ShapeDtypeStruct((S, D), BF16),
        scratch_shapes=[pltpu.VMEM((DK, DK), F32),
                        pltpu.VMEM((HG_CHUNK, HG_SUB * DK), BF16)],
        compiler_params=_params("parallel", "arbitrary"),
        name="hgrn_rec",
    )(proj, proj, proj, proj, log_lb.reshape(1, D), log_1m_lb.reshape(1, D),
      g_norm.reshape(1, D), esum)


def _nsa_compress_kernel(x_ref, pos_ref, w1_ref, w2_ref, o_ref):
    half = CMP_STRIDE * NSA_HEAD_DIM
    x = x_ref[0]
    pos = pos_ref[0]
    x1 = (x + pos[:, :half]).astype(BF16)
    x2 = (x + pos[:, half:]).astype(BF16)
    h1 = jnp.dot(x1, w1_ref[0, :half, :], preferred_element_type=F32)
    h2 = jnp.dot(x2, w1_ref[0, half:, :], preferred_element_type=F32)
    n = h2.shape[0]
    hid = h1 + pltpu.roll(h2, n - 1, 0)
    o_ref[0] = jnp.dot(_silu(hid).astype(BF16), w2_ref[0],
                       preferred_element_type=F32).astype(o_ref.dtype)


def _nsa_compress(kv_c, pos, w1, w2):
    G = NSA_KV_GROUPS
    n, width = kv_c.shape[1], kv_c.shape[2]
    return pl.pallas_call(
        _nsa_compress_kernel,
        grid=(2, G),
        in_specs=[pl.BlockSpec((1, n, width), lambda a, g: (a * G + g, 0, 0)),
                  pl.BlockSpec((1, 1, 2 * width), lambda a, g: (a, 0, 0)),
                  pl.BlockSpec((1, 2 * width, CMP_HIDDEN), lambda a, g: (a, 0, 0)),
                  pl.BlockSpec((1, CMP_HIDDEN, NSA_HEAD_DIM), lambda a, g: (a, 0, 0))],
        out_specs=pl.BlockSpec((1, n, NSA_HEAD_DIM), lambda a, g: (a * G + g, 0, 0)),
        out_shape=jax.ShapeDtypeStruct((2 * G, n, NSA_HEAD_DIM), BF16),
        compiler_params=_params("parallel", "parallel"),
        name="nsa_compress",
    )(kv_c, pos, w1, w2)


def _nsa_kernel(q_ref, kc_ref, vc_ref, ks_ref, vs_ref, kw_ref, vw_ref, oh_ref, agg_ref,
                gate_ref, z_ref, o_ref, m_ref, acc_ref, *, tq):
    R, DH = NSA_REP, NSA_HEAD_DIM
    rows = R * tq
    qb = pl.program_id(1)
    start = qb * tq
    q4 = q_ref[...].reshape(rows, DH)
    ncp = kc_ref.shape[1]

    t_c = start + lax.broadcasted_iota(jnp.int32, (tq, ncp), 0)
    n_c = lax.broadcasted_iota(jnp.int32, (tq, ncp), 1)
    cmask = (n_c * CMP_STRIDE + CMP_BLOCK - 1) <= t_c
    kc = kc_ref[0]
    vc = vc_ref[0]
    p_sum = jnp.zeros((tq, ncp), F32)
    o_cmp = []
    for r in range(R):
        s = lax.dot_general(q4[r * tq:(r + 1) * tq], kc, _NT, preferred_element_type=F32)
        s = jnp.where(cmask, s, -jnp.inf)
        m = jnp.max(s, axis=1, keepdims=True)
        m = jnp.where(m == -jnp.inf, 0.0, m)
        e = jnp.exp(s - m)
        p = e * (1.0 / jnp.maximum(jnp.sum(e, axis=1, keepdims=True), 1e-30))
        o_cmp.append(jnp.dot(p.astype(BF16), vc, preferred_element_type=F32))
        p_sum = p_sum + p

    agg = agg_ref[...]
    p_hi = p_sum.astype(BF16)
    p_lo = (p_sum - p_hi.astype(F32)).astype(BF16)
    imp = (jnp.dot(p_hi, agg, preferred_element_type=F32)
           + jnp.dot(p_lo, agg, preferred_element_type=F32))
    t_s = start + lax.broadcasted_iota(jnp.int32, (tq, LANES), 0)
    j_s = lax.broadcasted_iota(jnp.int32, (tq, LANES), 1)
    cur = lax.shift_right_logical(t_s, 6)
    forced = (j_s == 0) | (j_s == cur) | (j_s == cur - 1)
    val = jnp.where(j_s * SLC_BLOCK <= t_s, imp + jnp.where(forced, FORCE_BONUS, 0.0), -jnp.inf)
    lane_f = j_s.astype(F32)
    sel = jnp.zeros((tq, LANES), F32)
    for _ in range(SLC_TOPK):
        mx = jnp.max(val, axis=1, keepdims=True)
        idx = jnp.min(jnp.where(val == mx, lane_f, float(LANES)), axis=1, keepdims=True)
        pick = lane_f == idx
        sel = jnp.where(pick, 1.0, sel)
        val = jnp.where(pick, -jnp.inf, val)
    m_neg = jnp.where(sel > 0.0, 0.0, MASK_NEG).astype(BF16)
    q_aug = jnp.concatenate([q4, jnp.concatenate([m_neg] * R, axis=0)], axis=1)

    ones_v = jnp.ones((tq, DH), BF16)
    i_t = lax.broadcasted_iota(jnp.int32, (tq, tq), 0)
    c_t = lax.broadcasted_iota(jnp.int32, (tq, tq), 1)
    causal = jnp.concatenate([c_t <= i_t] * R, axis=0)
    win_old = jnp.concatenate([(2 * tq + i_t - c_t) < NSA_WINDOW] * R, axis=0)

    def flash_init():
        m_ref[...] = jnp.full(m_ref.shape, -jnp.inf, F32)
        acc_ref[...] = jnp.zeros_like(acc_ref)

    def flash_update(s, v_tile):
        m_prev = m_ref[...]
        m_next = jnp.maximum(m_prev, jnp.max(s, axis=1, keepdims=True))
        alpha = jnp.exp(m_prev - m_next)
        p = jnp.exp(s - jnp.concatenate([m_next, m_next], axis=1))
        v_aug = jnp.concatenate([v_tile, ones_v], axis=1)
        acc_ref[...] = (acc_ref[...] * jnp.concatenate([alpha, alpha], axis=1)
                        + jnp.dot(p.astype(BF16), v_aug, preferred_element_type=F32))
        m_ref[...] = m_next

    def flash_result():
        acc = acc_ref[...]
        return acc[:, :DH] * (1.0 / acc[:, DH:])

    def tile(ref, kt):
        return ref[0, pl.ds(pl.multiple_of(kt * tq, tq), tq), :]

    def slc_scores(kt):
        oh = oh_ref[pl.ds(pl.multiple_of(kt * tq, tq), tq), :]
        k_aug = jnp.concatenate([tile(ks_ref, kt), oh], axis=1)
        return lax.dot_general(q_aug, k_aug, _NT, preferred_element_type=F32)

    def win_scores(kt):
        return lax.dot_general(q4, tile(kw_ref, kt), _NT, preferred_element_type=F32)

    flash_init()

    def slc_body(kt, carry):
        flash_update(slc_scores(kt), tile(vs_ref, kt))
        return carry

    lax.fori_loop(0, qb, slc_body, 0)
    flash_update(jnp.where(causal, slc_scores(qb), MASK_NEG), tile(vs_ref, qb))
    o_slc = flash_result()

    flash_init()

    @pl.when(qb >= 2)
    def _():
        flash_update(jnp.where(win_old, win_scores(qb - 2), MASK_NEG), tile(vw_ref, qb - 2))

    @pl.when(qb >= 1)
    def _():
        flash_update(win_scores(qb - 1), tile(vw_ref, qb - 1))

    flash_update(jnp.where(causal, win_scores(qb), MASK_NEG), tile(vw_ref, qb))
    o_win = flash_result()

    gate = 1.0 / (1.0 + jnp.exp(-gate_ref[...]))
    for r in range(R):
        sl = slice(r * tq, (r + 1) * tq)
        o = (gate[:, r:r + 1] * o_cmp[r] + gate[:, R + r:R + r + 1] * o_slc[sl]
             + gate[:, 2 * R + r:2 * R + r + 1] * o_win[sl])
        z = z_ref[:, r * DH:(r + 1) * DH]
        o_ref[:, r * DH:(r + 1) * DH] = (o * _silu(z)).astype(o_ref.dtype)


def _nsa_attn(q, kvc, ksw, vsw, onehot, agg, gate, z):
    G, R, DH = NSA_KV_GROUPS, NSA_REP, NSA_HEAD_DIM
    S = q.shape[1]
    tq = NSA_TQ
    assert S % tq == 0 and NSA_WINDOW <= 2 * tq and S // SLC_BLOCK <= LANES
    ncp = kvc.shape[1]
    rows = R * tq
    whole = lambda off: pl.BlockSpec((1, S, DH), lambda g, i, off=off: (off + g, 0, 0))
    cmp_spec = lambda off: pl.BlockSpec((1, ncp, DH), lambda g, i, off=off: (off + g, 0, 0))
    return pl.pallas_call(
        functools.partial(_nsa_kernel, tq=tq),
        grid=(G, S // tq),
        in_specs=[pl.BlockSpec((R, tq, DH), lambda g, i: (g, i, 0)),
                  cmp_spec(0), cmp_spec(G),
                  whole(0), whole(0), whole(G), whole(G),
                  pl.BlockSpec((S, LANES), lambda g, i: (0, 0)),
                  pl.BlockSpec((ncp, LANES), lambda g, i: (0, 0)),
                  pl.BlockSpec((tq, LANES), lambda g, i: (i, g)),
                  pl.BlockSpec((tq, R * DH), lambda g, i: (i, g))],
        out_specs=pl.BlockSpec((tq, R * DH), lambda g, i: (i, g)),
        out_shape=jax.ShapeDtypeStruct((S, D_MODEL), BF16),
        scratch_shapes=[pltpu.VMEM((rows, LANES), F32), pltpu.VMEM((rows, 2 * DH), F32)],
        compiler_params=_params("parallel", "arbitrary"),
        name="nsa_attn",
    )(q, kvc, kvc, ksw, vsw, ksw, vsw, onehot, agg, gate, z)


def _swa_kernel(sink_ref, q_ref, kp_ref, kc_ref, vp_ref, vc_ref, z_ref, o_ref):
    W, R = SWA_WINDOW, SWA_REP
    n = pl.program_id(0)
    i_q = lax.broadcasted_iota(jnp.int32, (W, 2 * W), 0)
    c_k = lax.broadcasted_iota(jnp.int32, (W, 2 * W), 1)
    diff = i_q - (c_k - W)
    mask = (diff >= 0) & (diff < W) & ((n > 0) | (c_k >= W))
    lane = lax.broadcasted_iota(jnp.int32, (W, LANES), 1)
    low = lane < SWA_HEAD_DIM
    for g in range(SWA_KV_HEADS):
        gl = slice(g * LANES, (g + 1) * LANES)
        kk = jnp.concatenate([kp_ref[:, gl], kc_ref[:, gl]], axis=0)
        vv = jnp.concatenate([vp_ref[:, gl], vc_ref[:, gl]], axis=0)
        qs = []
        for r in range(R):
            h = g * R + r
            q2 = q_ref[:, (h // 2) * LANES:(h // 2 + 1) * LANES]
            qs.append(jnp.where(low if h % 2 == 0 else ~low, q2, jnp.zeros_like(q2)))
        s_all = lax.dot_general(jnp.concatenate(qs, axis=0), kk, _NT,
                                preferred_element_type=F32)
        ps, inv_l = [], []
        for r in range(R):
            sink = sink_ref[g * R + r]
            s = jnp.where(mask, s_all[r * W:(r + 1) * W], -jnp.inf)
            m = jnp.maximum(jnp.max(s, axis=1, keepdims=True), sink)
            e = jnp.exp(s - m)
            inv_l.append(1.0 / (jnp.sum(e, axis=1, keepdims=True) + jnp.exp(sink - m)))
            ps.append(e.astype(BF16))
        o_all = jnp.dot(jnp.concatenate(ps, axis=0), vv, preferred_element_type=F32)
        for pr in range(R // 2):
            o_even = o_all[(2 * pr) * W:(2 * pr + 1) * W] * inv_l[2 * pr]
            o_odd = o_all[(2 * pr + 1) * W:(2 * pr + 2) * W] * inv_l[2 * pr + 1]
            col = slice((g * R // 2 + pr) * LANES, (g * R // 2 + pr + 1) * LANES)
            o_ref[:, col] = (jnp.where(low, o_even, o_odd) * _silu(z_ref[:, col])).astype(o_ref.dtype)


def _swa_attn(sinks, q, k_dup, v_dup, z):
    S, D = q.shape
    W = SWA_WINDOW
    wide = SWA_KV_HEADS * LANES
    prev = lambda i: (jnp.maximum(i - 1, 0), 0)
    cur = lambda i: (i, 0)
    return pl.pallas_call(
        _swa_kernel,
        grid=(S // W,),
        in_specs=[pl.BlockSpec(memory_space=pltpu.SMEM),
                  pl.BlockSpec((W, D), cur),
                  pl.BlockSpec((W, wide), prev), pl.BlockSpec((W, wide), cur),
                  pl.BlockSpec((W, wide), prev), pl.BlockSpec((W, wide), cur),
                  pl.BlockSpec((W, D), cur)],
        out_specs=pl.BlockSpec((W, D), cur),
        out_shape=jax.ShapeDtypeStruct((S, D), BF16),
        compiler_params=_params("parallel"),
        name="swa_attn",
    )(sinks, q, k_dup, k_dup, v_dup, v_dup, z)


def _rope_tables(positions, head_dim, scale):
    rot = head_dim // ROPE_FRACTION
    half = rot // 2
    inv_freq = ROPE_THETA ** (-jnp.arange(0, rot, 2, dtype=F32) / rot)
    ang = positions.reshape(-1).astype(F32)[:, None] * inv_freq
    cos, sin = jnp.cos(ang), jnp.sin(ang)
    S = cos.shape[0]
    rest = head_dim - rot
    cos_h = jnp.concatenate([cos, cos, jnp.ones((S, rest), F32)], axis=1)
    sin_h = jnp.concatenate([-sin, sin, jnp.zeros((S, rest), F32)], axis=1)
    reps = LANES // head_dim
    return (jnp.tile(cos_h, (1, reps)) * scale, jnp.tile(sin_h, (1, reps)) * scale, half, head_dim)


def _hgrn_esum():
    row_s = np.arange(HG_SUB * HG_HEAD_DIM) // HG_HEAD_DIM
    col_s = np.arange(HG_CHUNK) % HG_SUB
    return jnp.asarray(row_s[:, None] == col_s[None, :], dtype=BF16)


def _nsa_constants(S):
    n_cmp = S // CMP_STRIDE - 1
    ncp = S // CMP_STRIDE
    n_slc = S // SLC_BLOCK
    ratio = SLC_BLOCK // CMP_STRIDE
    i = np.arange(ncp)[:, None]
    j = np.arange(LANES)[None, :]
    agg = ((i >= ratio * j - CMP_BLOCK // CMP_STRIDE + 1) & (i <= ratio * j + ratio - 1)
           & (i < n_cmp) & (j < n_slc))
    onehot = (np.arange(S)[:, None] // SLC_BLOCK) == j
    return jnp.asarray(onehot, dtype=BF16), jnp.asarray(agg, dtype=BF16)


def _hgrn_layer(hb, w_in, g_norm, log_lb, log_1m_lb, esum):
    proj = _mm(hb, w_in.astype(BF16), out_dtype=F32)
    return _hgrn_rec(proj, log_lb, log_1m_lb, g_norm, esum)


def _nsa_layer(hb, positions, w_in, pos_k, w1_k, w2_k, pos_v, w1_v, w2_v):
    S = hb.shape[0]
    H, G, R, DH = NSA_HEADS, NSA_KV_GROUPS, NSA_REP, NSA_HEAD_DIM
    gw = G * DH
    o_q, o_kc, o_vc, o_ks, o_vs, o_kw, o_vw = (0, H * DH, H * DH + gw, H * DH + 2 * gw,
                                               H * DH + 3 * gw, H * DH + 4 * gw, H * DH + 5 * gw)
    o_gate = H * DH + 6 * gw
    o_z = o_gate + 3 * H
    wb = w_in.astype(BF16)
    seg = lambda off, width: wb[:, off:off + width]
    rope_q = _rope_tables(positions, DH, DH ** -0.5)
    rope_k = _rope_tables(positions, DH, 1.0)

    q = _mm(hb, seg(o_q, H * DH), out_dtype=BF16, rope=rope_q, head_major=True)
    k_c = _mm(hb, seg(o_kc, gw), out_dtype=F32, rope=rope_k, head_major=True)
    v_c = _mm(hb, seg(o_vc, gw), out_dtype=F32, head_major=True)
    ksw = _mm(hb, jnp.concatenate([seg(o_ks, gw), seg(o_kw, gw)], axis=1), out_dtype=BF16,
              rope=rope_k, head_major=True)
    vsw = _mm(hb, jnp.concatenate([seg(o_vs, gw), seg(o_vw, gw)], axis=1), out_dtype=BF16,
              head_major=True)
    z = _mm(hb, seg(o_z, H * DH), out_dtype=F32)
    src = np.full(G * LANES, 3 * H, dtype=np.int32)
    for g in range(G):
        for b in range(3):
            for r in range(R):
                src[g * LANES + b * R + r] = b * H + g * R + r
    w_gate = jnp.concatenate([seg(o_gate, 3 * H), jnp.zeros((w_in.shape[0], 1), BF16)], axis=1)[:, src]
    gate = _mm(hb, w_gate, out_dtype=F32)

    n_grp = S // CMP_STRIDE
    kv_c = jnp.concatenate([k_c, v_c], axis=0).reshape(2 * G, n_grp, CMP_STRIDE * DH)
    pos = jnp.stack([pos_k, pos_v]).reshape(2, 1, CMP_BLOCK * DH)
    kvc_blk = _nsa_compress(kv_c, pos, jnp.stack([w1_k, w1_v]).astype(BF16),
                            jnp.stack([w2_k, w2_v]).astype(BF16))
    onehot, agg = _nsa_constants(S)
    return _nsa_attn(q, kvc_blk, ksw, vsw, onehot, agg, gate, z)


def _swa_layer(hb, positions, w_in, sinks):
    H, KV, DH = SWA_HEADS, SWA_KV_HEADS, SWA_HEAD_DIM
    wb = w_in.astype(BF16)
    o_k = H * DH
    o_v = o_k + KV * DH
    o_z = o_v + KV * DH
    dup = np.concatenate([np.tile(np.arange(g * DH, (g + 1) * DH), 2) for g in range(KV)])
    rope_q = _rope_tables(positions, DH, DH ** -0.5)
    rope_k = _rope_tables(positions, DH, 1.0)
    q = _mm(hb, wb[:, :o_k], out_dtype=BF16, rope=rope_q)
    k_dup = _mm(hb, wb[:, o_k:o_v][:, dup], out_dtype=BF16, rope=rope_k)
    v_dup = _mm(hb, wb[:, o_v:o_z][:, dup], out_dtype=BF16)
    z = _mm(hb, wb[:, o_z:], out_dtype=F32)
    return _swa_attn(sinks, q, k_dup, v_dup, z)


def kernel(x, positions, hgrn_lb_logits, l0_w_in, l0_g_norm, l0_w_out, l0_ln_g, l0_ln_b, l1_w_in, l1_cmp_pos_k, l1_cmp_w1_k, l1_cmp_w2_k, l1_cmp_pos_v, l1_cmp_w1_v, l1_cmp_w2_v, l1_w_out, l1_ln_g, l1_ln_b, l2_w_in, l2_sinks, l2_w_out, l2_ln_g, l2_ln_b, l3_w_in, l3_g_norm, l3_w_out, l3_ln_g, l3_ln_b):
    B, S, D = x.shape
    assert B == 1 and D == D_MODEL
    lb = jnp.cumsum(jax.nn.softmax(hgrn_lb_logits.astype(F32), axis=0), axis=0)
    lb = lb - lb[0:1]
    log_lb, log_1m_lb = jnp.log(lb), jnp.log1p(-lb)
    esum = _hgrn_esum()

    h = x.reshape(S, D)
    hb = h.astype(BF16)

    a = _hgrn_layer(hb, l0_w_in, l0_g_norm, log_lb[0], log_1m_lb[0], esum)
    h, hb = _outproj_ln(a, l0_w_out.astype(BF16), h, l0_ln_g, l0_ln_b)

    a = _nsa_layer(hb, positions, l1_w_in, l1_cmp_pos_k, l1_cmp_w1_k, l1_cmp_w2_k,
                   l1_cmp_pos_v, l1_cmp_w1_v, l1_cmp_w2_v)
    h, hb = _outproj_ln(a, l1_w_out.astype(BF16), h, l1_ln_g, l1_ln_b)

    a = _swa_layer(hb, positions, l2_w_in, l2_sinks)
    h, hb = _outproj_ln(a, l2_w_out.astype(BF16), h, l2_ln_g, l2_ln_b)

    a = _hgrn_layer(hb, l3_w_in, l3_g_norm, log_lb[1], log_1m_lb[1], esum)
    h, hb = _outproj_ln(a, l3_w_out.astype(BF16), h, l3_ln_g, l3_ln_b)
    return h.reshape(B, S, D)
```

```python
import functools

import numpy as np
import jax
import jax.numpy as jnp
from jax import lax
from jax.experimental import pallas as pl
from jax.experimental.pallas import tpu as pltpu

F32 = jnp.float32
BF16 = jnp.bfloat16

D_MODEL = 2048
DEPTH = 4
N_MIXERS = 3
DEEPNORM_ALPHA = (2 * DEPTH) ** 0.25
LN_EPS = 1e-5
RMS_EPS = 1e-6
ROPE_THETA = 500000.0
ROPE_FRACTION = 4

HG_HEAD_DIM = 128
HG_HEADS = D_MODEL // HG_HEAD_DIM
HG_CHUNK = 64
HG_SUB = 16
HG_TIME_BLOCK = 512
LOG2_E = 1.4426950408889634

NSA_HEAD_DIM = 128
NSA_HEADS = D_MODEL // NSA_HEAD_DIM
NSA_KV_GROUPS = 4
NSA_REP = NSA_HEADS // NSA_KV_GROUPS
CMP_BLOCK = 32
CMP_STRIDE = 16
CMP_HIDDEN = 256
SLC_BLOCK = 64
SLC_TOPK = 16
NSA_WINDOW = 512
NSA_TQ = 256
FORCE_BONUS = 1.0e4
MASK_NEG = -1.0e30

SWA_HEAD_DIM = 64
SWA_HEADS = D_MODEL // SWA_HEAD_DIM
SWA_KV_HEADS = 4
SWA_REP = SWA_HEADS // SWA_KV_HEADS
SWA_WINDOW = 128

LANES = 128
VMEM_LIMIT_BYTES = 48 * 1024 * 1024

_NT = (((1,), (1,)), ((), ()))
_TN = (((0,), (0,)), ((), ()))


def _params(*sem):
    return pltpu.CompilerParams(dimension_semantics=sem, vmem_limit_bytes=VMEM_LIMIT_BYTES)


def _silu(x):
    return x * (1.0 / (1.0 + jnp.exp(-x)))


def _mm_kernel(*refs, rope_half, rope_period, head_major, n_chunks):
    if rope_half:
        x_ref, w_ref, c_ref, s_ref, o_ref = refs
    else:
        x_ref, w_ref, o_ref = refs
    acc = jnp.dot(x_ref[...], w_ref[...], preferred_element_type=F32)
    if rope_half:
        cos = c_ref[...]
        sin = s_ref[...]
        lane = lax.broadcasted_iota(jnp.int32, cos.shape, 1)
        first_half = (lane & (rope_period - 1)) < rope_half
    for j in range(n_chunks):
        a = acc[:, j * LANES:(j + 1) * LANES]
        if rope_half:
            up = pltpu.roll(a, LANES - rope_half, 1)
            dn = pltpu.roll(a, rope_half, 1)
            a = a * cos + jnp.where(first_half, up, dn) * sin
        if head_major:
            o_ref[j] = a.astype(o_ref.dtype)
        else:
            o_ref[:, j * LANES:(j + 1) * LANES] = a.astype(o_ref.dtype)


def _mm(x, w, *, out_dtype, rope=None, head_major=False, tn=512):
    S, K = x.shape
    N = w.shape[1]
    tm = min(1024, S)
    tn = min(tn, N)
    n_chunks = tn // LANES
    in_specs = [pl.BlockSpec((tm, K), lambda i, j: (i, 0)),
                pl.BlockSpec((K, tn), lambda i, j: (0, j))]
    args = [x, w]
    half = period = 0
    if rope is not None:
        cos_t, sin_t, half, period = rope
        in_specs += [pl.BlockSpec((tm, LANES), lambda i, j: (i, 0))] * 2
        args += [cos_t, sin_t]
    if head_major:
        out_shape = jax.ShapeDtypeStruct((N // LANES, S, LANES), out_dtype)
        out_spec = pl.BlockSpec((n_chunks, tm, LANES), lambda i, j: (j, i, 0))
    else:
        out_shape = jax.ShapeDtypeStruct((S, N), out_dtype)
        out_spec = pl.BlockSpec((tm, tn), lambda i, j: (i, j))
    return pl.pallas_call(
        functools.partial(_mm_kernel, rope_half=half, rope_period=period,
                          head_major=head_major, n_chunks=n_chunks),
        grid=(S // tm, N // tn),
        in_specs=in_specs,
        out_specs=out_spec,
        out_shape=out_shape,
        compiler_params=_params("parallel", "arbitrary"),
        name="in_proj",
    )(*args)


def _outproj_ln_kernel(a_ref, w_ref, h_ref, g_ref, b_ref, o_ref, ob_ref):
    y = jnp.dot(a_ref[...], w_ref[...], preferred_element_type=F32)
    u = DEEPNORM_ALPHA * h_ref[...] + y
    mu = jnp.mean(u, axis=-1, keepdims=True)
    xc = u - mu
    var = jnp.mean(xc * xc, axis=-1, keepdims=True)
    out = xc * lax.rsqrt(var + LN_EPS) * g_ref[...] + b_ref[...]
    o_ref[...] = out
    ob_ref[...] = out.astype(BF16)


def _outproj_ln(a, w, h, g, b):
    S, D = h.shape
    tm = min(256, S)
    row = pl.BlockSpec((tm, D), lambda i: (i, 0))
    vec = pl.BlockSpec((1, D), lambda i: (0, 0))
    return pl.pallas_call(
        _outproj_ln_kernel,
        grid=(S // tm,),
        in_specs=[row, pl.BlockSpec((D, D), lambda i: (0, 0)), row, vec, vec],
        out_specs=[row, row],
        out_shape=[jax.ShapeDtypeStruct((S, D), F32), jax.ShapeDtypeStruct((S, D), BF16)],
        compiler_params=_params("parallel"),
        name="outproj_ln",
    )(a, w, h, g.reshape(1, D), b.reshape(1, D))


def _hgrn_kernel(q_ref, f_ref, v_ref, z_ref, llb_ref, l1m_ref, gn_ref, esum_ref,
                 o_ref, st_ref, ycat_ref, u_ref, *, n_chunks):
    C, SUB, DK = HG_CHUNK, HG_SUB, HG_HEAD_DIM
    n_sub = C // SUB

    @pl.when(pl.program_id(1) == 0)
    def _():
        st_ref[...] = jnp.zeros_like(st_ref)

    r = lax.broadcasted_iota(jnp.int32, (C, C), 0)
    c = lax.broadcasted_iota(jnp.int32, (C, C), 1)
    same_sub = (r // SUB) == (c // SUB)
    causal = c <= r
    cum_one = jnp.concatenate([
        causal.astype(F32), (causal & same_sub).astype(F32),
        jnp.ones((C, C), F32), same_sub.astype(F32)], axis=0).astype(BF16)
    cum_mat = jnp.concatenate([cum_one] * 3, axis=1)
    diag_mask = causal & same_sub
    row_k = lax.broadcasted_iota(jnp.int32, (C, DK), 0)
    log_lb = llb_ref[...]
    log_1m_lb = l1m_ref[...]
    chunks = [slice(ci * C, (ci + 1) * C) for ci in range(n_chunks)]

    def rows_to_lanes(x):
        return jnp.concatenate([x[sl] for sl in chunks], axis=1)

    def lanes_to_rows(x):
        return jnp.concatenate([x[:, ci * DK:(ci + 1) * DK] for ci in range(n_chunks)], axis=0)

    q = q_ref[...]
    fl = f_ref[...]
    vb = v_ref[...].astype(BF16)
    log_sig = jnp.minimum(fl, 0.0) - jnp.log(1.0 + jnp.exp(-jnp.abs(fl)))
    t = log_1m_lb + log_sig
    log_f = jnp.maximum(log_lb, t) + jnp.log(1.0 + jnp.exp(-jnp.abs(log_lb - t)))
    log2_k = (t - fl) * LOG2_E
    k = jnp.exp2(log2_k)
    lf = rows_to_lanes(log_f)
    lf_hi = lf.astype(BF16)
    lf_r1 = lf - lf_hi.astype(F32)
    lf_mid = lf_r1.astype(BF16)
    lf_lo = (lf_r1 - lf_mid.astype(F32)).astype(BF16)
    cums = jnp.dot(cum_mat, jnp.concatenate([lf_hi, lf_mid, lf_lo], axis=0),
                   preferred_element_type=F32) * LOG2_E
    b_full = lanes_to_rows(cums[0:C])
    b_loc = lanes_to_rows(cums[C:2 * C])
    b_tot = lanes_to_rows(cums[2 * C:3 * C])
    b_loc_tot = lanes_to_rows(cums[3 * C:4 * C])
    b_start = b_full - b_loc
    qe = (q * jnp.exp2(b_full)).astype(BF16)
    kd = (k * jnp.exp2(b_tot - b_full)).astype(BF16)
    qt = q * jnp.exp2(b_loc)
    kt = k * jnp.exp2(b_loc_tot - b_loc)

    o_intra, a_offs = [], []
    for ci, ch in enumerate(chunks):
        for i in range(n_sub):
            sl = slice(ci * C + i * SUB, ci * C + (i + 1) * SUB)
            bi, qi, lki = b_loc[sl], q[sl], log2_k[sl]
            c_row = bi - lki
            for s in range(SUB):
                lo = 8 * (s // 8)
                y = qi[lo:] * jnp.exp2(jnp.minimum(bi[lo:] - c_row[s:s + 1, :], lki[s:s + 1, :]))
                if lo:
                    y = jnp.concatenate([jnp.zeros((lo, DK), F32), y], axis=0)
                ycat_ref[sl, s * DK:(s + 1) * DK] = y.astype(BF16)
        bs_c, qt_c, kt_c = b_start[ch], qt[ch], kt[ch]
        lhs, rhs = [], []
        for j in range(n_sub - 1):
            b_next = jnp.tile(bs_c[(j + 1) * SUB:(j + 1) * SUB + 8], (C // 8, 1))
            e = jnp.exp2(jnp.minimum(bs_c - b_next, 0.0))
            lhs.append(jnp.where(row_k >= (j + 1) * SUB, qt_c * e, 0.0).astype(BF16))
            rhs.append(jnp.where((row_k >= j * SUB) & (row_k < (j + 1) * SUB), kt_c, 0.0).astype(BF16))
        a_offs.append(lax.dot_general(jnp.concatenate(lhs, axis=1), jnp.concatenate(rhs, axis=1),
                                      _NT, preferred_element_type=F32))
        u_ref[ci] = lax.dot_general(vb[ch], kd[ch], _TN, preferred_element_type=F32)
    a_diag = jnp.dot(ycat_ref[...], esum_ref[...], preferred_element_type=F32)
    for ci, ch in enumerate(chunks):
        a = jnp.where(diag_mask, a_diag[ch], 0.0) + a_offs[ci]
        o_intra.append(jnp.dot(a.astype(BF16), vb[ch], preferred_element_type=F32))

    st = st_ref[...]
    outs = []
    for ci, ch in enumerate(chunks):
        outs.append(o_intra[ci] + lax.dot_general(qe[ch], st.astype(BF16), _NT,
                                                  preferred_element_type=F32))
        decay = jnp.exp2(jnp.tile(b_tot[ci * C:ci * C + 8], (DK // 8, 1)))
        st = st * decay + u_ref[ci]
    st_ref[...] = st

    o = jnp.concatenate(outs, axis=0)
    ms = jnp.mean(o * o, axis=-1, keepdims=True)
    out = o * lax.rsqrt(ms + RMS_EPS) * gn_ref[...] * _silu(z_ref[...])
    o_ref[...] = out.astype(o_ref.dtype)


def _hgrn_rec(proj, log_lb, log_1m_lb, g_norm, esum):
    S = proj.shape[0]
    D, DK, H = D_MODEL, HG_HEAD_DIM, HG_HEADS
    T = min(HG_TIME_BLOCK, S)

    def col(off):
        return pl.BlockSpec((T, DK), lambda h, t, off=off: (t, off + h))

    vec = pl.BlockSpec((1, DK), lambda h, t: (0, h))
    return pl.pallas_call(
        functools.partial(_hgrn_kernel, n_chunks=T // HG_CHUNK),
        grid=(H, S // T),
        in_specs=[col(0), col(H), col(2 * H), col(3 * H), vec, vec, vec,
                  pl.BlockSpec(esum.shape, lambda h, t: (0, 0))],
        out_specs=pl.BlockSpec((T, DK), lambda h, t: (t, h)),
        out_shape=jax.ShapeDtypeStruct((S, D), BF16),
        scratch_shapes=[pltpu.VMEM((DK, DK), F32),
                        pltpu.VMEM((T, HG_SUB * DK), BF16),
                        pltpu.VMEM((T // HG_CHUNK, DK, DK), F32)],
        compiler_params=_params("parallel", "arbitrary"),
        name="hgrn_rec",
    )(proj, proj, proj, proj, log_lb.reshape(1, D), log_1m_lb.reshape(1, D),
      g_norm.reshape(1, D), esum)


def _nsa_compress_kernel(x_ref, pos_ref, w1_ref, w2_ref, o_ref):
    half = CMP_STRIDE * NSA_HEAD_DIM
    x = x_ref[0]
    pos = pos_ref[0]
    x1 = (x + pos[:, :half]).astype(BF16)
    x2 = (x + pos[:, half:]).astype(BF16)
    h1 = jnp.dot(x1, w1_ref[0, :half, :], preferred_element_type=F32)
    h2 = jnp.dot(x2, w1_ref[0, half:, :], preferred_element_type=F32)
    n = h2.shape[0]
    hid = h1 + pltpu.roll(h2, n - 1, 0)
    o_ref[0] = jnp.dot(_silu(hid).astype(BF16), w2_ref[0],
                       preferred_element_type=F32).astype(o_ref.dtype)


def _nsa_compress(kv_c, pos, w1, w2):
    G = NSA_KV_GROUPS
    n, width = kv_c.shape[1], kv_c.shape[2]
    return pl.pallas_call(
        _nsa_compress_kernel,
        grid=(2, G),
        in_specs=[pl.BlockSpec((1, n, width), lambda a, g: (a * G + g, 0, 0)),
                  pl.BlockSpec((1, 1, 2 * width), lambda a, g: (a, 0, 0)),
                  pl.BlockSpec((1, 2 * width, CMP_HIDDEN), lambda a, g: (a, 0, 0)),
                  pl.BlockSpec((1, CMP_HIDDEN, NSA_HEAD_DIM), lambda a, g: (a, 0, 0))],
        out_specs=pl.BlockSpec((1, n, NSA_HEAD_DIM), lambda a, g: (a * G + g, 0, 0)),
        out_shape=jax.ShapeDtypeStruct((2 * G, n, NSA_HEAD_DIM), BF16),
        compiler_params=_params("parallel", "parallel"),
        name="nsa_compress",
    )(kv_c, pos, w1, w2)


def _nsa_kernel(q_ref, kc_ref, vc_ref, ks_ref, vs_ref, kw_ref, vw_ref, oh_ref, agg_ref,
                gate_ref, z_ref, o_ref, m_ref, acc_ref, *, tq):
    R, DH = NSA_REP, NSA_HEAD_DIM
    rows = R * tq
    qb = pl.program_id(1)
    start = qb * tq
    q4 = q_ref[...].reshape(rows, DH)
    ncp = kc_ref.shape[1]

    t_c = start + lax.broadcasted_iota(jnp.int32, (tq, ncp), 0)
    n_c = lax.broadcasted_iota(jnp.int32, (tq, ncp), 1)
    cmask = (n_c * CMP_STRIDE + CMP_BLOCK - 1) <= t_c
    kc = kc_ref[0]
    vc = vc_ref[0]
    p_sum = jnp.zeros((tq, ncp), F32)
    o_cmp = []
    for r in range(R):
        s = lax.dot_general(q4[r * tq:(r + 1) * tq], kc, _NT, preferred_element_type=F32)
        s = jnp.where(cmask, s, -jnp.inf)
        m = jnp.max(s, axis=1, keepdims=True)
        m = jnp.where(m == -jnp.inf, 0.0, m)
        e = jnp.exp(s - m)
        p = e * (1.0 / jnp.maximum(jnp.sum(e, axis=1, keepdims=True), 1e-30))
        o_cmp.append(jnp.dot(p.astype(BF16), vc, preferred_element_type=F32))
        p_sum = p_sum + p

    agg = agg_ref[...]
    p_hi = p_sum.astype(BF16)
    p_lo = (p_sum - p_hi.astype(F32)).astype(BF16)
    imp = (jnp.dot(p_hi, agg, preferred_element_type=F32)
           + jnp.dot(p_lo, agg, preferred_element_type=F32))
    t_s = start + lax.broadcasted_iota(jnp.int32, (tq, LANES), 0)
    j_s = lax.broadcasted_iota(jnp.int32, (tq, LANES), 1)
    cur = lax.shift_right_logical(t_s, 6)
    forced = (j_s == 0) | (j_s == cur) | (j_s == cur - 1)
    val = jnp.where(j_s * SLC_BLOCK <= t_s, imp + jnp.where(forced, FORCE_BONUS, 0.0), -jnp.inf)
    lane_f = j_s.astype(F32)
    sel = jnp.zeros((tq, LANES), F32)
    for _ in range(SLC_TOPK):
        mx = jnp.max(val, axis=1, keepdims=True)
        idx = jnp.min(jnp.where(val == mx, lane_f, float(LANES)), axis=1, keepdims=True)
        pick = lane_f == idx
        sel = jnp.where(pick, 1.0, sel)
        val = jnp.where(pick, -jnp.inf, val)
    m_neg = jnp.where(sel > 0.0, 0.0, MASK_NEG).astype(BF16)
    q_aug = jnp.concatenate([q4, jnp.concatenate([m_neg] * R, axis=0)], axis=1)

    ones_v = jnp.ones((tq, DH), BF16)
    i_t = lax.broadcasted_iota(jnp.int32, (tq, tq), 0)
    c_t = lax.broadcasted_iota(jnp.int32, (tq, tq), 1)
    causal = jnp.concatenate([c_t <= i_t] * R, axis=0)
    win_old = jnp.concatenate([(2 * tq + i_t - c_t) < NSA_WINDOW] * R, axis=0)

    def flash_init():
        m_ref[...] = jnp.full(m_ref.shape, -jnp.inf, F32)
        acc_ref[...] = jnp.zeros_like(acc_ref)

    def flash_update(s, v_tile):
        m_prev = m_ref[...]
        m_next = jnp.maximum(m_prev, jnp.max(s, axis=1, keepdims=True))
        alpha = jnp.exp(m_prev - m_next)
        p = jnp.exp(s - jnp.concatenate([m_next, m_next], axis=1))
        v_aug = jnp.concatenate([v_tile, ones_v], axis=1)
        acc_ref[...] = (acc_ref[...] * jnp.concatenate([alpha, alpha], axis=1)
                        + jnp.dot(p.astype(BF16), v_aug, preferred_element_type=F32))
        m_ref[...] = m_next

    def flash_result():
        acc = acc_ref[...]
        return acc[:, :DH] * (1.0 / acc[:, DH:])

    def tile(ref, kt):
        return ref[0, pl.ds(pl.multiple_of(kt * tq, tq), tq), :]

    def slc_scores(kt):
        oh = oh_ref[pl.ds(pl.multiple_of(kt * tq, tq), tq), :]
        k_aug = jnp.concatenate([tile(ks_ref, kt), oh], axis=1)
        return lax.dot_general(q_aug, k_aug, _NT, preferred_element_type=F32)

    def win_scores(kt):
        return lax.dot_general(q4, tile(kw_ref, kt), _NT, preferred_element_type=F32)

    flash_init()

    def slc_body(kt, carry):
        flash_update(slc_scores(kt), tile(vs_ref, kt))
        return carry

    lax.fori_loop(0, qb, slc_body, 0)
    flash_update(jnp.where(causal, slc_scores(qb), MASK_NEG), tile(vs_ref, qb))
    o_slc = flash_result()

    flash_init()

    @pl.when(qb >= 2)
    def _():
        flash_update(jnp.where(win_old, win_scores(qb - 2), MASK_NEG), tile(vw_ref, qb - 2))

    @pl.when(qb >= 1)
    def _():
        flash_update(win_scores(qb - 1), tile(vw_ref, qb - 1))

    flash_update(jnp.where(causal, win_scores(qb), MASK_NEG), tile(vw_ref, qb))
    o_win = flash_result()

    gate = 1.0 / (1.0 + jnp.exp(-gate_ref[...]))
    for r in range(R):
        sl = slice(r * tq, (r + 1) * tq)
        o = (gate[:, r:r + 1] * o_cmp[r] + gate[:, R + r:R + r + 1] * o_slc[sl]
             + gate[:, 2 * R + r:2 * R + r + 1] * o_win[sl])
        z = z_ref[:, r * DH:(r + 1) * DH]
        o_ref[:, r * DH:(r + 1) * DH] = (o * _silu(z)).astype(o_ref.dtype)


def _nsa_attn(q, kvc, ksw, vsw, onehot, agg, gate, z):
    G, R, DH = NSA_KV_GROUPS, NSA_REP, NSA_HEAD_DIM
    S = q.shape[1]
    tq = NSA_TQ
    assert S % tq == 0 and NSA_WINDOW <= 2 * tq and S // SLC_BLOCK <= LANES
    ncp = kvc.shape[1]
    rows = R * tq
    whole = lambda off: pl.BlockSpec((1, S, DH), lambda g, i, off=off: (off + g, 0, 0))
    cmp_spec = lambda off: pl.BlockSpec((1, ncp, DH), lambda g, i, off=off: (off + g, 0, 0))
    return pl.pallas_call(
        functools.partial(_nsa_kernel, tq=tq),
        grid=(G, S // tq),
        in_specs=[pl.BlockSpec((R, tq, DH), lambda g, i: (g, i, 0)),
                  cmp_spec(0), cmp_spec(G),
                  whole(0), whole(0), whole(G), whole(G),
                  pl.BlockSpec((S, LANES), lambda g, i: (0, 0)),
                  pl.BlockSpec((ncp, LANES), lambda g, i: (0, 0)),
                  pl.BlockSpec((tq, LANES), lambda g, i: (i, g)),
                  pl.BlockSpec((tq, R * DH), lambda g, i: (i, g))],
        out_specs=pl.BlockSpec((tq, R * DH), lambda g, i: (i, g)),
        out_shape=jax.ShapeDtypeStruct((S, D_MODEL), BF16),
        scratch_shapes=[pltpu.VMEM((rows, LANES), F32), pltpu.VMEM((rows, 2 * DH), F32)],
        compiler_params=_params("parallel", "arbitrary"),
        name="nsa_attn",
    )(q, kvc, kvc, ksw, vsw, ksw, vsw, onehot, agg, gate, z)


def _swa_kernel(sink_ref, q_ref, kp_ref, kc_ref, vp_ref, vc_ref, z_ref, o_ref):
    W, R = SWA_WINDOW, SWA_REP
    n = pl.program_id(0)
    i_q = lax.broadcasted_iota(jnp.int32, (W, 2 * W), 0)
    c_k = lax.broadcasted_iota(jnp.int32, (W, 2 * W), 1)
    diff = i_q - (c_k - W)
    mask = (diff >= 0) & (diff < W) & ((n > 0) | (c_k >= W))
    lane = lax.broadcasted_iota(jnp.int32, (W, LANES), 1)
    low = lane < SWA_HEAD_DIM
    for g in range(SWA_KV_HEADS):
        gl = slice(g * LANES, (g + 1) * LANES)
        kk = jnp.concatenate([kp_ref[:, gl], kc_ref[:, gl]], axis=0)
        vv = jnp.concatenate([vp_ref[:, gl], vc_ref[:, gl]], axis=0)
        qs = []
        for r in range(R):
            h = g * R + r
            q2 = q_ref[:, (h // 2) * LANES:(h // 2 + 1) * LANES]
            qs.append(jnp.where(low if h % 2 == 0 else ~low, q2, jnp.zeros_like(q2)))
        s_all = lax.dot_general(jnp.concatenate(qs, axis=0), kk, _NT,
                                preferred_element_type=F32)
        ps, inv_l = [], []
        for r in range(R):
            sink = sink_ref[g * R + r]
            s = jnp.where(mask, s_all[r * W:(r + 1) * W], -jnp.inf)
            m = jnp.maximum(jnp.max(s, axis=1, keepdims=True), sink)
            e = jnp.exp(s - m)
            inv_l.append(1.0 / (jnp.sum(e, axis=1, keepdims=True) + jnp.exp(sink - m)))
            ps.append(e.astype(BF16))
        o_all = jnp.dot(jnp.concatenate(ps, axis=0), vv, preferred_element_type=F32)
        for pr in range(R // 2):
            o_even = o_all[(2 * pr) * W:(2 * pr + 1) * W] * inv_l[2 * pr]
            o_odd = o_all[(2 * pr + 1) * W:(2 * pr + 2) * W] * inv_l[2 * pr + 1]
            col = slice((g * R // 2 + pr) * LANES, (g * R // 2 + pr + 1) * LANES)
            o_ref[:, col] = (jnp.where(low, o_even, o_odd) * _silu(z_ref[:, col])).astype(o_ref.dtype)


def _swa_attn(sinks, q, k_dup, v_dup, z):
    S, D = q.shape
    W = SWA_WINDOW
    wide = SWA_KV_HEADS * LANES
    prev = lambda i: (jnp.maximum(i - 1, 0), 0)
    cur = lambda i: (i, 0)
    return pl.pallas_call(
        _swa_kernel,
        grid=(S // W,),
        in_specs=[pl.BlockSpec(memory_space=pltpu.SMEM),
                  pl.BlockSpec((W, D), cur),
                  pl.BlockSpec((W, wide), prev), pl.BlockSpec((W, wide), cur),
                  pl.BlockSpec((W, wide), prev), pl.BlockSpec((W, wide), cur),
                  pl.BlockSpec((W, D), cur)],
        out_specs=pl.BlockSpec((W, D), cur),
        out_shape=jax.ShapeDtypeStruct((S, D), BF16),
        compiler_params=_params("parallel"),
        name="swa_attn",
    )(sinks, q, k_dup, k_dup, v_dup, v_dup, z)


def _rope_tables(positions, head_dim, scale):
    rot = head_dim // ROPE_FRACTION
    half = rot // 2
    inv_freq = ROPE_THETA ** (-jnp.arange(0, rot, 2, dtype=F32) / rot)
    ang = positions.reshape(-1).astype(F32)[:, None] * inv_freq
    cos, sin = jnp.cos(ang), jnp.sin(ang)
    S = cos.shape[0]
    rest = head_dim - rot
    cos_h = jnp.concatenate([cos, cos, jnp.ones((S, rest), F32)], axis=1)
    sin_h = jnp.concatenate([-sin, sin, jnp.zeros((S, rest), F32)], axis=1)
    reps = LANES // head_dim
    return (jnp.tile(cos_h, (1, reps)) * scale, jnp.tile(sin_h, (1, reps)) * scale, half, head_dim)


def _hgrn_esum():
    row_s = np.arange(HG_SUB * HG_HEAD_DIM) // HG_HEAD_DIM
    col_s = np.arange(HG_CHUNK) % HG_SUB
    return jnp.asarray(row_s[:, None] == col_s[None, :], dtype=BF16)


def _nsa_constants(S):
    n_cmp = S // CMP_STRIDE - 1
    ncp = S // CMP_STRIDE
    n_slc = S // SLC_BLOCK
    ratio = SLC_BLOCK // CMP_STRIDE
    i = np.arange(ncp)[:, None]
    j = np.arange(LANES)[None, :]
    agg = ((i >= ratio * j - CMP_BLOCK // CMP_STRIDE + 1) & (i <= ratio * j + ratio - 1)
           & (i < n_cmp) & (j < n_slc))
    onehot = (np.arange(S)[:, None] // SLC_BLOCK) == j
    return jnp.asarray(onehot, dtype=BF16), jnp.asarray(agg, dtype=BF16)


def _hgrn_layer(hb, w_in, g_norm, log_lb, log_1m_lb, esum):
    proj = _mm(hb, w_in.astype(BF16), out_dtype=F32)
    return _hgrn_rec(proj, log_lb, log_1m_lb, g_norm, esum)


def _nsa_layer(hb, positions, w_in, pos_k, w1_k, w2_k, pos_v, w1_v, w2_v):
    S = hb.shape[0]
    H, G, R, DH = NSA_HEADS, NSA_KV_GROUPS, NSA_REP, NSA_HEAD_DIM
    gw = G * DH
    o_q, o_kc, o_vc, o_ks, o_vs, o_kw, o_vw = (0, H * DH, H * DH + gw, H * DH + 2 * gw,
                                               H * DH + 3 * gw, H * DH + 4 * gw, H * DH + 5 * gw)
    o_gate = H * DH + 6 * gw
    o_z = o_gate + 3 * H
    wb = w_in.astype(BF16)
    seg = lambda off, width: wb[:, off:off + width]
    rope_q = _rope_tables(positions, DH, DH ** -0.5)
    rope_k = _rope_tables(positions, DH, 1.0)

    q = _mm(hb, seg(o_q, H * DH), out_dtype=BF16, rope=rope_q, head_major=True)
    k_c = _mm(hb, seg(o_kc, gw), out_dtype=F32, rope=rope_k, head_major=True)
    v_c = _mm(hb, seg(o_vc, gw), out_dtype=F32, head_major=True)
    ksw = _mm(hb, jnp.concatenate([seg(o_ks, gw), seg(o_kw, gw)], axis=1), out_dtype=BF16,
              rope=rope_k, head_major=True)
    vsw = _mm(hb, jnp.concatenate([seg(o_vs, gw), seg(o_vw, gw)], axis=1), out_dtype=BF16,
              head_major=True)
    z = _mm(hb, seg(o_z, H * DH), out_dtype=F32)
    src = np.full(G * LANES, 3 * H, dtype=np.int32)
    for g in range(G):
        for b in range(3):
            for r in range(R):
                src[g * LANES + b * R + r] = b * H + g * R + r
    w_gate = jnp.concatenate([seg(o_gate, 3 * H), jnp.zeros((w_in.shape[0], 1), BF16)], axis=1)[:, src]
    gate = _mm(hb, w_gate, out_dtype=F32)

    n_grp = S // CMP_STRIDE
    kv_c = jnp.concatenate([k_c, v_c], axis=0).reshape(2 * G, n_grp, CMP_STRIDE * DH)
    pos = jnp.stack([pos_k, pos_v]).reshape(2, 1, CMP_BLOCK * DH)
    kvc_blk = _nsa_compress(kv_c, pos, jnp.stack([w1_k, w1_v]).astype(BF16),
                            jnp.stack([w2_k, w2_v]).astype(BF16))
    onehot, agg = _nsa_constants(S)
    return _nsa_attn(q, kvc_blk, ksw, vsw, onehot, agg, gate, z)


def _swa_layer(hb, positions, w_in, sinks):
    H, KV, DH = SWA_HEADS, SWA_KV_HEADS, SWA_HEAD_DIM
    wb = w_in.astype(BF16)
    o_k = H * DH
    o_v = o_k + KV * DH
    o_z = o_v + KV * DH
    dup = np.concatenate([np.tile(np.arange(g * DH, (g + 1) * DH), 2) for g in range(KV)])
    rope_q = _rope_tables(positions, DH, DH ** -0.5)
    rope_k = _rope_tables(positions, DH, 1.0)
    q = _mm(hb, wb[:, :o_k], out_dtype=BF16, rope=rope_q)
    k_dup = _mm(hb, wb[:, o_k:o_v][:, dup], out_dtype=BF16, rope=rope_k)
    v_dup = _mm(hb, wb[:, o_v:o_z][:, dup], out_dtype=BF16)
    z = _mm(hb, wb[:, o_z:], out_dtype=F32)
    return _swa_attn(sinks, q, k_dup, v_dup, z)


def kernel(x, positions, hgrn_lb_logits, l0_w_in, l0_g_norm, l0_w_out, l0_ln_g, l0_ln_b, l1_w_in, l1_cmp_pos_k, l1_cmp_w1_k, l1_cmp_w2_k, l1_cmp_pos_v, l1_cmp_w1_v, l1_cmp_w2_v, l1_w_out, l1_ln_g, l1_ln_b, l2_w_in, l2_sinks, l2_w_out, l2_ln_g, l2_ln_b, l3_w_in, l3_g_norm, l3_w_out, l3_ln_g, l3_ln_b):
    B, S, D = x.shape
    assert B == 1 and D == D_MODEL
    lb = jnp.cumsum(jax.nn.softmax(hgrn_lb_logits.astype(F32), axis=0), axis=0)
    lb = lb - lb[0:1]
    log_lb, log_1m_lb = jnp.log(lb), jnp.log1p(-lb)
    esum = _hgrn_esum()

    h = x.reshape(S, D)
    hb = h.astype(BF16)

    a = _hgrn_layer(hb, l0_w_in, l0_g_norm, log_lb[0], log_1m_lb[0], esum)
    h, hb = _outproj_ln(a, l0_w_out.astype(BF16), h, l0_ln_g, l0_ln_b)

    a = _nsa_layer(hb, positions, l1_w_in, l1_cmp_pos_k, l1_cmp_w1_k, l1_cmp_w2_k,
                   l1_cmp_pos_v, l1_cmp_w1_v, l1_cmp_w2_v)
    h, hb = _outproj_ln(a, l1_w_out.astype(BF16), h, l1_ln_g, l1_ln_b)

    a = _swa_layer(hb, positions, l2_w_in, l2_sinks)
    h, hb = _outproj_ln(a, l2_w_out.astype(BF16), h, l2_ln_g, l2_ln_b)

    a = _hgrn_layer(hb, l3_w_in, l3_g_norm, log_lb[1], log_1m_lb[1], esum)
    h, hb = _outproj_ln(a, l3_w_out.astype(BF16), h, l3_ln_g, l3_ln_b)
    return h.reshape(B, S, D)
```

```python
import functools

import numpy as np
import jax
import jax.numpy as jnp
from jax import lax
from jax.experimental import pallas as pl
from jax.experimental.pallas import tpu as pltpu

F32 = jnp.float32
BF16 = jnp.bfloat16

D_MODEL = 2048
DEPTH = 4
N_MIXERS = 3
DEEPNORM_ALPHA = (2 * DEPTH) ** 0.25
LN_EPS = 1e-5
RMS_EPS = 1e-6
ROPE_THETA = 500000.0
ROPE_FRACTION = 4

HG_HEAD_DIM = 128
HG_HEADS = D_MODEL // HG_HEAD_DIM
HG_CHUNK = 64
HG_SUB = 16
HG_TIME_BLOCK = 512
LOG2_E = 1.4426950408889634

NSA_HEAD_DIM = 128
NSA_HEADS = D_MODEL // NSA_HEAD_DIM
NSA_KV_GROUPS = 4
NSA_REP = NSA_HEADS // NSA_KV_GROUPS
CMP_BLOCK = 32
CMP_STRIDE = 16
CMP_HIDDEN = 256
SLC_BLOCK = 64
SLC_TOPK = 16
NSA_WINDOW = 512
NSA_TQ = 256
NSA_ONES_ROWS = 16
FORCE_BONUS = 1.0e4
MASK_NEG = -1.0e30

SWA_HEAD_DIM = 64
SWA_HEADS = D_MODEL // SWA_HEAD_DIM
SWA_KV_HEADS = 4
SWA_REP = SWA_HEADS // SWA_KV_HEADS
SWA_WINDOW = 128

LANES = 128
VMEM_LIMIT_BYTES = 48 * 1024 * 1024

_NT = (((1,), (1,)), ((), ()))
_TN = (((0,), (0,)), ((), ()))


def _params(*sem):
    return pltpu.CompilerParams(dimension_semantics=sem, vmem_limit_bytes=VMEM_LIMIT_BYTES)


def _silu(x):
    return x * (1.0 / (1.0 + jnp.exp(-x)))


def _mm_kernel(*refs, rope_half, rope_period, head_major, n_chunks):
    if rope_half:
        x_ref, w_ref, c_ref, s_ref, o_ref = refs
    else:
        x_ref, w_ref, o_ref = refs
    acc = jnp.dot(x_ref[...], w_ref[...], preferred_element_type=F32)
    if rope_half:
        cos = c_ref[...]
        sin = s_ref[...]
        lane = lax.broadcasted_iota(jnp.int32, cos.shape, 1)
        first_half = (lane & (rope_period - 1)) < rope_half
    for j in range(n_chunks):
        a = acc[:, j * LANES:(j + 1) * LANES]
        if rope_half:
            up = pltpu.roll(a, LANES - rope_half, 1)
            dn = pltpu.roll(a, rope_half, 1)
            a = a * cos + jnp.where(first_half, up, dn) * sin
        if head_major:
            o_ref[j] = a.astype(o_ref.dtype)
        else:
            o_ref[:, j * LANES:(j + 1) * LANES] = a.astype(o_ref.dtype)


def _mm(x, w, *, out_dtype, rope=None, head_major=False, tn=512):
    S, K = x.shape
    N = w.shape[1]
    tm = min(1024, S)
    tn = min(tn, N)
    n_chunks = tn // LANES
    in_specs = [pl.BlockSpec((tm, K), lambda i, j: (i, 0)),
                pl.BlockSpec((K, tn), lambda i, j: (0, j))]
    args = [x, w]
    half = period = 0
    if rope is not None:
        cos_t, sin_t, half, period = rope
        in_specs += [pl.BlockSpec((tm, LANES), lambda i, j: (i, 0))] * 2
        args += [cos_t, sin_t]
    if head_major:
        out_shape = jax.ShapeDtypeStruct((N // LANES, S, LANES), out_dtype)
        out_spec = pl.BlockSpec((n_chunks, tm, LANES), lambda i, j: (j, i, 0))
    else:
        out_shape = jax.ShapeDtypeStruct((S, N), out_dtype)
        out_spec = pl.BlockSpec((tm, tn), lambda i, j: (i, j))
    return pl.pallas_call(
        functools.partial(_mm_kernel, rope_half=half, rope_period=period,
                          head_major=head_major, n_chunks=n_chunks),
        grid=(S // tm, N // tn),
        in_specs=in_specs,
        out_specs=out_spec,
        out_shape=out_shape,
        compiler_params=_params("parallel", "arbitrary"),
        name="in_proj",
    )(*args)


def _mm_t_kernel(*refs, rope_half, scale, n_heads, n_tiles, tk):
    if rope_half:
        wt_ref, x_ref, c_ref, s_ref, o_ref = refs
    else:
        wt_ref, x_ref, o_ref = refs
    acc = lax.dot_general(wt_ref[...], x_ref[...], _NT, preferred_element_type=F32)
    for c in range(n_heads):
        a = acc[c * LANES:(c + 1) * LANES]
        if scale != 1.0:
            a = a * scale
        if rope_half:
            rot = 2 * rope_half
            swapped = jnp.concatenate([a[rope_half:rot], a[:rope_half]], axis=0)
            a = jnp.concatenate([a[:rot] * c_ref[...] + swapped * s_ref[...], a[rot:]], axis=0)
        for b in range(n_tiles):
            o_ref[c, b] = a[:, b * tk:(b + 1) * tk].astype(o_ref.dtype)


def _mm_t(x, wt, *, out_dtype, tk, rope=None, scale=1.0):
    S, K = x.shape
    N = wt.shape[0]
    tm = min(1024, S)
    tn = min(512, N)
    in_specs = [pl.BlockSpec((tn, K), lambda i, j: (j, 0)),
                pl.BlockSpec((tm, K), lambda i, j: (i, 0))]
    args = [wt, x]
    half = 0
    if rope is not None:
        cos_t, sin_t, half = rope
        in_specs += [pl.BlockSpec((2 * half, tm), lambda i, j: (0, i))] * 2
        args += [cos_t, sin_t]
    return pl.pallas_call(
        functools.partial(_mm_t_kernel, rope_half=half, scale=scale, n_heads=tn // LANES,
                          n_tiles=tm // tk, tk=tk),
        grid=(S // tm, N // tn),
        in_specs=in_specs,
        out_specs=pl.BlockSpec((tn // LANES, tm // tk, LANES, tk), lambda i, j: (j, i, 0, 0)),
        out_shape=jax.ShapeDtypeStruct((N // LANES, S // tk, LANES, tk), out_dtype),
        compiler_params=_params("parallel", "arbitrary"),
        name="in_proj_t",
    )(*args)


def _outproj_ln_kernel(a_ref, w_ref, h_ref, g_ref, b_ref, o_ref, ob_ref):
    y = jnp.dot(a_ref[...], w_ref[...], preferred_element_type=F32)
    u = DEEPNORM_ALPHA * h_ref[...] + y
    mu = jnp.mean(u, axis=-1, keepdims=True)
    xc = u - mu
    var = jnp.mean(xc * xc, axis=-1, keepdims=True)
    out = xc * lax.rsqrt(var + LN_EPS) * g_ref[...] + b_ref[...]
    o_ref[...] = out
    ob_ref[...] = out.astype(BF16)


def _outproj_ln(a, w, h, g, b):
    S, D = h.shape
    tm = min(256, S)
    row = pl.BlockSpec((tm, D), lambda i: (i, 0))
    vec = pl.BlockSpec((1, D), lambda i: (0, 0))
    return pl.pallas_call(
        _outproj_ln_kernel,
        grid=(S // tm,),
        in_specs=[row, pl.BlockSpec((D, D), lambda i: (0, 0)), row, vec, vec],
        out_specs=[row, row],
        out_shape=[jax.ShapeDtypeStruct((S, D), F32), jax.ShapeDtypeStruct((S, D), BF16)],
        compiler_params=_params("parallel"),
        name="outproj_ln",
    )(a, w, h, g.reshape(1, D), b.reshape(1, D))


def _hgrn_kernel(q_ref, f_ref, v_ref, z_ref, llb_ref, l1m_ref, gn_ref, esum_ref,
                 o_ref, st_ref, ycat_ref, u_ref, *, n_chunks):
    C, SUB, DK = HG_CHUNK, HG_SUB, HG_HEAD_DIM
    n_sub = C // SUB

    @pl.when(pl.program_id(1) == 0)
    def _():
        st_ref[...] = jnp.zeros_like(st_ref)

    r = lax.broadcasted_iota(jnp.int32, (C, C), 0)
    c = lax.broadcasted_iota(jnp.int32, (C, C), 1)
    same_sub = (r // SUB) == (c // SUB)
    causal = c <= r
    cum_one = jnp.concatenate([
        causal.astype(F32), (causal & same_sub).astype(F32),
        jnp.ones((C, C), F32), same_sub.astype(F32)], axis=0).astype(BF16)
    cum_mat = jnp.concatenate([cum_one] * 3, axis=1)
    diag_mask = causal & same_sub
    row_k = lax.broadcasted_iota(jnp.int32, (C, DK), 0)
    log_lb = llb_ref[...]
    log_1m_lb = l1m_ref[...]
    chunks = [slice(ci * C, (ci + 1) * C) for ci in range(n_chunks)]

    def rows_to_lanes(x):
        return jnp.concatenate([x[sl] for sl in chunks], axis=1)

    def lanes_to_rows(x):
        return jnp.concatenate([x[:, ci * DK:(ci + 1) * DK] for ci in range(n_chunks)], axis=0)

    q = q_ref[...]
    fl = f_ref[...]
    vb = v_ref[...].astype(BF16)
    log_sig = jnp.minimum(fl, 0.0) - jnp.log(1.0 + jnp.exp(-jnp.abs(fl)))
    t = log_1m_lb + log_sig
    log_f = jnp.maximum(log_lb, t) + jnp.log(1.0 + jnp.exp(-jnp.abs(log_lb - t)))
    log2_k = (t - fl) * LOG2_E
    k = jnp.exp2(log2_k)
    lf = rows_to_lanes(log_f)
    lf_hi = lf.astype(BF16)
    lf_r1 = lf - lf_hi.astype(F32)
    lf_mid = lf_r1.astype(BF16)
    lf_lo = (lf_r1 - lf_mid.astype(F32)).astype(BF16)
    cums = jnp.dot(cum_mat, jnp.concatenate([lf_hi, lf_mid, lf_lo], axis=0),
                   preferred_element_type=F32) * LOG2_E
    b_full = lanes_to_rows(cums[0:C])
    b_loc = lanes_to_rows(cums[C:2 * C])
    b_tot = lanes_to_rows(cums[2 * C:3 * C])
    b_loc_tot = lanes_to_rows(cums[3 * C:4 * C])
    b_start = b_full - b_loc
    qe = (q * jnp.exp2(b_full)).astype(BF16)
    kd = (k * jnp.exp2(b_tot - b_full)).astype(BF16)
    qt = q * jnp.exp2(b_loc)
    kt = k * jnp.exp2(b_loc_tot - b_loc)

    o_intra, a_offs = [], []
    for ci, ch in enumerate(chunks):
        for i in range(n_sub):
            sl = slice(ci * C + i * SUB, ci * C + (i + 1) * SUB)
            bi, qi, lki = b_loc[sl], q[sl], log2_k[sl]
            c_row = bi - lki
            for s in range(SUB):
                lo = 8 * (s // 8)
                y = qi[lo:] * jnp.exp2(jnp.minimum(bi[lo:] - c_row[s:s + 1, :], lki[s:s + 1, :]))
                if lo:
                    y = jnp.concatenate([jnp.zeros((lo, DK), F32), y], axis=0)
                ycat_ref[sl, s * DK:(s + 1) * DK] = y.astype(BF16)
        bs_c, qt_c, kt_c = b_start[ch], qt[ch], kt[ch]
        lhs, rhs = [], []
        for j in range(n_sub - 1):
            b_next = jnp.tile(bs_c[(j + 1) * SUB:(j + 1) * SUB + 8], (C // 8, 1))
            e = jnp.exp2(jnp.minimum(bs_c - b_next, 0.0))
            lhs.append(jnp.where(row_k >= (j + 1) * SUB, qt_c * e, 0.0).astype(BF16))
            rhs.append(jnp.where((row_k >= j * SUB) & (row_k < (j + 1) * SUB), kt_c, 0.0).astype(BF16))
        a_offs.append(lax.dot_general(jnp.concatenate(lhs, axis=1), jnp.concatenate(rhs, axis=1),
                                      _NT, preferred_element_type=F32))
        u_ref[ci] = lax.dot_general(vb[ch], kd[ch], _TN, preferred_element_type=F32)
    a_diag = jnp.dot(ycat_ref[...], esum_ref[...], preferred_element_type=F32)
    for ci, ch in enumerate(chunks):
        a = jnp.where(diag_mask, a_diag[ch], 0.0) + a_offs[ci]
        o_intra.append(jnp.dot(a.astype(BF16), vb[ch], preferred_element_type=F32))

    st = st_ref[...]
    outs = []
    for ci, ch in enumerate(chunks):
        outs.append(o_intra[ci] + lax.dot_general(qe[ch], st.astype(BF16), _NT,
                                                  preferred_element_type=F32))
        decay = jnp.exp2(jnp.tile(b_tot[ci * C:ci * C + 8], (DK // 8, 1)))
        st = st * decay + u_ref[ci]
    st_ref[...] = st

    o = jnp.concatenate(outs, axis=0)
    ms = jnp.mean(o * o, axis=-1, keepdims=True)
    out = o * lax.rsqrt(ms + RMS_EPS) * gn_ref[...] * _silu(z_ref[...])
    o_ref[...] = out.astype(o_ref.dtype)


def _hgrn_rec(proj, log_lb, log_1m_lb, g_norm, esum):
    S = proj.shape[0]
    D, DK, H = D_MODEL, HG_HEAD_DIM, HG_HEADS
    T = min(HG_TIME_BLOCK, S)

    def col(off):
        return pl.BlockSpec((T, DK), lambda h, t, off=off: (t, off + h))

    vec = pl.BlockSpec((1, DK), lambda h, t: (0, h))
    return pl.pallas_call(
        functools.partial(_hgrn_kernel, n_chunks=T // HG_CHUNK),
        grid=(H, S // T),
        in_specs=[col(0), col(H), col(2 * H), col(3 * H), vec, vec, vec,
                  pl.BlockSpec(esum.shape, lambda h, t: (0, 0))],
        out_specs=pl.BlockSpec((T, DK), lambda h, t: (t, h)),
        out_shape=jax.ShapeDtypeStruct((S, D), BF16),
        scratch_shapes=[pltpu.VMEM((DK, DK), F32),
                        pltpu.VMEM((T, HG_SUB * DK), BF16),
                        pltpu.VMEM((T // HG_CHUNK, DK, DK), F32)],
        compiler_params=_params("parallel", "arbitrary"),
        name="hgrn_rec",
    )(proj, proj, proj, proj, log_lb.reshape(1, D), log_1m_lb.reshape(1, D),
      g_norm.reshape(1, D), esum)


def _nsa_compress_kernel(x_ref, pos_ref, w1_ref, w2_ref, w2t_ref, o_ref, ot_ref):
    half = CMP_STRIDE * NSA_HEAD_DIM
    x = x_ref[0]
    pos = pos_ref[0]
    x1 = (x + pos[:, :half]).astype(BF16)
    x2 = (x + pos[:, half:]).astype(BF16)
    h1 = jnp.dot(x1, w1_ref[0, :half, :], preferred_element_type=F32)
    h2 = jnp.dot(x2, w1_ref[0, half:, :], preferred_element_type=F32)
    n = h2.shape[0]
    hid = h1 + pltpu.roll(h2, n - 1, 0)
    act = _silu(hid).astype(BF16)
    o_ref[0] = jnp.dot(act, w2_ref[0], preferred_element_type=F32).astype(o_ref.dtype)
    ot_ref[0] = lax.dot_general(w2t_ref[0], act, _NT,
                                preferred_element_type=F32).astype(ot_ref.dtype)


def _nsa_compress(kv_c, pos, w1, w2, w2t):
    G = NSA_KV_GROUPS
    n, width = kv_c.shape[1], kv_c.shape[2]
    per_group = lambda a, g: (a * G + g, 0, 0)
    per_kind = lambda a, g: (a, 0, 0)
    return pl.pallas_call(
        _nsa_compress_kernel,
        grid=(2, G),
        in_specs=[pl.BlockSpec((1, n, width), per_group),
                  pl.BlockSpec((1, 1, 2 * width), per_kind),
                  pl.BlockSpec((1, 2 * width, CMP_HIDDEN), per_kind),
                  pl.BlockSpec((1, CMP_HIDDEN, NSA_HEAD_DIM), per_kind),
                  pl.BlockSpec((1, NSA_HEAD_DIM, CMP_HIDDEN), per_kind)],
        out_specs=[pl.BlockSpec((1, n, NSA_HEAD_DIM), per_group),
                   pl.BlockSpec((1, NSA_HEAD_DIM, n), per_group)],
        out_shape=[jax.ShapeDtypeStruct((2 * G, n, NSA_HEAD_DIM), BF16),
                   jax.ShapeDtypeStruct((2 * G, NSA_HEAD_DIM, n), BF16)],
        compiler_params=_params("parallel", "parallel"),
        name="nsa_compress",
    )(kv_c, pos, w1, w2, w2t)


def _nsa_kernel(q_ref, kc_ref, vct_ref, ks_ref, vst_ref, kw_ref, vwt_ref, oh_ref, aggt_ref,
                gate_ref, z_ref, o_ref, m_ref, acc_ref, s_ref, *, tq):
    R, DH = NSA_REP, NSA_HEAD_DIM
    qb = pl.program_id(1)
    start = qb * tq
    q = jnp.concatenate([q_ref[r, 0] for r in range(R)], axis=1)
    ncp = kc_ref.shape[1]

    def per_head(x):
        return jnp.concatenate([x] * R, axis=1)

    def rows8(x, n):
        return jnp.tile(x, (n // 8, 1))

    n_c = lax.broadcasted_iota(jnp.int32, (ncp, tq), 0)
    t_c = start + lax.broadcasted_iota(jnp.int32, (ncp, tq), 1)
    cmask = per_head((n_c * CMP_STRIDE + CMP_BLOCK - 1) <= t_c)
    s = jnp.where(cmask, jnp.dot(kc_ref[0], q, preferred_element_type=F32), -jnp.inf)
    m = jnp.max(s, axis=0, keepdims=True)
    m = jnp.where(m == -jnp.inf, 0.0, m)
    e = jnp.exp2(s - m)
    p = e * (1.0 / jnp.maximum(jnp.sum(e, axis=0, keepdims=True), 1e-30))
    o_cmp = jnp.dot(vct_ref[0], p.astype(BF16), preferred_element_type=F32)
    p_sum = p[:, :tq]
    for r in range(1, R):
        p_sum = p_sum + p[:, r * tq:(r + 1) * tq]

    def col_max8(x):
        mx = x[0:8]
        for i in range(1, x.shape[0] // 8):
            mx = jnp.maximum(mx, x[8 * i:8 * i + 8])
        for shift in (4, 2, 1):
            mx = jnp.maximum(mx, pltpu.roll(mx, shift, 0))
        return mx

    def key_rows(ref, kt):
        return ref[0, pl.ds(pl.multiple_of(kt * tq, tq), tq), :]

    c_k = lax.broadcasted_iota(jnp.int32, (tq, tq), 0)
    i_q = lax.broadcasted_iota(jnp.int32, (tq, tq), 1)
    causal = per_head(c_k <= i_q)
    no_old = jnp.where(qb >= 2, 0, NSA_WINDOW)
    no_mid = jnp.where(qb >= 1, 0, NSA_WINDOW)
    old_ok = per_head((2 * tq + i_q - c_k + no_old) < NSA_WINDOW)
    mid_ok = per_head((tq + i_q - c_k + no_mid) < NSA_WINDOW)
    kt_old, kt_mid = jnp.maximum(qb - 2, 0), jnp.maximum(qb - 1, 0)
    k_win = jnp.concatenate([key_rows(kw_ref, kt_old), key_rows(kw_ref, kt_mid),
                             key_rows(kw_ref, qb)], axis=0)
    s_w = jnp.where(jnp.concatenate([old_ok, mid_ok, causal], axis=0),
                    jnp.dot(k_win, q, preferred_element_type=F32), MASK_NEG)
    p_w = jnp.exp2(s_w - rows8(col_max8(s_w), 3 * tq)).astype(BF16)
    v_win = jnp.concatenate([vwt_ref[0, kt_old], vwt_ref[0, kt_mid], vwt_ref[0, qb]], axis=1)
    acc_w = jnp.dot(jnp.concatenate([v_win, jnp.ones((NSA_ONES_ROWS, 3 * tq), BF16)], axis=0),
                    p_w, preferred_element_type=F32)
    o_win = acc_w[:DH] * rows8(1.0 / acc_w[DH:DH + 8], DH)

    aggt = aggt_ref[...]
    p_hi = p_sum.astype(BF16)
    p_lo = (p_sum - p_hi.astype(F32)).astype(BF16)
    imp = (jnp.dot(aggt, p_hi, preferred_element_type=F32)
           + jnp.dot(aggt, p_lo, preferred_element_type=F32))
    j_s = lax.broadcasted_iota(jnp.int32, (LANES, tq), 0)
    t_s = start + lax.broadcasted_iota(jnp.int32, (LANES, tq), 1)
    cur = lax.shift_right_logical(t_s, 6)
    forced = (j_s == 0) | (j_s == cur) | (j_s == cur - 1)
    allowed = j_s * SLC_BLOCK <= t_s
    sel = jnp.where(forced & allowed, 1.0, 0.0)
    val = jnp.where(allowed & ~forced, imp, -jnp.inf)
    row_f = j_s.astype(F32)
    for _ in range(SLC_TOPK - 3):
        mx = jnp.max(val, axis=0, keepdims=True)
        idx = jnp.min(jnp.where(val == mx, row_f, float(LANES)), axis=0, keepdims=True)
        pick = row_f == idx
        sel = jnp.where(pick, 1.0, sel)
        val = jnp.where(pick, -jnp.inf, val)
    m_neg = jnp.where(sel > 0.0, 0.0, MASK_NEG).astype(BF16)
    q_aug = jnp.concatenate([q, per_head(m_neg)], axis=0)

    ones_rows = jnp.ones((NSA_ONES_ROWS, tq), BF16)

    def flash_init():
        m_ref[...] = jnp.full(m_ref.shape, -jnp.inf, F32)
        acc_ref[...] = jnp.zeros_like(acc_ref)

    def flash_update(s, vt_tile):
        m_prev = m_ref[...]
        m_next = jnp.maximum(m_prev, col_max8(s))
        alpha = jnp.exp2(m_prev - m_next)
        p = jnp.exp2(s - rows8(m_next, tq)).astype(BF16)
        v_aug = jnp.concatenate([vt_tile, ones_rows], axis=0)
        acc_ref[...] = (acc_ref[...] * rows8(alpha, DH + NSA_ONES_ROWS)
                        + jnp.dot(v_aug, p, preferred_element_type=F32))
        m_ref[...] = m_next

    def flash_result():
        acc = acc_ref[...]
        return acc[:DH] * rows8(1.0 / acc[DH:DH + 8], DH)

    def slc_scores(kt):
        oh = oh_ref[pl.ds(pl.multiple_of(kt * tq, tq), tq), :]
        k_aug = jnp.concatenate([key_rows(ks_ref, kt), oh], axis=1)
        return jnp.dot(k_aug, q_aug, preferred_element_type=F32)

    flash_init()
    s_ref[...] = slc_scores(0)

    def slc_step(kt):
        s_cur = s_ref[...]
        s_ref[...] = slc_scores(kt + 1)
        flash_update(s_cur, vst_ref[0, kt])

    def slc_pair(i, carry):
        slc_step(2 * i)
        slc_step(2 * i + 1)
        return carry

    lax.fori_loop(0, lax.shift_right_logical(qb, 1), slc_pair, 0)

    @pl.when((qb & 1) == 1)
    def _():
        slc_step(qb - 1)

    flash_update(jnp.where(causal, s_ref[...], MASK_NEG), vst_ref[0, qb])
    o_slc = flash_result()

    gate = 1.0 / (1.0 + jnp.exp(-gate_ref[0, 0]))
    for r in range(R):
        sl = slice(r * tq, (r + 1) * tq)
        g = [rows8(gate[8 * (b * R + r):8 * (b * R + r) + 8], DH) for b in range(3)]
        o_t = g[0] * o_cmp[:, sl] + g[1] * o_slc[:, sl] + g[2] * o_win[:, sl]
        z = z_ref[:, r * DH:(r + 1) * DH]
        o_ref[:, r * DH:(r + 1) * DH] = (o_t.T * _silu(z)).astype(o_ref.dtype)


def _nsa_attn(q_t, kc, vc_t, ksw, vsw_t, onehot, agg_t, gate_t, z):
    G, R, DH = NSA_KV_GROUPS, NSA_REP, NSA_HEAD_DIM
    S = ksw.shape[1]
    tq = NSA_TQ
    assert S % tq == 0 and NSA_WINDOW <= 2 * tq and S // SLC_BLOCK <= LANES
    ncp = kc.shape[1]
    n_t = S // tq
    rows = lambda off: pl.BlockSpec((1, S, DH), lambda g, i, off=off: (off + g, 0, 0))
    tiles_t = lambda off: pl.BlockSpec((1, n_t, DH, tq), lambda g, i, off=off: (off + g, 0, 0, 0))
    return pl.pallas_call(
        functools.partial(_nsa_kernel, tq=tq),
        grid=(G, n_t),
        in_specs=[pl.BlockSpec((R, 1, DH, tq), lambda g, i: (g, i, 0, 0)),
                  pl.BlockSpec((1, ncp, DH), lambda g, i: (g, 0, 0)),
                  pl.BlockSpec((1, DH, ncp), lambda g, i: (G + g, 0, 0)),
                  rows(0), tiles_t(0), rows(G), tiles_t(G),
                  pl.BlockSpec((S, LANES), lambda g, i: (0, 0)),
                  pl.BlockSpec((LANES, ncp), lambda g, i: (0, 0)),
                  pl.BlockSpec((1, 1, LANES, tq), lambda g, i: (g, i, 0, 0)),
                  pl.BlockSpec((tq, R * DH), lambda g, i: (i, g))],
        out_specs=pl.BlockSpec((tq, R * DH), lambda g, i: (i, g)),
        out_shape=jax.ShapeDtypeStruct((S, D_MODEL), BF16),
        scratch_shapes=[pltpu.VMEM((8, R * tq), F32),
                        pltpu.VMEM((DH + NSA_ONES_ROWS, R * tq), F32),
                        pltpu.VMEM((tq, R * tq), F32)],
        compiler_params=_params("parallel", "arbitrary"),
        name="nsa_attn",
    )(q_t, kc, vc_t, ksw, vsw_t, ksw, vsw_t, onehot, agg_t, gate_t, z)


def _swa_kernel(sink_ref, q_ref, kp_ref, kc_ref, vp_ref, vc_ref, z_ref, o_ref):
    W, R = SWA_WINDOW, SWA_REP
    n = pl.program_id(0)
    i_q = lax.broadcasted_iota(jnp.int32, (W, 2 * W), 0)
    c_k = lax.broadcasted_iota(jnp.int32, (W, 2 * W), 1)
    diff = i_q - (c_k - W)
    mask = (diff >= 0) & (diff < W) & ((n > 0) | (c_k >= W))
    lane = lax.broadcasted_iota(jnp.int32, (W, LANES), 1)
    low = lane < SWA_HEAD_DIM
    for g in range(SWA_KV_HEADS):
        gl = slice(g * LANES, (g + 1) * LANES)
        kk = jnp.concatenate([kp_ref[:, gl], kc_ref[:, gl]], axis=0)
        vv = jnp.concatenate([vp_ref[:, gl], vc_ref[:, gl]], axis=0)
        qs = []
        for r in range(R):
            h = g * R + r
            q2 = q_ref[:, (h // 2) * LANES:(h // 2 + 1) * LANES]
            qs.append(jnp.where(low if h % 2 == 0 else ~low, q2, jnp.zeros_like(q2)))
        s_all = lax.dot_general(jnp.concatenate(qs, axis=0), kk, _NT,
                                preferred_element_type=F32)
        ps, inv_l = [], []
        for r in range(R):
            sink = sink_ref[g * R + r]
            s = jnp.where(mask, s_all[r * W:(r + 1) * W], -jnp.inf)
            m = jnp.maximum(jnp.max(s, axis=1, keepdims=True), sink)
            e = jnp.exp(s - m)
            inv_l.append(1.0 / (jnp.sum(e, axis=1, keepdims=True) + jnp.exp(sink - m)))
            ps.append(e.astype(BF16))
        o_all = jnp.dot(jnp.concatenate(ps, axis=0), vv, preferred_element_type=F32)
        for pr in range(R // 2):
            o_even = o_all[(2 * pr) * W:(2 * pr + 1) * W] * inv_l[2 * pr]
            o_odd = o_all[(2 * pr + 1) * W:(2 * pr + 2) * W] * inv_l[2 * pr + 1]
            col = slice((g * R // 2 + pr) * LANES, (g * R // 2 + pr + 1) * LANES)
            o_ref[:, col] = (jnp.where(low, o_even, o_odd) * _silu(z_ref[:, col])).astype(o_ref.dtype)


def _swa_attn(sinks, q, k_dup, v_dup, z):
    S, D = q.shape
    W = SWA_WINDOW
    wide = SWA_KV_HEADS * LANES
    prev = lambda i: (jnp.maximum(i - 1, 0), 0)
    cur = lambda i: (i, 0)
    return pl.pallas_call(
        _swa_kernel,
        grid=(S // W,),
        in_specs=[pl.BlockSpec(memory_space=pltpu.SMEM),
                  pl.BlockSpec((W, D), cur),
                  pl.BlockSpec((W, wide), prev), pl.BlockSpec((W, wide), cur),
                  pl.BlockSpec((W, wide), prev), pl.BlockSpec((W, wide), cur),
                  pl.BlockSpec((W, D), cur)],
        out_specs=pl.BlockSpec((W, D), cur),
        out_shape=jax.ShapeDtypeStruct((S, D), BF16),
        compiler_params=_params("parallel"),
        name="swa_attn",
    )(sinks, q, k_dup, k_dup, v_dup, v_dup, z)


def _rope_tables(positions, head_dim, scale):
    rot = head_dim // ROPE_FRACTION
    half = rot // 2
    inv_freq = ROPE_THETA ** (-jnp.arange(0, rot, 2, dtype=F32) / rot)
    ang = positions.reshape(-1).astype(F32)[:, None] * inv_freq
    cos, sin = jnp.cos(ang), jnp.sin(ang)
    S = cos.shape[0]
    rest = head_dim - rot
    cos_h = jnp.concatenate([cos, cos, jnp.ones((S, rest), F32)], axis=1)
    sin_h = jnp.concatenate([-sin, sin, jnp.zeros((S, rest), F32)], axis=1)
    reps = LANES // head_dim
    return (jnp.tile(cos_h, (1, reps)) * scale, jnp.tile(sin_h, (1, reps)) * scale, half, head_dim)


def _hgrn_esum():
    row_s = np.arange(HG_SUB * HG_HEAD_DIM) // HG_HEAD_DIM
    col_s = np.arange(HG_CHUNK) % HG_SUB
    return jnp.asarray(row_s[:, None] == col_s[None, :], dtype=BF16)


def _nsa_constants(S):
    n_cmp = S // CMP_STRIDE - 1
    ncp = S // CMP_STRIDE
    n_slc = S // SLC_BLOCK
    ratio = SLC_BLOCK // CMP_STRIDE
    i = np.arange(ncp)[:, None]
    j = np.arange(LANES)[None, :]
    agg = ((i >= ratio * j - CMP_BLOCK // CMP_STRIDE + 1) & (i <= ratio * j + ratio - 1)
           & (i < n_cmp) & (j < n_slc))
    onehot = (np.arange(S)[:, None] // SLC_BLOCK) == j
    return jnp.asarray(onehot, dtype=BF16), jnp.asarray(agg.T, dtype=BF16)


def _rope_tables_t(positions, head_dim):
    rot = head_dim // ROPE_FRACTION
    inv_freq = ROPE_THETA ** (-jnp.arange(0, rot, 2, dtype=F32) / rot)
    ang = inv_freq[:, None] * positions.reshape(-1).astype(F32)[None, :]
    cos, sin = jnp.cos(ang), jnp.sin(ang)
    return jnp.concatenate([cos, cos], axis=0), jnp.concatenate([-sin, sin], axis=0), rot // 2


def _hgrn_layer(hb, w_in, g_norm, log_lb, log_1m_lb, esum):
    proj = _mm(hb, w_in.astype(BF16), out_dtype=F32)
    return _hgrn_rec(proj, log_lb, log_1m_lb, g_norm, esum)


def _nsa_layer(hb, positions, w_in, pos_k, w1_k, w2_k, pos_v, w1_v, w2_v):
    S = hb.shape[0]
    H, G, R, DH = NSA_HEADS, NSA_KV_GROUPS, NSA_REP, NSA_HEAD_DIM
    gw = G * DH
    o_q, o_kc, o_vc, o_ks, o_vs, o_kw, o_vw = (0, H * DH, H * DH + gw, H * DH + 2 * gw,
                                               H * DH + 3 * gw, H * DH + 4 * gw, H * DH + 5 * gw)
    o_gate = H * DH + 6 * gw
    o_z = o_gate + 3 * H
    wb = w_in.astype(BF16)
    seg = lambda off, width: wb[:, off:off + width]
    rope_k = _rope_tables(positions, DH, 1.0)
    tq = NSA_TQ

    q_t = _mm_t(hb, seg(o_q, H * DH).T, out_dtype=BF16, tk=tq,
                rope=_rope_tables_t(positions, DH), scale=DH ** -0.5 * LOG2_E)
    k_c = _mm(hb, seg(o_kc, gw), out_dtype=F32, rope=rope_k, head_major=True)
    v_c = _mm(hb, seg(o_vc, gw), out_dtype=F32, head_major=True)
    ksw = _mm(hb, jnp.concatenate([seg(o_ks, gw), seg(o_kw, gw)], axis=1), out_dtype=BF16,
              rope=rope_k, head_major=True)
    vsw_t = _mm_t(hb, jnp.concatenate([seg(o_vs, gw), seg(o_vw, gw)], axis=1).T,
                  out_dtype=BF16, tk=tq)
    z = _mm(hb, seg(o_z, H * DH), out_dtype=F32)
    src = np.full(G * LANES, 3 * H, dtype=np.int32)
    for g in range(G):
        for b in range(3):
            for r in range(R):
                c = b * R + r
                src[g * LANES + 8 * c:g * LANES + 8 * c + 8] = b * H + g * R + r
    w_gate = jnp.concatenate([seg(o_gate, 3 * H), jnp.zeros((w_in.shape[0], 1), BF16)], axis=1)[:, src]
    gate_t = _mm_t(hb, w_gate.T, out_dtype=F32, tk=tq)

    n_grp = S // CMP_STRIDE
    kv_c = jnp.concatenate([k_c, v_c], axis=0).reshape(2 * G, n_grp, CMP_STRIDE * DH)
    pos = jnp.stack([pos_k, pos_v]).reshape(2, 1, CMP_BLOCK * DH)
    w2 = jnp.stack([w2_k, w2_v]).astype(BF16)
    kvc, kvc_t = _nsa_compress(kv_c, pos, jnp.stack([w1_k, w1_v]).astype(BF16), w2,
                               jnp.swapaxes(w2, 1, 2))
    onehot, agg_t = _nsa_constants(S)
    return _nsa_attn(q_t, kvc, kvc_t, ksw, vsw_t, onehot, agg_t, gate_t, z)


def _swa_layer(hb, positions, w_in, sinks):
    H, KV, DH = SWA_HEADS, SWA_KV_HEADS, SWA_HEAD_DIM
    wb = w_in.astype(BF16)
    o_k = H * DH
    o_v = o_k + KV * DH
    o_z = o_v + KV * DH
    dup = np.concatenate([np.tile(np.arange(g * DH, (g + 1) * DH), 2) for g in range(KV)])
    rope_q = _rope_tables(positions, DH, DH ** -0.5)
    rope_k = _rope_tables(positions, DH, 1.0)
    q = _mm(hb, wb[:, :o_k], out_dtype=BF16, rope=rope_q)
    k_dup = _mm(hb, wb[:, o_k:o_v][:, dup], out_dtype=BF16, rope=rope_k)
    v_dup = _mm(hb, wb[:, o_v:o_z][:, dup], out_dtype=BF16)
    z = _mm(hb, wb[:, o_z:], out_dtype=F32)
    return _swa_attn(sinks, q, k_dup, v_dup, z)


def kernel(x, positions, hgrn_lb_logits, l0_w_in, l0_g_norm, l0_w_out, l0_ln_g, l0_ln_b, l1_w_in, l1_cmp_pos_k, l1_cmp_w1_k, l1_cmp_w2_k, l1_cmp_pos_v, l1_cmp_w1_v, l1_cmp_w2_v, l1_w_out, l1_ln_g, l1_ln_b, l2_w_in, l2_sinks, l2_w_out, l2_ln_g, l2_ln_b, l3_w_in, l3_g_norm, l3_w_out, l3_ln_g, l3_ln_b):
    B, S, D = x.shape
    assert B == 1 and D == D_MODEL
    lb = jnp.cumsum(jax.nn.softmax(hgrn_lb_logits.astype(F32), axis=0), axis=0)
    lb = lb - lb[0:1]
    log_lb, log_1m_lb = jnp.log(lb), jnp.log1p(-lb)
    esum = _hgrn_esum()

    h = x.reshape(S, D)
    hb = h.astype(BF16)

    a = _hgrn_layer(hb, l0_w_in, l0_g_norm, log_lb[0], log_1m_lb[0], esum)
    h, hb = _outproj_ln(a, l0_w_out.astype(BF16), h, l0_ln_g, l0_ln_b)

    a = _nsa_layer(hb, positions, l1_w_in, l1_cmp_pos_k, l1_cmp_w1_k, l1_cmp_w2_k,
                   l1_cmp_pos_v, l1_cmp_w1_v, l1_cmp_w2_v)
    h, hb = _outproj_ln(a, l1_w_out.astype(BF16), h, l1_ln_g, l1_ln_b)

    a = _swa_layer(hb, positions, l2_w_in, l2_sinks)
    h, hb = _outproj_ln(a, l2_w_out.astype(BF16), h, l2_ln_g, l2_ln_b)

    a = _hgrn_layer(hb, l3_w_in, l3_g_norm, log_lb[1], log_1m_lb[1], esum)
    h, hb = _outproj_ln(a, l3_w_out.astype(BF16), h, l3_ln_g, l3_ln_b)
    return h.reshape(B, S, D)
```

```python
import functools

import numpy as np
import jax
import jax.numpy as jnp
from jax import lax
from jax.experimental import pallas as pl
from jax.experimental.pallas import tpu as pltpu

F32 = jnp.float32
BF16 = jnp.bfloat16

D_MODEL = 2048
DEPTH = 4
N_MIXERS = 3
DEEPNORM_ALPHA = (2 * DEPTH) ** 0.25
LN_EPS = 1e-5
RMS_EPS = 1e-6
ROPE_THETA = 500000.0
ROPE_FRACTION = 4

HG_HEAD_DIM = 128
HG_HEADS = D_MODEL // HG_HEAD_DIM
HG_CHUNK = 64
HG_SUB = 16
HG_TIME_BLOCK = 512
LOG2_E = 1.4426950408889634

NSA_HEAD_DIM = 128
NSA_HEADS = D_MODEL // NSA_HEAD_DIM
NSA_KV_GROUPS = 4
NSA_REP = NSA_HEADS // NSA_KV_GROUPS
CMP_BLOCK = 32
CMP_STRIDE = 16
CMP_HIDDEN = 256
SLC_BLOCK = 64
SLC_TOPK = 16
NSA_WINDOW = 512
NSA_TQ = 256
NSA_ONES_ROWS = 16
FORCE_BONUS = 1.0e4
MASK_NEG = -1.0e30

SWA_HEAD_DIM = 64
SWA_HEADS = D_MODEL // SWA_HEAD_DIM
SWA_KV_HEADS = 4
SWA_REP = SWA_HEADS // SWA_KV_HEADS
SWA_WINDOW = 128

LANES = 128
VMEM_LIMIT_BYTES = 48 * 1024 * 1024

_NT = (((1,), (1,)), ((), ()))
_TN = (((0,), (0,)), ((), ()))


def _params(*sem):
    return pltpu.CompilerParams(dimension_semantics=sem, vmem_limit_bytes=VMEM_LIMIT_BYTES)


def _silu(x):
    return x * (1.0 / (1.0 + jnp.exp(-x)))


def _mm_kernel(*refs, rope_half, rope_period, head_major, n_chunks):
    if rope_half:
        x_ref, w_ref, c_ref, s_ref, o_ref, wb_ref = refs
    else:
        x_ref, w_ref, o_ref, wb_ref = refs

    @pl.when(pl.program_id(1) == 0)
    def _():
        wb_ref[...] = w_ref[...].astype(BF16)

    acc = jnp.dot(x_ref[...], wb_ref[...], preferred_element_type=F32)
    if rope_half:
        cos = c_ref[...]
        sin = s_ref[...]
        lane = lax.broadcasted_iota(jnp.int32, cos.shape, 1)
        first_half = (lane & (rope_period - 1)) < rope_half
    for j in range(n_chunks):
        a = acc[:, j * LANES:(j + 1) * LANES]
        if rope_half:
            up = pltpu.roll(a, LANES - rope_half, 1)
            dn = pltpu.roll(a, rope_half, 1)
            a = a * cos + jnp.where(first_half, up, dn) * sin
        if head_major:
            o_ref[j] = a.astype(o_ref.dtype)
        else:
            o_ref[:, j * LANES:(j + 1) * LANES] = a.astype(o_ref.dtype)


def _col_tiles(w, cols, tn):
    return (0, 1, w.shape[1] // tn) if cols is None else cols


def _mm(x, w, *, out_dtype, cols=None, rope=None, head_major=False, tn=512):
    S, K = x.shape
    tm = min(1024, S)
    first, stride, n_tiles = _col_tiles(w, cols, tn)
    N = n_tiles * tn
    n_chunks = tn // LANES
    in_specs = [pl.BlockSpec((tm, K), lambda j, i: (i, 0)),
                pl.BlockSpec((K, tn), lambda j, i: (0, first + stride * j))]
    args = [x, w]
    half = period = 0
    if rope is not None:
        cos_t, sin_t, half, period = rope
        in_specs += [pl.BlockSpec((tm, LANES), lambda j, i: (i, 0))] * 2
        args += [cos_t, sin_t]
    if head_major:
        out_shape = jax.ShapeDtypeStruct((N // LANES, S, LANES), out_dtype)
        out_spec = pl.BlockSpec((n_chunks, tm, LANES), lambda j, i: (j, i, 0))
    else:
        out_shape = jax.ShapeDtypeStruct((S, N), out_dtype)
        out_spec = pl.BlockSpec((tm, tn), lambda j, i: (i, j))
    return pl.pallas_call(
        functools.partial(_mm_kernel, rope_half=half, rope_period=period,
                          head_major=head_major, n_chunks=n_chunks),
        grid=(n_tiles, S // tm),
        in_specs=in_specs,
        out_specs=out_spec,
        out_shape=out_shape,
        scratch_shapes=[pltpu.VMEM((K, tn), BF16)],
        compiler_params=_params("arbitrary", "arbitrary"),
        name="in_proj",
    )(*args)


def _mm_t_kernel(*refs, rope_half, scale, n_heads, n_tiles, tk):
    if rope_half:
        w_ref, x_ref, c_ref, s_ref, o_ref, wt_ref = refs
    else:
        w_ref, x_ref, o_ref, wt_ref = refs

    @pl.when(pl.program_id(1) == 0)
    def _():
        wt_ref[...] = w_ref[...].T.astype(BF16)

    acc = lax.dot_general(wt_ref[...], x_ref[...], _NT, preferred_element_type=F32)
    for c in range(n_heads):
        a = acc[c * LANES:(c + 1) * LANES]
        if scale != 1.0:
            a = a * scale
        if rope_half:
            rot = 2 * rope_half
            swapped = jnp.concatenate([a[rope_half:rot], a[:rope_half]], axis=0)
            a = jnp.concatenate([a[:rot] * c_ref[...] + swapped * s_ref[...], a[rot:]], axis=0)
        for b in range(n_tiles):
            o_ref[c, b] = a[:, b * tk:(b + 1) * tk].astype(o_ref.dtype)


def _mm_t(x, w, *, out_dtype, tk, cols=None, rope=None, scale=1.0, tn=512):
    S, K = x.shape
    tm = min(1024, S)
    first, stride, n_tiles = _col_tiles(w, cols, tn)
    N = n_tiles * tn
    in_specs = [pl.BlockSpec((K, tn), lambda j, i: (0, first + stride * j)),
                pl.BlockSpec((tm, K), lambda j, i: (i, 0))]
    args = [w, x]
    half = 0
    if rope is not None:
        cos_t, sin_t, half = rope
        in_specs += [pl.BlockSpec((2 * half, tm), lambda j, i: (0, i))] * 2
        args += [cos_t, sin_t]
    return pl.pallas_call(
        functools.partial(_mm_t_kernel, rope_half=half, scale=scale, n_heads=tn // LANES,
                          n_tiles=tm // tk, tk=tk),
        grid=(n_tiles, S // tm),
        in_specs=in_specs,
        out_specs=pl.BlockSpec((tn // LANES, tm // tk, LANES, tk), lambda j, i: (j, i, 0, 0)),
        out_shape=jax.ShapeDtypeStruct((N // LANES, S // tk, LANES, tk), out_dtype),
        scratch_shapes=[pltpu.VMEM((tn, K), BF16)],
        compiler_params=_params("arbitrary", "arbitrary"),
        name="in_proj_t",
    )(*args)


def _outproj_ln_kernel(a_ref, w_ref, h_ref, g_ref, b_ref, o_ref, ob_ref):
    y = jnp.dot(a_ref[...], w_ref[...], preferred_element_type=F32)
    u = DEEPNORM_ALPHA * h_ref[...] + y
    mu = jnp.mean(u, axis=-1, keepdims=True)
    xc = u - mu
    var = jnp.mean(xc * xc, axis=-1, keepdims=True)
    out = xc * lax.rsqrt(var + LN_EPS) * g_ref[...] + b_ref[...]
    o_ref[...] = out
    ob_ref[...] = out.astype(BF16)


def _outproj_ln(a, w, h, g, b):
    S, D = h.shape
    tm = min(512, S)
    row = pl.BlockSpec((tm, D), lambda i: (i, 0))
    vec = pl.BlockSpec((1, D), lambda i: (0, 0))
    return pl.pallas_call(
        _outproj_ln_kernel,
        grid=(S // tm,),
        in_specs=[row, pl.BlockSpec((D, D), lambda i: (0, 0)), row, vec, vec],
        out_specs=[row, row],
        out_shape=[jax.ShapeDtypeStruct((S, D), F32), jax.ShapeDtypeStruct((S, D), BF16)],
        compiler_params=_params("parallel"),
        name="outproj_ln",
    )(a, w, h, g.reshape(1, D), b.reshape(1, D))


def _hgrn_kernel(q_ref, f_ref, v_ref, z_ref, llb_ref, l1m_ref, gn_ref, esum_ref,
                 o_ref, st_ref, ycat_ref, u_ref, *, n_chunks):
    C, SUB, DK = HG_CHUNK, HG_SUB, HG_HEAD_DIM
    n_sub = C // SUB

    @pl.when(pl.program_id(1) == 0)
    def _():
        st_ref[...] = jnp.zeros_like(st_ref)

    r = lax.broadcasted_iota(jnp.int32, (C, C), 0)
    c = lax.broadcasted_iota(jnp.int32, (C, C), 1)
    same_sub = (r // SUB) == (c // SUB)
    causal = c <= r
    cum_one = jnp.concatenate([
        causal.astype(F32), (causal & same_sub).astype(F32),
        jnp.ones((C, C), F32), same_sub.astype(F32)], axis=0).astype(BF16)
    cum_mat = jnp.concatenate([cum_one] * 3, axis=1)
    diag_mask = causal & same_sub
    row_k = lax.broadcasted_iota(jnp.int32, (C, DK), 0)
    log_lb = llb_ref[...]
    log_1m_lb = l1m_ref[...]
    chunks = [slice(ci * C, (ci + 1) * C) for ci in range(n_chunks)]

    def rows_to_lanes(x):
        return jnp.concatenate([x[sl] for sl in chunks], axis=1)

    def lanes_to_rows(x):
        return jnp.concatenate([x[:, ci * DK:(ci + 1) * DK] for ci in range(n_chunks)], axis=0)

    q = q_ref[...]
    fl = f_ref[...]
    vb = v_ref[...].astype(BF16)
    log_sig = jnp.minimum(fl, 0.0) - jnp.log(1.0 + jnp.exp(-jnp.abs(fl)))
    t = log_1m_lb + log_sig
    log_f = jnp.maximum(log_lb, t) + jnp.log(1.0 + jnp.exp(-jnp.abs(log_lb - t)))
    log2_k = (t - fl) * LOG2_E
    k = jnp.exp2(log2_k)
    lf = rows_to_lanes(log_f)
    lf_hi = lf.astype(BF16)
    lf_r1 = lf - lf_hi.astype(F32)
    lf_mid = lf_r1.astype(BF16)
    lf_lo = (lf_r1 - lf_mid.astype(F32)).astype(BF16)
    cums = jnp.dot(cum_mat, jnp.concatenate([lf_hi, lf_mid, lf_lo], axis=0),
                   preferred_element_type=F32) * LOG2_E
    b_full = lanes_to_rows(cums[0:C])
    b_loc = lanes_to_rows(cums[C:2 * C])
    b_tot = lanes_to_rows(cums[2 * C:3 * C])
    b_loc_tot = lanes_to_rows(cums[3 * C:4 * C])
    b_start = b_full - b_loc
    qe = (q * jnp.exp2(b_full)).astype(BF16)
    kd = (k * jnp.exp2(b_tot - b_full)).astype(BF16)
    qt = q * jnp.exp2(b_loc)
    kt = k * jnp.exp2(b_loc_tot - b_loc)

    o_intra, a_offs = [], []
    for ci, ch in enumerate(chunks):
        for i in range(n_sub):
            sl = slice(ci * C + i * SUB, ci * C + (i + 1) * SUB)
            bi, qi, lki = b_loc[sl], q[sl], log2_k[sl]
            c_row = bi - lki
            for s in range(SUB):
                lo = 8 * (s // 8)
                y = qi[lo:] * jnp.exp2(jnp.minimum(bi[lo:] - c_row[s:s + 1, :], lki[s:s + 1, :]))
                if lo:
                    y = jnp.concatenate([jnp.zeros((lo, DK), F32), y], axis=0)
                ycat_ref[sl, s * DK:(s + 1) * DK] = y.astype(BF16)
        bs_c, qt_c, kt_c = b_start[ch], qt[ch], kt[ch]
        lhs, rhs = [], []
        for j in range(n_sub - 1):
            b_next = jnp.tile(bs_c[(j + 1) * SUB:(j + 1) * SUB + 8], (C // 8, 1))
            e = jnp.exp2(jnp.minimum(bs_c - b_next, 0.0))
            lhs.append(jnp.where(row_k >= (j + 1) * SUB, qt_c * e, 0.0).astype(BF16))
            rhs.append(jnp.where((row_k >= j * SUB) & (row_k < (j + 1) * SUB), kt_c, 0.0).astype(BF16))
        a_offs.append(lax.dot_general(jnp.concatenate(lhs, axis=1), jnp.concatenate(rhs, axis=1),
                                      _NT, preferred_element_type=F32))
        u_ref[ci] = lax.dot_general(vb[ch], kd[ch], _TN, preferred_element_type=F32)
    a_diag = jnp.dot(ycat_ref[...], esum_ref[...], preferred_element_type=F32)
    for ci, ch in enumerate(chunks):
        a = jnp.where(diag_mask, a_diag[ch], 0.0) + a_offs[ci]
        o_intra.append(jnp.dot(a.astype(BF16), vb[ch], preferred_element_type=F32))

    st = st_ref[...]
    outs = []
    for ci, ch in enumerate(chunks):
        outs.append(o_intra[ci] + lax.dot_general(qe[ch], st.astype(BF16), _NT,
                                                  preferred_element_type=F32))
        decay = jnp.exp2(jnp.tile(b_tot[ci * C:ci * C + 8], (DK // 8, 1)))
        st = st * decay + u_ref[ci]
    st_ref[...] = st

    o = jnp.concatenate(outs, axis=0)
    ms = jnp.mean(o * o, axis=-1, keepdims=True)
    out = o * lax.rsqrt(ms + RMS_EPS) * gn_ref[...] * _silu(z_ref[...])
    o_ref[...] = out.astype(o_ref.dtype)


def _hgrn_rec(proj, log_lb, log_1m_lb, g_norm, esum):
    S = proj.shape[1]
    D, DK, H = D_MODEL, HG_HEAD_DIM, HG_HEADS
    T = min(HG_TIME_BLOCK, S)

    def col(off):
        return pl.BlockSpec((None, T, DK), lambda h, t, off=off: (off + h, t, 0))

    vec = pl.BlockSpec((1, DK), lambda h, t: (0, h))
    return pl.pallas_call(
        functools.partial(_hgrn_kernel, n_chunks=T // HG_CHUNK),
        grid=(H, S // T),
        in_specs=[col(0), col(H), col(2 * H), col(3 * H), vec, vec, vec,
                  pl.BlockSpec(esum.shape, lambda h, t: (0, 0))],
        out_specs=pl.BlockSpec((T, DK), lambda h, t: (t, h)),
        out_shape=jax.ShapeDtypeStruct((S, D), BF16),
        scratch_shapes=[pltpu.VMEM((DK, DK), F32),
                        pltpu.VMEM((T, HG_SUB * DK), BF16),
                        pltpu.VMEM((T // HG_CHUNK, DK, DK), F32)],
        compiler_params=_params("parallel", "arbitrary"),
        name="hgrn_rec",
    )(proj, proj, proj, proj, log_lb.reshape(1, D), log_1m_lb.reshape(1, D),
      g_norm.reshape(1, D), esum)


def _nsa_compress_kernel(x_ref, pos_ref, w1_ref, w2_ref, o_ref, *, transposed):
    DH = NSA_HEAD_DIM
    n = x_ref.shape[0] // CMP_STRIDE
    h1 = jnp.zeros((n, CMP_HIDDEN), F32)
    h2 = jnp.zeros((n, CMP_HIDDEN), F32)
    for l in range(CMP_STRIDE):
        x_l = x_ref[pl.ds(l, n, stride=CMP_STRIDE), :]
        l2 = CMP_STRIDE + l
        h1 = h1 + jnp.dot((x_l + pos_ref[l:l + 1, :]).astype(BF16),
                          w1_ref[l * DH:(l + 1) * DH, :].astype(BF16), preferred_element_type=F32)
        h2 = h2 + jnp.dot((x_l + pos_ref[l2:l2 + 1, :]).astype(BF16),
                          w1_ref[l2 * DH:(l2 + 1) * DH, :].astype(BF16), preferred_element_type=F32)
    hid = h1 + pltpu.roll(h2, n - 1, 0)
    act = _silu(hid).astype(BF16)
    if transposed:
        o_ref[...] = lax.dot_general(w2_ref[...].T.astype(BF16), act, _NT,
                                     preferred_element_type=F32).astype(o_ref.dtype)
    else:
        o_ref[...] = jnp.dot(act, w2_ref[...].astype(BF16),
                             preferred_element_type=F32).astype(o_ref.dtype)


def _nsa_compress(x, pos, w1, w2, *, transposed):
    G, S, DH = x.shape
    n = S // CMP_STRIDE
    whole = lambda a: pl.BlockSpec(a.shape, lambda g: (0,) * a.ndim)
    out_block = (None, DH, n) if transposed else (None, n, DH)
    return pl.pallas_call(
        functools.partial(_nsa_compress_kernel, transposed=transposed),
        grid=(G,),
        in_specs=[pl.BlockSpec((None, S, DH), lambda g: (g, 0, 0)), whole(pos), whole(w1), whole(w2)],
        out_specs=pl.BlockSpec(out_block, lambda g: (g, 0, 0)),
        out_shape=jax.ShapeDtypeStruct((G,) + out_block[1:], BF16),
        compiler_params=_params("parallel"),
        name="nsa_compress",
    )(x, pos, w1, w2)


def _nsa_kernel(q_ref, kc_ref, vct_ref, ks_ref, vst_ref, kw_ref, vwt_ref, oh_ref, aggt_ref,
                gate_ref, z_ref, o_ref, m_ref, acc_ref, s_ref, *, tq):
    R, DH = NSA_REP, NSA_HEAD_DIM
    qb = pl.program_id(1)
    start = qb * tq
    q = jnp.concatenate([q_ref[r, 0] for r in range(R)], axis=1)
    ncp = kc_ref.shape[1]

    def per_head(x):
        return jnp.concatenate([x] * R, axis=1)

    def rows8(x, n):
        return jnp.tile(x, (n // 8, 1))

    n_c = lax.broadcasted_iota(jnp.int32, (ncp, tq), 0)
    t_c = start + lax.broadcasted_iota(jnp.int32, (ncp, tq), 1)
    cmask = per_head((n_c * CMP_STRIDE + CMP_BLOCK - 1) <= t_c)
    s = jnp.where(cmask, jnp.dot(kc_ref[0], q, preferred_element_type=F32), -jnp.inf)
    m = jnp.max(s, axis=0, keepdims=True)
    m = jnp.where(m == -jnp.inf, 0.0, m)
    e = jnp.exp2(s - m)
    p = e * (1.0 / jnp.maximum(jnp.sum(e, axis=0, keepdims=True), 1e-30))
    o_cmp = jnp.dot(vct_ref[0], p.astype(BF16), preferred_element_type=F32)
    p_sum = p[:, :tq]
    for r in range(1, R):
        p_sum = p_sum + p[:, r * tq:(r + 1) * tq]

    def col_max8(x):
        mx = x[0:8]
        for i in range(1, x.shape[0] // 8):
            mx = jnp.maximum(mx, x[8 * i:8 * i + 8])
        for shift in (4, 2, 1):
            mx = jnp.maximum(mx, pltpu.roll(mx, shift, 0))
        return mx

    def key_rows(ref, kt):
        return ref[0, pl.ds(pl.multiple_of(kt * tq, tq), tq), :]

    c_k = lax.broadcasted_iota(jnp.int32, (tq, tq), 0)
    i_q = lax.broadcasted_iota(jnp.int32, (tq, tq), 1)
    causal = per_head(c_k <= i_q)
    no_old = jnp.where(qb >= 2, 0, NSA_WINDOW)
    no_mid = jnp.where(qb >= 1, 0, NSA_WINDOW)
    old_ok = per_head((2 * tq + i_q - c_k + no_old) < NSA_WINDOW)
    mid_ok = per_head((tq + i_q - c_k + no_mid) < NSA_WINDOW)
    kt_old, kt_mid = jnp.maximum(qb - 2, 0), jnp.maximum(qb - 1, 0)
    k_win = jnp.concatenate([key_rows(kw_ref, kt_old), key_rows(kw_ref, kt_mid),
                             key_rows(kw_ref, qb)], axis=0)
    s_w = jnp.where(jnp.concatenate([old_ok, mid_ok, causal], axis=0),
                    jnp.dot(k_win, q, preferred_element_type=F32), MASK_NEG)
    p_w = jnp.exp2(s_w - rows8(col_max8(s_w), 3 * tq)).astype(BF16)
    v_win = jnp.concatenate([vwt_ref[0, kt_old], vwt_ref[0, kt_mid], vwt_ref[0, qb]], axis=1)
    acc_w = jnp.dot(jnp.concatenate([v_win, jnp.ones((NSA_ONES_ROWS, 3 * tq), BF16)], axis=0),
                    p_w, preferred_element_type=F32)
    o_win = acc_w[:DH] * rows8(1.0 / acc_w[DH:DH + 8], DH)

    aggt = aggt_ref[...]
    p_hi = p_sum.astype(BF16)
    p_lo = (p_sum - p_hi.astype(F32)).astype(BF16)
    imp = (jnp.dot(aggt, p_hi, preferred_element_type=F32)
           + jnp.dot(aggt, p_lo, preferred_element_type=F32))
    j_s = lax.broadcasted_iota(jnp.int32, (LANES, tq), 0)
    t_s = start + lax.broadcasted_iota(jnp.int32, (LANES, tq), 1)
    cur = lax.shift_right_logical(t_s, 6)
    forced = (j_s == 0) | (j_s == cur) | (j_s == cur - 1)
    allowed = j_s * SLC_BLOCK <= t_s
    sel = jnp.where(forced & allowed, 1.0, 0.0)
    val = jnp.where(allowed & ~forced, imp, -jnp.inf)
    row_f = j_s.astype(F32)
    for _ in range(SLC_TOPK - 3):
        mx = jnp.max(val, axis=0, keepdims=True)
        idx = jnp.min(jnp.where(val == mx, row_f, float(LANES)), axis=0, keepdims=True)
        pick = row_f == idx
        sel = jnp.where(pick, 1.0, sel)
        val = jnp.where(pick, -jnp.inf, val)
    m_neg = jnp.where(sel > 0.0, 0.0, MASK_NEG).astype(BF16)
    q_aug = jnp.concatenate([q, per_head(m_neg)], axis=0)

    ones_rows = jnp.ones((NSA_ONES_ROWS, tq), BF16)

    def flash_init():
        m_ref[...] = jnp.full(m_ref.shape, -jnp.inf, F32)
        acc_ref[...] = jnp.zeros_like(acc_ref)

    def flash_update(s, vt_tile):
        m_prev = m_ref[...]
        m_next = jnp.maximum(m_prev, col_max8(s))
        alpha = jnp.exp2(m_prev - m_next)
        p = jnp.exp2(s - rows8(m_next, tq)).astype(BF16)
        v_aug = jnp.concatenate([vt_tile, ones_rows], axis=0)
        acc_ref[...] = (acc_ref[...] * rows8(alpha, DH + NSA_ONES_ROWS)
                        + jnp.dot(v_aug, p, preferred_element_type=F32))
        m_ref[...] = m_next

    def flash_result():
        acc = acc_ref[...]
        return acc[:DH] * rows8(1.0 / acc[DH:DH + 8], DH)

    def slc_scores(kt):
        oh = oh_ref[pl.ds(pl.multiple_of(kt * tq, tq), tq), :]
        k_aug = jnp.concatenate([key_rows(ks_ref, kt), oh], axis=1)
        return jnp.dot(k_aug, q_aug, preferred_element_type=F32)

    flash_init()
    s_ref[...] = slc_scores(0)

    def slc_step(kt):
        s_cur = s_ref[...]
        s_ref[...] = slc_scores(kt + 1)
        flash_update(s_cur, vst_ref[0, kt])

    def slc_pair(i, carry):
        slc_step(2 * i)
        slc_step(2 * i + 1)
        return carry

    lax.fori_loop(0, lax.shift_right_logical(qb, 1), slc_pair, 0)

    @pl.when((qb & 1) == 1)
    def _():
        slc_step(qb - 1)

    flash_update(jnp.where(causal, s_ref[...], MASK_NEG), vst_ref[0, qb])
    o_slc = flash_result()

    gate = 1.0 / (1.0 + jnp.exp(-gate_ref[0, 0]))
    for r in range(R):
        sl = slice(r * tq, (r + 1) * tq)
        g = [rows8(gate[8 * (b * R + r):8 * (b * R + r) + 8], DH) for b in range(3)]
        o_t = g[0] * o_cmp[:, sl] + g[1] * o_slc[:, sl] + g[2] * o_win[:, sl]
        z = z_ref[:, r * DH:(r + 1) * DH]
        o_ref[:, r * DH:(r + 1) * DH] = (o_t.T * _silu(z)).astype(o_ref.dtype)


def _nsa_attn(q_t, kc, vc_t, ksw, vsw_t, onehot, agg_t, gate_t, z):
    G, R, DH = NSA_KV_GROUPS, NSA_REP, NSA_HEAD_DIM
    S = ksw.shape[1]
    tq = NSA_TQ
    assert S % tq == 0 and NSA_WINDOW <= 2 * tq and S // SLC_BLOCK <= LANES
    ncp = kc.shape[1]
    n_t = S // tq
    rows = lambda off: pl.BlockSpec((1, S, DH), lambda g, i, off=off: (off + g, 0, 0))
    tiles_t = lambda off: pl.BlockSpec((1, n_t, DH, tq), lambda g, i, off=off: (off + g, 0, 0, 0))
    return pl.pallas_call(
        functools.partial(_nsa_kernel, tq=tq),
        grid=(G, n_t),
        in_specs=[pl.BlockSpec((R, 1, DH, tq), lambda g, i: (g, i, 0, 0)),
                  pl.BlockSpec((1, ncp, DH), lambda g, i: (g, 0, 0)),
                  pl.BlockSpec((1, DH, ncp), lambda g, i: (g, 0, 0)),
                  rows(0), tiles_t(0), rows(G), tiles_t(G),
                  pl.BlockSpec((S, LANES), lambda g, i: (0, 0)),
                  pl.BlockSpec((LANES, ncp), lambda g, i: (0, 0)),
                  pl.BlockSpec((1, 1, LANES, tq), lambda g, i: (g, i, 0, 0)),
                  pl.BlockSpec((tq, R * DH), lambda g, i: (i, g))],
        out_specs=pl.BlockSpec((tq, R * DH), lambda g, i: (i, g)),
        out_shape=jax.ShapeDtypeStruct((S, D_MODEL), BF16),
        scratch_shapes=[pltpu.VMEM((8, R * tq), F32),
                        pltpu.VMEM((DH + NSA_ONES_ROWS, R * tq), F32),
                        pltpu.VMEM((tq, R * tq), F32)],
        compiler_params=_params("parallel", "arbitrary"),
        name="nsa_attn",
    )(q_t, kc, vc_t, ksw, vsw_t, ksw, vsw_t, onehot, agg_t, gate_t, z)


def _swa_kernel(sink_ref, q_ref, kp_ref, kc_ref, vp_ref, vc_ref, z_ref, o_ref):
    W, R = SWA_WINDOW, SWA_REP
    n = pl.program_id(0)
    i_q = lax.broadcasted_iota(jnp.int32, (W, 2 * W), 0)
    c_k = lax.broadcasted_iota(jnp.int32, (W, 2 * W), 1)
    diff = i_q - (c_k - W)
    mask = (diff >= 0) & (diff < W) & ((n > 0) | (c_k >= W))
    lane = lax.broadcasted_iota(jnp.int32, (W, LANES), 1)
    low = lane < SWA_HEAD_DIM
    for g in range(SWA_KV_HEADS):
        gl = slice(g * LANES, (g + 1) * LANES)
        kk = jnp.concatenate([kp_ref[:, gl], kc_ref[:, gl]], axis=0)
        vv = jnp.concatenate([vp_ref[:, gl], vc_ref[:, gl]], axis=0)
        qs = []
        for r in range(R):
            h = g * R + r
            q2 = q_ref[:, (h // 2) * LANES:(h // 2 + 1) * LANES]
            qs.append(jnp.where(low if h % 2 == 0 else ~low, q2, jnp.zeros_like(q2)))
        s_all = lax.dot_general(jnp.concatenate(qs, axis=0), kk, _NT,
                                preferred_element_type=F32)
        ps, inv_l = [], []
        for r in range(R):
            sink = sink_ref[g * R + r]
            s = jnp.where(mask, s_all[r * W:(r + 1) * W], -jnp.inf)
            m = jnp.maximum(jnp.max(s, axis=1, keepdims=True), sink)
            e = jnp.exp(s - m)
            inv_l.append(1.0 / (jnp.sum(e, axis=1, keepdims=True) + jnp.exp(sink - m)))
            ps.append(e.astype(BF16))
        o_all = jnp.dot(jnp.concatenate(ps, axis=0), vv, preferred_element_type=F32)
        for pr in range(R // 2):
            o_even = o_all[(2 * pr) * W:(2 * pr + 1) * W] * inv_l[2 * pr]
            o_odd = o_all[(2 * pr + 1) * W:(2 * pr + 2) * W] * inv_l[2 * pr + 1]
            col = slice((g * R // 2 + pr) * LANES, (g * R // 2 + pr + 1) * LANES)
            o_ref[:, col] = (jnp.where(low, o_even, o_odd) * _silu(z_ref[:, col])).astype(o_ref.dtype)


def _swa_attn(sinks, q, k_dup, v_dup, z):
    S, D = q.shape
    W = SWA_WINDOW
    wide = SWA_KV_HEADS * LANES
    prev = lambda i: (jnp.maximum(i - 1, 0), 0)
    cur = lambda i: (i, 0)
    return pl.pallas_call(
        _swa_kernel,
        grid=(S // W,),
        in_specs=[pl.BlockSpec(memory_space=pltpu.SMEM),
                  pl.BlockSpec((W, D), cur),
                  pl.BlockSpec((W, wide), prev), pl.BlockSpec((W, wide), cur),
                  pl.BlockSpec((W, wide), prev), pl.BlockSpec((W, wide), cur),
                  pl.BlockSpec((W, D), cur)],
        out_specs=pl.BlockSpec((W, D), cur),
        out_shape=jax.ShapeDtypeStruct((S, D), BF16),
        compiler_params=_params("parallel"),
        name="swa_attn",
    )(sinks, q, k_dup, k_dup, v_dup, v_dup, z)


def _rope_tables(positions, head_dim, scale):
    rot = head_dim // ROPE_FRACTION
    half = rot // 2
    inv_freq = ROPE_THETA ** (-jnp.arange(0, rot, 2, dtype=F32) / rot)
    ang = positions.reshape(-1).astype(F32)[:, None] * inv_freq
    cos, sin = jnp.cos(ang), jnp.sin(ang)
    S = cos.shape[0]
    rest = head_dim - rot
    cos_h = jnp.concatenate([cos, cos, jnp.ones((S, rest), F32)], axis=1)
    sin_h = jnp.concatenate([-sin, sin, jnp.zeros((S, rest), F32)], axis=1)
    reps = LANES // head_dim
    return (jnp.tile(cos_h, (1, reps)) * scale, jnp.tile(sin_h, (1, reps)) * scale, half, head_dim)


def _hgrn_esum():
    row_s = np.arange(HG_SUB * HG_HEAD_DIM) // HG_HEAD_DIM
    col_s = np.arange(HG_CHUNK) % HG_SUB
    return jnp.asarray(row_s[:, None] == col_s[None, :], dtype=BF16)


def _nsa_constants(S):
    n_cmp = S // CMP_STRIDE - 1
    ncp = S // CMP_STRIDE
    n_slc = S // SLC_BLOCK
    ratio = SLC_BLOCK // CMP_STRIDE
    i = np.arange(ncp)[:, None]
    j = np.arange(LANES)[None, :]
    agg = ((i >= ratio * j - CMP_BLOCK // CMP_STRIDE + 1) & (i <= ratio * j + ratio - 1)
           & (i < n_cmp) & (j < n_slc))
    onehot = (np.arange(S)[:, None] // SLC_BLOCK) == j
    return jnp.asarray(onehot, dtype=BF16), jnp.asarray(agg.T, dtype=BF16)


def _rope_tables_t(positions, head_dim):
    rot = head_dim // ROPE_FRACTION
    inv_freq = ROPE_THETA ** (-jnp.arange(0, rot, 2, dtype=F32) / rot)
    ang = inv_freq[:, None] * positions.reshape(-1).astype(F32)[None, :]
    cos, sin = jnp.cos(ang), jnp.sin(ang)
    return jnp.concatenate([cos, cos], axis=0), jnp.concatenate([-sin, sin], axis=0), rot // 2


def _hgrn_layer(hb, w_in, g_norm, log_lb, log_1m_lb, esum):
    proj = _mm(hb, w_in, out_dtype=F32, head_major=True, tn=1024)
    return _hgrn_rec(proj, log_lb, log_1m_lb, g_norm, esum)


def _nsa_layer(hb, positions, w_in, pos_k, w1_k, w2_k, pos_v, w1_v, w2_v):
    S = hb.shape[0]
    H, G, R, DH = NSA_HEADS, NSA_KV_GROUPS, NSA_REP, NSA_HEAD_DIM
    gw = G * DH
    assert gw == 512 and (H * DH) % gw == 0
    t_kc = H * DH // gw
    o_gate = H * DH + 6 * gw
    o_z = o_gate + 3 * H
    rope_k = _rope_tables(positions, DH, 1.0)
    tq = NSA_TQ

    q_t = _mm_t(hb, w_in, cols=(0, 1, t_kc), out_dtype=BF16, tk=tq,
                rope=_rope_tables_t(positions, DH), scale=DH ** -0.5 * LOG2_E)
    k_c = _mm(hb, w_in, cols=(t_kc, 1, 1), out_dtype=F32, rope=rope_k, head_major=True)
    v_c = _mm(hb, w_in, cols=(t_kc + 1, 1, 1), out_dtype=F32, head_major=True)
    ksw = _mm(hb, w_in, cols=(t_kc + 2, 2, 2), out_dtype=BF16, rope=rope_k, head_major=True)
    vsw_t = _mm_t(hb, w_in, cols=(t_kc + 3, 2, 2), out_dtype=BF16, tk=tq)
    z = _mm(hb, w_in[:, o_z:], out_dtype=F32)
    src = np.full(G * LANES, 3 * H, dtype=np.int32)
    for g in range(G):
        for b in range(3):
            for r in range(R):
                c = b * R + r
                src[g * LANES + 8 * c:g * LANES + 8 * c + 8] = b * H + g * R + r
    w_gate = jnp.concatenate([w_in[:, o_gate:o_z], jnp.zeros((w_in.shape[0], 1), F32)], axis=1)[:, src]
    gate_t = _mm_t(hb, w_gate, out_dtype=F32, tk=tq)

    kc = _nsa_compress(k_c, pos_k, w1_k, w2_k, transposed=False)
    vc_t = _nsa_compress(v_c, pos_v, w1_v, w2_v, transposed=True)
    onehot, agg_t = _nsa_constants(S)
    return _nsa_attn(q_t, kc, vc_t, ksw, vsw_t, onehot, agg_t, gate_t, z)


def _swa_layer(hb, positions, w_in, sinks):
    H, KV, DH = SWA_HEADS, SWA_KV_HEADS, SWA_HEAD_DIM
    o_k = H * DH
    o_v = o_k + KV * DH
    o_z = o_v + KV * DH
    tn = 512
    assert o_k % tn == 0 and o_z % tn == 0
    dup = np.concatenate([np.tile(np.arange(g * DH, (g + 1) * DH), 2) for g in range(KV)])
    rope_q = _rope_tables(positions, DH, DH ** -0.5)
    rope_k = _rope_tables(positions, DH, 1.0)
    q = _mm(hb, w_in, cols=(0, 1, o_k // tn), out_dtype=BF16, rope=rope_q, tn=tn)
    k_dup = _mm(hb, w_in[:, o_k:o_v][:, dup], out_dtype=BF16, rope=rope_k)
    v_dup = _mm(hb, w_in[:, o_v:o_z][:, dup], out_dtype=BF16)
    z = _mm(hb, w_in, cols=(o_z // tn, 1, o_k // tn), out_dtype=F32, tn=tn)
    return _swa_attn(sinks, q, k_dup, v_dup, z)


def kernel(x, positions, hgrn_lb_logits, l0_w_in, l0_g_norm, l0_w_out, l0_ln_g, l0_ln_b, l1_w_in, l1_cmp_pos_k, l1_cmp_w1_k, l1_cmp_w2_k, l1_cmp_pos_v, l1_cmp_w1_v, l1_cmp_w2_v, l1_w_out, l1_ln_g, l1_ln_b, l2_w_in, l2_sinks, l2_w_out, l2_ln_g, l2_ln_b, l3_w_in, l3_g_norm, l3_w_out, l3_ln_g, l3_ln_b):
    B, S, D = x.shape
    assert B == 1 and D == D_MODEL
    lb = jnp.cumsum(jax.nn.softmax(hgrn_lb_logits.astype(F32), axis=0), axis=0)
    lb = lb - lb[0:1]
    log_lb, log_1m_lb = jnp.log(lb), jnp.log1p(-lb)
    esum = _hgrn_esum()

    h = x.reshape(S, D)
    hb = h.astype(BF16)

    a = _hgrn_layer(hb, l0_w_in, l0_g_norm, log_lb[0], log_1m_lb[0], esum)
    h, hb = _outproj_ln(a, l0_w_out.astype(BF16), h, l0_ln_g, l0_ln_b)

    a = _nsa_layer(hb, positions, l1_w_in, l1_cmp_pos_k, l1_cmp_w1_k, l1_cmp_w2_k,
                   l1_cmp_pos_v, l1_cmp_w1_v, l1_cmp_w2_v)
    h, hb = _outproj_ln(a, l1_w_out.astype(BF16), h, l1_ln_g, l1_ln_b)

    a = _swa_layer(hb, positions, l2_w_in, l2_sinks)
    h, hb = _outproj_ln(a, l2_w_out.astype(BF16), h, l2_ln_g, l2_ln_b)

    a = _hgrn_layer(hb, l3_w_in, l3_g_norm, log_lb[1], log_1m_lb[1], esum)
    h, hb = _outproj_ln(a, l3_w_out.astype(BF16), h, l3_ln_g, l3_ln_b)
    return h.reshape(B, S, D)
```

```python
import functools

import numpy as np
import jax
import jax.numpy as jnp
from jax import lax
from jax.experimental import pallas as pl
from jax.experimental.pallas import tpu as pltpu

F32 = jnp.float32
BF16 = jnp.bfloat16

D_MODEL = 2048
DEPTH = 4
N_MIXERS = 3
DEEPNORM_ALPHA = (2 * DEPTH) ** 0.25
LN_EPS = 1e-5
RMS_EPS = 1e-6
ROPE_THETA = 500000.0
ROPE_FRACTION = 4

HG_HEAD_DIM = 128
HG_HEADS = D_MODEL // HG_HEAD_DIM
HG_CHUNK = 64
HG_SUB = 8
HG_TIME_BLOCK = 1024
LOG2_E = 1.4426950408889634

NSA_HEAD_DIM = 128
NSA_HEADS = D_MODEL // NSA_HEAD_DIM
NSA_KV_GROUPS = 4
NSA_REP = NSA_HEADS // NSA_KV_GROUPS
CMP_BLOCK = 32
CMP_STRIDE = 16
CMP_HIDDEN = 256
SLC_BLOCK = 64
SLC_TOPK = 16
NSA_WINDOW = 512
NSA_TQ = 256
NSA_ONES_ROWS = 16
FORCE_BONUS = 1.0e4
MASK_NEG = -1.0e30

SWA_HEAD_DIM = 64
SWA_HEADS = D_MODEL // SWA_HEAD_DIM
SWA_KV_HEADS = 4
SWA_REP = SWA_HEADS // SWA_KV_HEADS
SWA_WINDOW = 128

LANES = 128
VMEM_LIMIT_BYTES = 48 * 1024 * 1024

_NT = (((1,), (1,)), ((), ()))
_TN = (((0,), (0,)), ((), ()))


def _params(*sem):
    return pltpu.CompilerParams(dimension_semantics=sem, vmem_limit_bytes=VMEM_LIMIT_BYTES)


def _silu(x):
    return x * (1.0 / (1.0 + jnp.exp(-x)))


def _mm_kernel(*refs, rope_half, rope_period, head_major, n_chunks):
    if rope_half:
        x_ref, w_ref, c_ref, s_ref, o_ref, wb_ref = refs
    else:
        x_ref, w_ref, o_ref, wb_ref = refs

    @pl.when(pl.program_id(1) == 0)
    def _():
        wb_ref[...] = w_ref[...].astype(BF16)

    acc = jnp.dot(x_ref[...], wb_ref[...], preferred_element_type=F32)
    if rope_half:
        cos = c_ref[...]
        sin = s_ref[...]
        lane = lax.broadcasted_iota(jnp.int32, cos.shape, 1)
        first_half = (lane & (rope_period - 1)) < rope_half
    for j in range(n_chunks):
        a = acc[:, j * LANES:(j + 1) * LANES]
        if rope_half:
            up = pltpu.roll(a, LANES - rope_half, 1)
            dn = pltpu.roll(a, rope_half, 1)
            a = a * cos + jnp.where(first_half, up, dn) * sin
        if head_major:
            o_ref[j] = a.astype(o_ref.dtype)
        else:
            o_ref[:, j * LANES:(j + 1) * LANES] = a.astype(o_ref.dtype)


def _col_tiles(w, cols, tn):
    return (0, 1, w.shape[1] // tn) if cols is None else cols


def _mm(x, w, *, out_dtype, cols=None, rope=None, head_major=False, tn=512):
    S, K = x.shape
    tm = min(1024, S)
    first, stride, n_tiles = _col_tiles(w, cols, tn)
    N = n_tiles * tn
    n_chunks = tn // LANES
    in_specs = [pl.BlockSpec((tm, K), lambda j, i: (i, 0)),
                pl.BlockSpec((K, tn), lambda j, i: (0, first + stride * j))]
    args = [x, w]
    half = period = 0
    if rope is not None:
        cos_t, sin_t, half, period = rope
        in_specs += [pl.BlockSpec((tm, LANES), lambda j, i: (i, 0))] * 2
        args += [cos_t, sin_t]
    if head_major:
        out_shape = jax.ShapeDtypeStruct((N // LANES, S, LANES), out_dtype)
        out_spec = pl.BlockSpec((n_chunks, tm, LANES), lambda j, i: (j, i, 0))
    else:
        out_shape = jax.ShapeDtypeStruct((S, N), out_dtype)
        out_spec = pl.BlockSpec((tm, tn), lambda j, i: (i, j))
    return pl.pallas_call(
        functools.partial(_mm_kernel, rope_half=half, rope_period=period,
                          head_major=head_major, n_chunks=n_chunks),
        grid=(n_tiles, S // tm),
        in_specs=in_specs,
        out_specs=out_spec,
        out_shape=out_shape,
        scratch_shapes=[pltpu.VMEM((K, tn), BF16)],
        compiler_params=_params("arbitrary", "arbitrary"),
        name="in_proj",
    )(*args)


def _mm_t_kernel(*refs, rope_half, scale, n_heads, n_tiles, tk):
    if rope_half:
        w_ref, x_ref, c_ref, s_ref, o_ref, wt_ref = refs
    else:
        w_ref, x_ref, o_ref, wt_ref = refs

    @pl.when(pl.program_id(1) == 0)
    def _():
        wt_ref[...] = w_ref[...].T.astype(BF16)

    acc = lax.dot_general(wt_ref[...], x_ref[...], _NT, preferred_element_type=F32)
    for c in range(n_heads):
        a = acc[c * LANES:(c + 1) * LANES]
        if scale != 1.0:
            a = a * scale
        if rope_half:
            rot = 2 * rope_half
            swapped = jnp.concatenate([a[rope_half:rot], a[:rope_half]], axis=0)
            a = jnp.concatenate([a[:rot] * c_ref[...] + swapped * s_ref[...], a[rot:]], axis=0)
        for b in range(n_tiles):
            o_ref[c, b] = a[:, b * tk:(b + 1) * tk].astype(o_ref.dtype)


def _mm_t(x, w, *, out_dtype, tk, cols=None, rope=None, scale=1.0, tn=512):
    S, K = x.shape
    tm = min(1024, S)
    first, stride, n_tiles = _col_tiles(w, cols, tn)
    N = n_tiles * tn
    in_specs = [pl.BlockSpec((K, tn), lambda j, i: (0, first + stride * j)),
                pl.BlockSpec((tm, K), lambda j, i: (i, 0))]
    args = [w, x]
    half = 0
    if rope is not None:
        cos_t, sin_t, half = rope
        in_specs += [pl.BlockSpec((2 * half, tm), lambda j, i: (0, i))] * 2
        args += [cos_t, sin_t]
    return pl.pallas_call(
        functools.partial(_mm_t_kernel, rope_half=half, scale=scale, n_heads=tn // LANES,
                          n_tiles=tm // tk, tk=tk),
        grid=(n_tiles, S // tm),
        in_specs=in_specs,
        out_specs=pl.BlockSpec((tn // LANES, tm // tk, LANES, tk), lambda j, i: (j, i, 0, 0)),
        out_shape=jax.ShapeDtypeStruct((N // LANES, S // tk, LANES, tk), out_dtype),
        scratch_shapes=[pltpu.VMEM((tn, K), BF16)],
        compiler_params=_params("arbitrary", "arbitrary"),
        name="in_proj_t",
    )(*args)


def _outproj_ln_kernel(a_ref, w_ref, h_ref, g_ref, b_ref, o_ref, ob_ref):
    y = jnp.dot(a_ref[...], w_ref[...], preferred_element_type=F32)
    u = DEEPNORM_ALPHA * h_ref[...] + y
    mu = jnp.mean(u, axis=-1, keepdims=True)
    xc = u - mu
    var = jnp.mean(xc * xc, axis=-1, keepdims=True)
    out = xc * lax.rsqrt(var + LN_EPS) * g_ref[...] + b_ref[...]
    o_ref[...] = out
    ob_ref[...] = out.astype(BF16)


def _outproj_ln(a, w, h, g, b):
    S, D = h.shape
    tm = min(512, S)
    row = pl.BlockSpec((tm, D), lambda i: (i, 0))
    vec = pl.BlockSpec((1, D), lambda i: (0, 0))
    return pl.pallas_call(
        _outproj_ln_kernel,
        grid=(S // tm,),
        in_specs=[row, pl.BlockSpec((D, D), lambda i: (0, 0)), row, vec, vec],
        out_specs=[row, row],
        out_shape=[jax.ShapeDtypeStruct((S, D), F32), jax.ShapeDtypeStruct((S, D), BF16)],
        compiler_params=_params("parallel"),
        name="outproj_ln",
    )(a, w, h, g.reshape(1, D), b.reshape(1, D))


def _hgrn_kernel(q_ref, f_ref, v_ref, z_ref, llb_ref, l1m_ref, gn_ref, esum_ref,
                 o_ref, st_ref, ycat_ref, u_ref, *, n_chunks):
    C, SUB, DK = HG_CHUNK, HG_SUB, HG_HEAD_DIM
    n_lvl = (C // (2 * SUB)).bit_length()

    @pl.when(pl.program_id(1) == 0)
    def _():
        st_ref[...] = jnp.zeros_like(st_ref)

    r = lax.broadcasted_iota(jnp.int32, (C, C), 0)
    c = lax.broadcasted_iota(jnp.int32, (C, C), 1)
    causal = c <= r
    halves = [SUB << lvl for lvl in range(n_lvl)]
    cum_rows = [causal, jnp.ones((C, C), jnp.bool_)]
    for h in halves:
        same_h = (r // h) == (c // h)
        cum_rows += [causal & same_h, same_h]
    cum_one = jnp.concatenate([m.astype(F32) for m in cum_rows], axis=0).astype(BF16)
    cum_mat = jnp.concatenate([cum_one] * 3, axis=1)
    diag_mask = causal & ((r // SUB) == (c // SUB))
    log_lb = llb_ref[...]
    log_1m_lb = l1m_ref[...]
    chunks = [slice(ci * C, (ci + 1) * C) for ci in range(n_chunks)]

    def rows_to_lanes(x):
        return jnp.concatenate([x[sl] for sl in chunks], axis=1)

    def lanes_to_rows(x):
        return jnp.concatenate([x[:, ci * DK:(ci + 1) * DK] for ci in range(n_chunks)], axis=0)

    q = q_ref[...]
    fl = f_ref[...]
    vb = v_ref[...].astype(BF16)
    log_sig = jnp.minimum(fl, 0.0) - jnp.log(1.0 + jnp.exp(-jnp.abs(fl)))
    t = log_1m_lb + log_sig
    log_f = jnp.maximum(log_lb, t) + jnp.log(1.0 + jnp.exp(-jnp.abs(log_lb - t)))
    log2_k = (t - fl) * LOG2_E
    k = jnp.exp2(log2_k)
    lf = rows_to_lanes(log_f)
    lf_hi = lf.astype(BF16)
    lf_r1 = lf - lf_hi.astype(F32)
    lf_mid = lf_r1.astype(BF16)
    lf_lo = (lf_r1 - lf_mid.astype(F32)).astype(BF16)
    cums = jnp.dot(cum_mat, jnp.concatenate([lf_hi, lf_mid, lf_lo], axis=0),
                   preferred_element_type=F32) * LOG2_E
    cum = [lanes_to_rows(cums[i * C:(i + 1) * C]) for i in range(2 + 2 * n_lvl)]
    b_full, b_tot = cum[0], cum[1]
    qe = (q * jnp.exp2(b_full)).astype(BF16)
    kd = (k * jnp.exp2(b_tot - b_full)).astype(BF16)
    q_lvl = [q * jnp.exp2(cum[2 + 2 * l]) for l in range(n_lvl)]
    k_lvl = [k * jnp.exp2(cum[3 + 2 * l] - cum[2 + 2 * l]) for l in range(n_lvl)]
    b_loc = cum[2]
    c_row = b_loc - log2_k

    def placed(x, lo):
        parts = [jnp.zeros((lo, DK), F32), x, jnp.zeros((C - lo - x.shape[0], DK), F32)]
        return jnp.concatenate([p for p in parts if p.shape[0]], axis=0)

    o_intra, a_offs = [], []
    for ci, ch in enumerate(chunks):
        for i in range(C // (2 * SUB)):
            base = ci * C + i * 2 * SUB
            sa, sb = slice(base, base + SUB), slice(base + SUB, base + 2 * SUB)
            for s in range(SUB):
                ya = q[sa] * jnp.exp2(jnp.minimum(b_loc[sa] - c_row[base + s:base + s + 1, :],
                                                  log2_k[base + s:base + s + 1, :]))
                yb = q[sb] * jnp.exp2(jnp.minimum(b_loc[sb] - c_row[base + SUB + s:base + SUB + s + 1, :],
                                                  log2_k[base + SUB + s:base + SUB + s + 1, :]))
                ycat_ref[base:base + 2 * SUB, s * DK:(s + 1) * DK] = (
                    jnp.concatenate([ya, yb], axis=0).astype(BF16))
        lhs, rhs = [], []
        for l, h in enumerate(halves):
            q_c, k_c = q_lvl[l][ch], k_lvl[l][ch]
            for lo in range(0, C, 2 * h):
                lhs.append(placed(q_c[lo + h:lo + 2 * h], lo + h))
                rhs.append(placed(k_c[lo:lo + h], lo))
        a_offs.append(lax.dot_general(jnp.concatenate(lhs, axis=1).astype(BF16),
                                      jnp.concatenate(rhs, axis=1).astype(BF16),
                                      _NT, preferred_element_type=F32))
        u_ref[ci] = lax.dot_general(vb[ch], kd[ch], _TN, preferred_element_type=F32)
    a_diag = jnp.dot(ycat_ref[...], esum_ref[...], preferred_element_type=F32)
    for ci, ch in enumerate(chunks):
        a = jnp.where(diag_mask, a_diag[ch], 0.0) + a_offs[ci]
        o_intra.append(jnp.dot(a.astype(BF16), vb[ch], preferred_element_type=F32))

    st = st_ref[...]
    outs = []
    for ci, ch in enumerate(chunks):
        outs.append(o_intra[ci] + lax.dot_general(qe[ch], st.astype(BF16), _NT,
                                                  preferred_element_type=F32))
        decay = jnp.exp2(jnp.tile(b_tot[ci * C:ci * C + 8], (DK // 8, 1)))
        st = st * decay + u_ref[ci]
    st_ref[...] = st

    o = jnp.concatenate(outs, axis=0)
    ms = jnp.mean(o * o, axis=-1, keepdims=True)
    out = o * lax.rsqrt(ms + RMS_EPS) * gn_ref[...] * _silu(z_ref[...])
    o_ref[...] = out.astype(o_ref.dtype)


def _hgrn_rec(proj, log_lb, log_1m_lb, g_norm, esum):
    S = proj.shape[1]
    D, DK, H = D_MODEL, HG_HEAD_DIM, HG_HEADS
    T = min(HG_TIME_BLOCK, S)

    def col(off):
        return pl.BlockSpec((None, T, DK), lambda h, t, off=off: (off + h, t, 0))

    vec = pl.BlockSpec((1, DK), lambda h, t: (0, h))
    return pl.pallas_call(
        functools.partial(_hgrn_kernel, n_chunks=T // HG_CHUNK),
        grid=(H, S // T),
        in_specs=[col(0), col(H), col(2 * H), col(3 * H), vec, vec, vec,
                  pl.BlockSpec(esum.shape, lambda h, t: (0, 0))],
        out_specs=pl.BlockSpec((T, DK), lambda h, t: (t, h)),
        out_shape=jax.ShapeDtypeStruct((S, D), BF16),
        scratch_shapes=[pltpu.VMEM((DK, DK), F32),
                        pltpu.VMEM((T, HG_SUB * DK), BF16),
                        pltpu.VMEM((T // HG_CHUNK, DK, DK), F32)],
        compiler_params=_params("parallel", "arbitrary"),
        name="hgrn_rec",
    )(proj, proj, proj, proj, log_lb.reshape(1, D), log_1m_lb.reshape(1, D),
      g_norm.reshape(1, D), esum)


def _nsa_compress_kernel(x_ref, pos_ref, w1_ref, w2_ref, o_ref, *, transposed):
    DH = NSA_HEAD_DIM
    n = x_ref.shape[0] // CMP_STRIDE
    h1 = jnp.zeros((n, CMP_HIDDEN), F32)
    h2 = jnp.zeros((n, CMP_HIDDEN), F32)
    for l in range(CMP_STRIDE):
        x_l = x_ref[pl.ds(l, n, stride=CMP_STRIDE), :]
        l2 = CMP_STRIDE + l
        h1 = h1 + jnp.dot((x_l + pos_ref[l:l + 1, :]).astype(BF16),
                          w1_ref[l * DH:(l + 1) * DH, :].astype(BF16), preferred_element_type=F32)
        h2 = h2 + jnp.dot((x_l + pos_ref[l2:l2 + 1, :]).astype(BF16),
                          w1_ref[l2 * DH:(l2 + 1) * DH, :].astype(BF16), preferred_element_type=F32)
    hid = h1 + pltpu.roll(h2, n - 1, 0)
    act = _silu(hid).astype(BF16)
    if transposed:
        o_ref[...] = lax.dot_general(w2_ref[...].T.astype(BF16), act, _NT,
                                     preferred_element_type=F32).astype(o_ref.dtype)
    else:
        o_ref[...] = jnp.dot(act, w2_ref[...].astype(BF16),
                             preferred_element_type=F32).astype(o_ref.dtype)


def _nsa_compress(x, pos, w1, w2, *, transposed):
    G, S, DH = x.shape
    n = S // CMP_STRIDE
    whole = lambda a: pl.BlockSpec(a.shape, lambda g: (0,) * a.ndim)
    out_block = (None, DH, n) if transposed else (None, n, DH)
    return pl.pallas_call(
        functools.partial(_nsa_compress_kernel, transposed=transposed),
        grid=(G,),
        in_specs=[pl.BlockSpec((None, S, DH), lambda g: (g, 0, 0)), whole(pos), whole(w1), whole(w2)],
        out_specs=pl.BlockSpec(out_block, lambda g: (g, 0, 0)),
        out_shape=jax.ShapeDtypeStruct((G,) + out_block[1:], BF16),
        compiler_params=_params("parallel"),
        name="nsa_compress",
    )(x, pos, w1, w2)


def _nsa_kernel(q_ref, kc_ref, vct_ref, ks_ref, vst_ref, kw_ref, vwt_ref, oh_ref, aggt_ref,
                gate_ref, z_ref, o_ref, m_ref, acc_ref, s_ref, *, tq):
    R, DH = NSA_REP, NSA_HEAD_DIM
    qb = pl.program_id(1)
    start = qb * tq
    q = jnp.concatenate([q_ref[r, 0] for r in range(R)], axis=1)
    ncp = kc_ref.shape[1]

    def per_head(x):
        return jnp.concatenate([x] * R, axis=1)

    def rows8(x, n):
        return jnp.tile(x, (n // 8, 1))

    n_c = lax.broadcasted_iota(jnp.int32, (ncp, tq), 0)
    t_c = start + lax.broadcasted_iota(jnp.int32, (ncp, tq), 1)
    cmask = per_head((n_c * CMP_STRIDE + CMP_BLOCK - 1) <= t_c)
    s = jnp.where(cmask, jnp.dot(kc_ref[0], q, preferred_element_type=F32), -jnp.inf)
    m = jnp.max(s, axis=0, keepdims=True)
    m = jnp.where(m == -jnp.inf, 0.0, m)
    e = jnp.exp2(s - m)
    p = e * (1.0 / jnp.maximum(jnp.sum(e, axis=0, keepdims=True), 1e-30))
    o_cmp = jnp.dot(vct_ref[0], p.astype(BF16), preferred_element_type=F32)
    p_sum = p[:, :tq]
    for r in range(1, R):
        p_sum = p_sum + p[:, r * tq:(r + 1) * tq]

    def col_max8(x):
        mx = x[0:8]
        for i in range(1, x.shape[0] // 8):
            mx = jnp.maximum(mx, x[8 * i:8 * i + 8])
        for shift in (4, 2, 1):
            mx = jnp.maximum(mx, pltpu.roll(mx, shift, 0))
        return mx

    def key_rows(ref, kt):
        return ref[0, pl.ds(pl.multiple_of(kt * tq, tq), tq), :]

    c_k = lax.broadcasted_iota(jnp.int32, (tq, tq), 0)
    i_q = lax.broadcasted_iota(jnp.int32, (tq, tq), 1)
    causal = per_head(c_k <= i_q)
    no_old = jnp.where(qb >= 2, 0, NSA_WINDOW)
    no_mid = jnp.where(qb >= 1, 0, NSA_WINDOW)
    old_ok = per_head((2 * tq + i_q - c_k + no_old) < NSA_WINDOW)
    mid_ok = per_head((tq + i_q - c_k + no_mid) < NSA_WINDOW)
    kt_old, kt_mid = jnp.maximum(qb - 2, 0), jnp.maximum(qb - 1, 0)
    k_win = jnp.concatenate([key_rows(kw_ref, kt_old), key_rows(kw_ref, kt_mid),
                             key_rows(kw_ref, qb)], axis=0)
    s_w = jnp.where(jnp.concatenate([old_ok, mid_ok, causal], axis=0),
                    jnp.dot(k_win, q, preferred_element_type=F32), MASK_NEG)
    p_w = jnp.exp2(s_w - rows8(col_max8(s_w), 3 * tq)).astype(BF16)
    v_win = jnp.concatenate([vwt_ref[0, kt_old], vwt_ref[0, kt_mid], vwt_ref[0, qb]], axis=1)
    acc_w = jnp.dot(jnp.concatenate([v_win, jnp.ones((NSA_ONES_ROWS, 3 * tq), BF16)], axis=0),
                    p_w, preferred_element_type=F32)
    o_win = acc_w[:DH] * rows8(1.0 / acc_w[DH:DH + 8], DH)

    aggt = aggt_ref[...]
    p_hi = p_sum.astype(BF16)
    p_lo = (p_sum - p_hi.astype(F32)).astype(BF16)
    imp = (jnp.dot(aggt, p_hi, preferred_element_type=F32)
           + jnp.dot(aggt, p_lo, preferred_element_type=F32))
    j_s = lax.broadcasted_iota(jnp.int32, (LANES, tq), 0)
    t_s = start + lax.broadcasted_iota(jnp.int32, (LANES, tq), 1)
    cur = lax.shift_right_logical(t_s, 6)
    forced = (j_s == 0) | (j_s == cur) | (j_s == cur - 1)
    allowed = j_s * SLC_BLOCK <= t_s
    sel = jnp.where(forced & allowed, 1.0, 0.0)
    val = jnp.where(allowed & ~forced, imp, -jnp.inf)
    row_f = j_s.astype(F32)
    for _ in range(SLC_TOPK - 3):
        mx = jnp.max(val, axis=0, keepdims=True)
        idx = jnp.min(jnp.where(val == mx, row_f, float(LANES)), axis=0, keepdims=True)
        pick = row_f == idx
        sel = jnp.where(pick, 1.0, sel)
        val = jnp.where(pick, -jnp.inf, val)
    m_neg = jnp.where(sel > 0.0, 0.0, MASK_NEG).astype(BF16)
    q_aug = jnp.concatenate([q, per_head(m_neg)], axis=0)

    ones_rows = jnp.ones((NSA_ONES_ROWS, tq), BF16)

    def flash_init():
        m_ref[...] = jnp.full(m_ref.shape, -jnp.inf, F32)
        acc_ref[...] = jnp.zeros_like(acc_ref)

    def flash_update(s, vt_tile):
        m_prev = m_ref[...]
        m_next = jnp.maximum(m_prev, col_max8(s))
        alpha = jnp.exp2(m_prev - m_next)
        p = jnp.exp2(s - rows8(m_next, tq)).astype(BF16)
        v_aug = jnp.concatenate([vt_tile, ones_rows], axis=0)
        acc_ref[...] = (acc_ref[...] * rows8(alpha, DH + NSA_ONES_ROWS)
                        + jnp.dot(v_aug, p, preferred_element_type=F32))
        m_ref[...] = m_next

    def flash_result():
        acc = acc_ref[...]
        return acc[:DH] * rows8(1.0 / acc[DH:DH + 8], DH)

    def slc_scores(kt):
        oh = oh_ref[pl.ds(pl.multiple_of(kt * tq, tq), tq), :]
        k_aug = jnp.concatenate([key_rows(ks_ref, kt), oh], axis=1)
        return jnp.dot(k_aug, q_aug, preferred_element_type=F32)

    flash_init()
    s_ref[...] = slc_scores(0)

    def slc_step(kt):
        s_cur = s_ref[...]
        s_ref[...] = slc_scores(kt + 1)
        flash_update(s_cur, vst_ref[0, kt])

    def slc_quad(i, carry):
        for u in range(4):
            slc_step(4 * i + u)
        return carry

    lax.fori_loop(0, lax.shift_right_logical(qb, 2), slc_quad, 0)
    done = qb & ~3

    @pl.when((qb & 2) != 0)
    def _():
        slc_step(done)
        slc_step(done + 1)

    @pl.when((qb & 1) != 0)
    def _():
        slc_step(qb - 1)

    flash_update(jnp.where(causal, s_ref[...], MASK_NEG), vst_ref[0, qb])
    o_slc = flash_result()

    gate = 1.0 / (1.0 + jnp.exp(-gate_ref[0, 0]))
    for r in range(R):
        sl = slice(r * tq, (r + 1) * tq)
        g = [rows8(gate[8 * (b * R + r):8 * (b * R + r) + 8], DH) for b in range(3)]
        o_t = g[0] * o_cmp[:, sl] + g[1] * o_slc[:, sl] + g[2] * o_win[:, sl]
        z = z_ref[:, r * DH:(r + 1) * DH]
        o_ref[:, r * DH:(r + 1) * DH] = (o_t.T * _silu(z)).astype(o_ref.dtype)


def _nsa_attn(q_t, kc, vc_t, ksw, vsw_t, onehot, agg_t, gate_t, z):
    G, R, DH = NSA_KV_GROUPS, NSA_REP, NSA_HEAD_DIM
    S = ksw.shape[1]
    tq = NSA_TQ
    assert S % tq == 0 and NSA_WINDOW <= 2 * tq and S // SLC_BLOCK <= LANES
    ncp = kc.shape[1]
    n_t = S // tq
    rows = lambda off: pl.BlockSpec((1, S, DH), lambda g, i, off=off: (off + g, 0, 0))
    tiles_t = lambda off: pl.BlockSpec((1, n_t, DH, tq), lambda g, i, off=off: (off + g, 0, 0, 0))
    return pl.pallas_call(
        functools.partial(_nsa_kernel, tq=tq),
        grid=(G, n_t),
        in_specs=[pl.BlockSpec((R, 1, DH, tq), lambda g, i: (g, i, 0, 0)),
                  pl.BlockSpec((1, ncp, DH), lambda g, i: (g, 0, 0)),
                  pl.BlockSpec((1, DH, ncp), lambda g, i: (g, 0, 0)),
                  rows(0), tiles_t(0), rows(G), tiles_t(G),
                  pl.BlockSpec((S, LANES), lambda g, i: (0, 0)),
                  pl.BlockSpec((LANES, ncp), lambda g, i: (0, 0)),
                  pl.BlockSpec((1, 1, LANES, tq), lambda g, i: (g, i, 0, 0)),
                  pl.BlockSpec((tq, R * DH), lambda g, i: (i, g))],
        out_specs=pl.BlockSpec((tq, R * DH), lambda g, i: (i, g)),
        out_shape=jax.ShapeDtypeStruct((S, D_MODEL), BF16),
        scratch_shapes=[pltpu.VMEM((8, R * tq), F32),
                        pltpu.VMEM((DH + NSA_ONES_ROWS, R * tq), F32),
                        pltpu.VMEM((tq, R * tq), F32)],
        compiler_params=_params("parallel", "arbitrary"),
        name="nsa_attn",
    )(q_t, kc, vc_t, ksw, vsw_t, ksw, vsw_t, onehot, agg_t, gate_t, z)


def _swa_kernel(sink_ref, q_ref, kp_ref, kc_ref, vp_ref, vc_ref, z_ref, o_ref):
    W, R = SWA_WINDOW, SWA_REP
    n = pl.program_id(0)
    i_q = lax.broadcasted_iota(jnp.int32, (W, 2 * W), 0)
    c_k = lax.broadcasted_iota(jnp.int32, (W, 2 * W), 1)
    diff = i_q - (c_k - W)
    mask = (diff >= 0) & (diff < W) & ((n > 0) | (c_k >= W))
    lane = lax.broadcasted_iota(jnp.int32, (W, LANES), 1)
    low = lane < SWA_HEAD_DIM
    for g in range(SWA_KV_HEADS):
        gl = slice(g * LANES, (g + 1) * LANES)
        kk = jnp.concatenate([kp_ref[:, gl], kc_ref[:, gl]], axis=0)
        vv = jnp.concatenate([vp_ref[:, gl], vc_ref[:, gl]], axis=0)
        qs = []
        for r in range(R):
            h = g * R + r
            q2 = q_ref[:, (h // 2) * LANES:(h // 2 + 1) * LANES]
            qs.append(jnp.where(low if h % 2 == 0 else ~low, q2, jnp.zeros_like(q2)))
        s_all = lax.dot_general(jnp.concatenate(qs, axis=0), kk, _NT,
                                preferred_element_type=F32)
        ps, inv_l = [], []
        for r in range(R):
            sink = sink_ref[g * R + r]
            s = jnp.where(mask, s_all[r * W:(r + 1) * W], -jnp.inf)
            m = jnp.maximum(jnp.max(s, axis=1, keepdims=True), sink)
            e = jnp.exp(s - m)
            inv_l.append(1.0 / (jnp.sum(e, axis=1, keepdims=True) + jnp.exp(sink - m)))
            ps.append(e.astype(BF16))
        o_all = jnp.dot(jnp.concatenate(ps, axis=0), vv, preferred_element_type=F32)
        for pr in range(R // 2):
            o_even = o_all[(2 * pr) * W:(2 * pr + 1) * W] * inv_l[2 * pr]
            o_odd = o_all[(2 * pr + 1) * W:(2 * pr + 2) * W] * inv_l[2 * pr + 1]
            col = slice((g * R // 2 + pr) * LANES, (g * R // 2 + pr + 1) * LANES)
            o_ref[:, col] = (jnp.where(low, o_even, o_odd) * _silu(z_ref[:, col])).astype(o_ref.dtype)


def _swa_attn(sinks, q, k_dup, v_dup, z):
    S, D = q.shape
    W = SWA_WINDOW
    wide = SWA_KV_HEADS * LANES
    prev = lambda i: (jnp.maximum(i - 1, 0), 0)
    cur = lambda i: (i, 0)
    return pl.pallas_call(
        _swa_kernel,
        grid=(S // W,),
        in_specs=[pl.BlockSpec(memory_space=pltpu.SMEM),
                  pl.BlockSpec((W, D), cur),
                  pl.BlockSpec((W, wide), prev), pl.BlockSpec((W, wide), cur),
                  pl.BlockSpec((W, wide), prev), pl.BlockSpec((W, wide), cur),
                  pl.BlockSpec((W, D), cur)],
        out_specs=pl.BlockSpec((W, D), cur),
        out_shape=jax.ShapeDtypeStruct((S, D), BF16),
        compiler_params=_params("parallel"),
        name="swa_attn",
    )(sinks, q, k_dup, k_dup, v_dup, v_dup, z)


def _rope_tables(positions, head_dim, scale):
    rot = head_dim // ROPE_FRACTION
    half = rot // 2
    inv_freq = ROPE_THETA ** (-jnp.arange(0, rot, 2, dtype=F32) / rot)
    ang = positions.reshape(-1).astype(F32)[:, None] * inv_freq
    cos, sin = jnp.cos(ang), jnp.sin(ang)
    S = cos.shape[0]
    rest = head_dim - rot
    cos_h = jnp.concatenate([cos, cos, jnp.ones((S, rest), F32)], axis=1)
    sin_h = jnp.concatenate([-sin, sin, jnp.zeros((S, rest), F32)], axis=1)
    reps = LANES // head_dim
    return (jnp.tile(cos_h, (1, reps)) * scale, jnp.tile(sin_h, (1, reps)) * scale, half, head_dim)


def _hgrn_esum():
    row_s = np.arange(HG_SUB * HG_HEAD_DIM) // HG_HEAD_DIM
    col_s = np.arange(HG_CHUNK) % HG_SUB
    return jnp.asarray(row_s[:, None] == col_s[None, :], dtype=BF16)


def _nsa_constants(S):
    n_cmp = S // CMP_STRIDE - 1
    ncp = S // CMP_STRIDE
    n_slc = S // SLC_BLOCK
    ratio = SLC_BLOCK // CMP_STRIDE
    i = np.arange(ncp)[:, None]
    j = np.arange(LANES)[None, :]
    agg = ((i >= ratio * j - CMP_BLOCK // CMP_STRIDE + 1) & (i <= ratio * j + ratio - 1)
           & (i < n_cmp) & (j < n_slc))
    onehot = (np.arange(S)[:, None] // SLC_BLOCK) == j
    return jnp.asarray(onehot, dtype=BF16), jnp.asarray(agg.T, dtype=BF16)


def _rope_tables_t(positions, head_dim):
    rot = head_dim // ROPE_FRACTION
    inv_freq = ROPE_THETA ** (-jnp.arange(0, rot, 2, dtype=F32) / rot)
    ang = inv_freq[:, None] * positions.reshape(-1).astype(F32)[None, :]
    cos, sin = jnp.cos(ang), jnp.sin(ang)
    return jnp.concatenate([cos, cos], axis=0), jnp.concatenate([-sin, sin], axis=0), rot // 2


def _hgrn_layer(hb, w_in, g_norm, log_lb, log_1m_lb, esum):
    proj = _mm(hb, w_in, out_dtype=F32, head_major=True, tn=1024)
    return _hgrn_rec(proj, log_lb, log_1m_lb, g_norm, esum)


def _nsa_layer(hb, positions, w_in, pos_k, w1_k, w2_k, pos_v, w1_v, w2_v):
    S = hb.shape[0]
    H, G, R, DH = NSA_HEADS, NSA_KV_GROUPS, NSA_REP, NSA_HEAD_DIM
    gw = G * DH
    assert gw == 512 and (H * DH) % gw == 0
    t_kc = H * DH // gw
    o_gate = H * DH + 6 * gw
    o_z = o_gate + 3 * H
    rope_k = _rope_tables(positions, DH, 1.0)
    tq = NSA_TQ

    q_t = _mm_t(hb, w_in, cols=(0, 1, t_kc), out_dtype=BF16, tk=tq,
                rope=_rope_tables_t(positions, DH), scale=DH ** -0.5 * LOG2_E)
    k_c = _mm(hb, w_in, cols=(t_kc, 1, 1), out_dtype=F32, rope=rope_k, head_major=True)
    v_c = _mm(hb, w_in, cols=(t_kc + 1, 1, 1), out_dtype=F32, head_major=True)
    ksw = _mm(hb, w_in, cols=(t_kc + 2, 2, 2), out_dtype=BF16, rope=rope_k, head_major=True)
    vsw_t = _mm_t(hb, w_in, cols=(t_kc + 3, 2, 2), out_dtype=BF16, tk=tq)
    z = _mm(hb, w_in[:, o_z:], out_dtype=F32)
    src = np.full(G * LANES, 3 * H, dtype=np.int32)
    for g in range(G):
        for b in range(3):
            for r in range(R):
                c = b * R + r
                src[g * LANES + 8 * c:g * LANES + 8 * c + 8] = b * H + g * R + r
    w_gate = jnp.concatenate([w_in[:, o_gate:o_z], jnp.zeros((w_in.shape[0], 1), F32)], axis=1)[:, src]
    gate_t = _mm_t(hb, w_gate, out_dtype=F32, tk=tq)

    kc = _nsa_compress(k_c, pos_k, w1_k, w2_k, transposed=False)
    vc_t = _nsa_compress(v_c, pos_v, w1_v, w2_v, transposed=True)
    onehot, agg_t = _nsa_constants(S)
    return _nsa_attn(q_t, kc, vc_t, ksw, vsw_t, onehot, agg_t, gate_t, z)


def _swa_layer(hb, positions, w_in, sinks):
    H, KV, DH = SWA_HEADS, SWA_KV_HEADS, SWA_HEAD_DIM
    o_k = H * DH
    o_v = o_k + KV * DH
    o_z = o_v + KV * DH
    tn = 512
    assert o_k % tn == 0 and o_z % tn == 0
    dup = np.concatenate([np.tile(np.arange(g * DH, (g + 1) * DH), 2) for g in range(KV)])
    rope_q = _rope_tables(positions, DH, DH ** -0.5)
    rope_k = _rope_tables(positions, DH, 1.0)
    q = _mm(hb, w_in, cols=(0, 1, o_k // tn), out_dtype=BF16, rope=rope_q, tn=tn)
    k_dup = _mm(hb, w_in[:, o_k:o_v][:, dup], out_dtype=BF16, rope=rope_k)
    v_dup = _mm(hb, w_in[:, o_v:o_z][:, dup], out_dtype=BF16)
    z = _mm(hb, w_in, cols=(o_z // tn, 1, o_k // tn), out_dtype=F32, tn=tn)
    return _swa_attn(sinks, q, k_dup, v_dup, z)


def kernel(x, positions, hgrn_lb_logits, l0_w_in, l0_g_norm, l0_w_out, l0_ln_g, l0_ln_b, l1_w_in, l1_cmp_pos_k, l1_cmp_w1_k, l1_cmp_w2_k, l1_cmp_pos_v, l1_cmp_w1_v, l1_cmp_w2_v, l1_w_out, l1_ln_g, l1_ln_b, l2_w_in, l2_sinks, l2_w_out, l2_ln_g, l2_ln_b, l3_w_in, l3_g_norm, l3_w_out, l3_ln_g, l3_ln_b):
    B, S, D = x.shape
    assert B == 1 and D == D_MODEL
    lb = jnp.cumsum(jax.nn.softmax(hgrn_lb_logits.astype(F32), axis=0), axis=0)
    lb = lb - lb[0:1]
    log_lb, log_1m_lb = jnp.log(lb), jnp.log1p(-lb)
    esum = _hgrn_esum()

    h = x.reshape(S, D)
    hb = h.astype(BF16)

    a = _hgrn_layer(hb, l0_w_in, l0_g_norm, log_lb[0], log_1m_lb[0], esum)
    h, hb = _outproj_ln(a, l0_w_out.astype(BF16), h, l0_ln_g, l0_ln_b)

    a = _nsa_layer(hb, positions, l1_w_in, l1_cmp_pos_k, l1_cmp_w1_k, l1_cmp_w2_k,
                   l1_cmp_pos_v, l1_cmp_w1_v, l1_cmp_w2_v)
    h, hb = _outproj_ln(a, l1_w_out.astype(BF16), h, l1_ln_g, l1_ln_b)

    a = _swa_layer(hb, positions, l2_w_in, l2_sinks)
    h, hb = _outproj_ln(a, l2_w_out.astype(BF16), h, l2_ln_g, l2_ln_b)

    a = _hgrn_layer(hb, l3_w_in, l3_g_norm, log_lb[1], log_1m_lb[1], esum)
    h, hb = _outproj_ln(a, l3_w_out.astype(BF16), h, l3_ln_g, l3_ln_b)
    return h.reshape(B, S, D)
```

```python
import functools

import numpy as np
import jax
import jax.numpy as jnp
from jax import lax
from jax.experimental import pallas as pl
from jax.experimental.pallas import tpu as pltpu

F32 = jnp.float32
BF16 = jnp.bfloat16

D_MODEL = 2048
DEPTH = 4
N_MIXERS = 3
DEEPNORM_ALPHA = (2 * DEPTH) ** 0.25
LN_EPS = 1e-5
RMS_EPS = 1e-6
ROPE_THETA = 500000.0
ROPE_FRACTION = 4

HG_HEAD_DIM = 128
HG_HEADS = D_MODEL // HG_HEAD_DIM
HG_CHUNK = 64
HG_SUB = 8
HG_TIME_BLOCK = 1024
LOG2_E = 1.4426950408889634

NSA_HEAD_DIM = 128
NSA_HEADS = D_MODEL // NSA_HEAD_DIM
NSA_KV_GROUPS = 4
NSA_REP = NSA_HEADS // NSA_KV_GROUPS
CMP_BLOCK = 32
CMP_STRIDE = 16
CMP_HIDDEN = 256
SLC_BLOCK = 64
SLC_TOPK = 16
NSA_WINDOW = 512
NSA_TQ = 256
NSA_ONES_ROWS = 16
NSA_BLOCKS_PER_STEP = 2
FORCE_BONUS = 1.0e4
MASK_NEG = -1.0e30

SWA_HEAD_DIM = 64
SWA_HEADS = D_MODEL // SWA_HEAD_DIM
SWA_KV_HEADS = 4
SWA_REP = SWA_HEADS // SWA_KV_HEADS
SWA_WINDOW = 128

LANES = 128
VMEM_LIMIT_BYTES = 48 * 1024 * 1024

_NT = (((1,), (1,)), ((), ()))
_TN = (((0,), (0,)), ((), ()))


def _params(*sem):
    return pltpu.CompilerParams(dimension_semantics=sem, vmem_limit_bytes=VMEM_LIMIT_BYTES)


def _silu(x):
    return x * (1.0 / (1.0 + jnp.exp(-x)))


def _mm_kernel(*refs, rope_half, rope_period, rope_tiles, head_major, n_chunks):
    if rope_half:
        x_ref, w_ref, c_ref, s_ref, o_ref, wb_ref = refs
    else:
        x_ref, w_ref, o_ref, wb_ref = refs

    @pl.when(pl.program_id(1) == 0)
    def _():
        wb_ref[...] = w_ref[...].astype(BF16)

    acc = jnp.dot(x_ref[...], wb_ref[...], preferred_element_type=F32)

    def write(with_rope):
        if with_rope:
            cos = c_ref[...]
            sin = s_ref[...]
            lane = lax.broadcasted_iota(jnp.int32, cos.shape, 1)
            first_half = (lane & (rope_period - 1)) < rope_half
        for j in range(n_chunks):
            a = acc[:, j * LANES:(j + 1) * LANES]
            if with_rope:
                up = pltpu.roll(a, LANES - rope_half, 1)
                dn = pltpu.roll(a, rope_half, 1)
                a = a * cos + jnp.where(first_half, up, dn) * sin
            if head_major:
                o_ref[j] = a.astype(o_ref.dtype)
            else:
                o_ref[:, j * LANES:(j + 1) * LANES] = a.astype(o_ref.dtype)

    if rope_half and rope_tiles is not None:
        pl.when(pl.program_id(0) < rope_tiles)(functools.partial(write, True))
        pl.when(pl.program_id(0) >= rope_tiles)(functools.partial(write, False))
    else:
        write(bool(rope_half))


def _col_tiles(w, cols, tn):
    return (0, 1, w.shape[1] // tn) if cols is None else cols


def _mm(x, w, *, out_dtype, cols=None, rope=None, rope_tiles=None, head_major=False, tn=512):
    S, K = x.shape
    tm = min(1024, S)
    first, stride, n_tiles = _col_tiles(w, cols, tn)
    N = n_tiles * tn
    n_chunks = tn // LANES
    in_specs = [pl.BlockSpec((tm, K), lambda j, i: (i, 0)),
                pl.BlockSpec((K, tn), lambda j, i: (0, first + stride * j))]
    args = [x, w]
    half = period = 0
    if rope is not None:
        cos_t, sin_t, half, period = rope
        in_specs += [pl.BlockSpec((tm, LANES), lambda j, i: (i, 0))] * 2
        args += [cos_t, sin_t]
    if head_major:
        out_shape = jax.ShapeDtypeStruct((N // LANES, S, LANES), out_dtype)
        out_spec = pl.BlockSpec((n_chunks, tm, LANES), lambda j, i: (j, i, 0))
    else:
        out_shape = jax.ShapeDtypeStruct((S, N), out_dtype)
        out_spec = pl.BlockSpec((tm, tn), lambda j, i: (i, j))
    return pl.pallas_call(
        functools.partial(_mm_kernel, rope_half=half, rope_period=period, rope_tiles=rope_tiles,
                          head_major=head_major, n_chunks=n_chunks),
        grid=(n_tiles, S // tm),
        in_specs=in_specs,
        out_specs=out_spec,
        out_shape=out_shape,
        scratch_shapes=[pltpu.VMEM((K, tn), BF16)],
        compiler_params=_params("arbitrary", "arbitrary"),
        name="in_proj",
    )(*args)


def _mm_t_kernel(*refs, rope_half, scale, n_heads, n_tiles, tk):
    if rope_half:
        w_ref, x_ref, c_ref, s_ref, o_ref, wt_ref = refs
    else:
        w_ref, x_ref, o_ref, wt_ref = refs

    @pl.when(pl.program_id(1) == 0)
    def _():
        wt_ref[...] = w_ref[...].T.astype(BF16)

    acc = lax.dot_general(wt_ref[...], x_ref[...], _NT, preferred_element_type=F32)
    for c in range(n_heads):
        a = acc[c * LANES:(c + 1) * LANES]
        if scale != 1.0:
            a = a * scale
        if rope_half:
            rot = 2 * rope_half
            swapped = jnp.concatenate([a[rope_half:rot], a[:rope_half]], axis=0)
            a = jnp.concatenate([a[:rot] * c_ref[...] + swapped * s_ref[...], a[rot:]], axis=0)
        for b in range(n_tiles):
            o_ref[c, b] = a[:, b * tk:(b + 1) * tk].astype(o_ref.dtype)


def _mm_t(x, w, *, out_dtype, tk, cols=None, rope=None, scale=1.0, tn=512):
    S, K = x.shape
    tm = min(1024, S)
    first, stride, n_tiles = _col_tiles(w, cols, tn)
    N = n_tiles * tn
    in_specs = [pl.BlockSpec((K, tn), lambda j, i: (0, first + stride * j)),
                pl.BlockSpec((tm, K), lambda j, i: (i, 0))]
    args = [w, x]
    half = 0
    if rope is not None:
        cos_t, sin_t, half = rope
        in_specs += [pl.BlockSpec((2 * half, tm), lambda j, i: (0, i))] * 2
        args += [cos_t, sin_t]
    return pl.pallas_call(
        functools.partial(_mm_t_kernel, rope_half=half, scale=scale, n_heads=tn // LANES,
                          n_tiles=tm // tk, tk=tk),
        grid=(n_tiles, S // tm),
        in_specs=in_specs,
        out_specs=pl.BlockSpec((tn // LANES, tm // tk, LANES, tk), lambda j, i: (j, i, 0, 0)),
        out_shape=jax.ShapeDtypeStruct((N // LANES, S // tk, LANES, tk), out_dtype),
        scratch_shapes=[pltpu.VMEM((tn, K), BF16)],
        compiler_params=_params("arbitrary", "arbitrary"),
        name="in_proj_t",
    )(*args)


def _outproj_ln_kernel(a_ref, w_ref, h_ref, g_ref, b_ref, o_ref, ob_ref):
    y = jnp.dot(a_ref[...], w_ref[...], preferred_element_type=F32)
    u = DEEPNORM_ALPHA * h_ref[...] + y
    mu = jnp.mean(u, axis=-1, keepdims=True)
    xc = u - mu
    var = jnp.mean(xc * xc, axis=-1, keepdims=True)
    out = xc * lax.rsqrt(var + LN_EPS) * g_ref[...] + b_ref[...]
    o_ref[...] = out
    ob_ref[...] = out.astype(BF16)


def _outproj_ln(a, w, h, g, b):
    S, D = h.shape
    tm = min(512, S)
    row = pl.BlockSpec((tm, D), lambda i: (i, 0))
    vec = pl.BlockSpec((1, D), lambda i: (0, 0))
    return pl.pallas_call(
        _outproj_ln_kernel,
        grid=(S // tm,),
        in_specs=[row, pl.BlockSpec((D, D), lambda i: (0, 0)), row, vec, vec],
        out_specs=[row, row],
        out_shape=[jax.ShapeDtypeStruct((S, D), F32), jax.ShapeDtypeStruct((S, D), BF16)],
        compiler_params=_params("parallel"),
        name="outproj_ln",
    )(a, w, h, g.reshape(1, D), b.reshape(1, D))


def _hgrn_kernel(q_ref, f_ref, v_ref, z_ref, llb_ref, l1m_ref, gn_ref, esum_ref,
                 o_ref, st_ref, ycat_ref, u_ref, *, n_chunks):
    C, SUB, DK = HG_CHUNK, HG_SUB, HG_HEAD_DIM
    n_lvl = (C // (2 * SUB)).bit_length()

    @pl.when(pl.program_id(1) == 0)
    def _():
        st_ref[...] = jnp.zeros_like(st_ref)

    r = lax.broadcasted_iota(jnp.int32, (C, C), 0)
    c = lax.broadcasted_iota(jnp.int32, (C, C), 1)
    causal = c <= r
    halves = [SUB << lvl for lvl in range(n_lvl)]
    cum_rows = [causal, jnp.ones((C, C), jnp.bool_)]
    for h in halves:
        same_h = (r // h) == (c // h)
        cum_rows += [causal & same_h, same_h]
    cum_one = jnp.concatenate([m.astype(F32) for m in cum_rows], axis=0).astype(BF16)
    cum_mat = jnp.concatenate([cum_one] * 3, axis=1)
    diag_mask = causal & ((r // SUB) == (c // SUB))
    log_lb = llb_ref[...]
    log_1m_lb = l1m_ref[...]
    chunks = [slice(ci * C, (ci + 1) * C) for ci in range(n_chunks)]

    def rows_to_lanes(x):
        return jnp.concatenate([x[sl] for sl in chunks], axis=1)

    def lanes_to_rows(x):
        return jnp.concatenate([x[:, ci * DK:(ci + 1) * DK] for ci in range(n_chunks)], axis=0)

    q = q_ref[...]
    fl = f_ref[...]
    vb = v_ref[...].astype(BF16)
    log_sig = jnp.minimum(fl, 0.0) - jnp.log(1.0 + jnp.exp(-jnp.abs(fl)))
    t = log_1m_lb + log_sig
    log_f = jnp.maximum(log_lb, t) + jnp.log(1.0 + jnp.exp(-jnp.abs(log_lb - t)))
    log2_k = (t - fl) * LOG2_E
    k = jnp.exp2(log2_k)
    lf = rows_to_lanes(log_f)
    lf_hi = lf.astype(BF16)
    lf_r1 = lf - lf_hi.astype(F32)
    lf_mid = lf_r1.astype(BF16)
    lf_lo = (lf_r1 - lf_mid.astype(F32)).astype(BF16)
    cums = jnp.dot(cum_mat, jnp.concatenate([lf_hi, lf_mid, lf_lo], axis=0),
                   preferred_element_type=F32) * LOG2_E
    cum = [lanes_to_rows(cums[i * C:(i + 1) * C]) for i in range(2 + 2 * n_lvl)]
    b_full, b_tot = cum[0], cum[1]
    qe = (q * jnp.exp2(b_full)).astype(BF16)
    kd = (k * jnp.exp2(b_tot - b_full)).astype(BF16)
    q_lvl = [q * jnp.exp2(cum[2 + 2 * l]) for l in range(n_lvl)]
    k_lvl = [k * jnp.exp2(cum[3 + 2 * l] - cum[2 + 2 * l]) for l in range(n_lvl)]
    b_loc = cum[2]
    c_row = b_loc - log2_k

    def placed(x, lo):
        parts = [jnp.zeros((lo, DK), F32), x, jnp.zeros((C - lo - x.shape[0], DK), F32)]
        return jnp.concatenate([p for p in parts if p.shape[0]], axis=0)

    o_intra, a_offs = [], []
    for ci, ch in enumerate(chunks):
        for i in range(C // (2 * SUB)):
            base = ci * C + i * 2 * SUB
            sa, sb = slice(base, base + SUB), slice(base + SUB, base + 2 * SUB)
            for s in range(SUB):
                ya = q[sa] * jnp.exp2(jnp.minimum(b_loc[sa] - c_row[base + s:base + s + 1, :],
                                                  log2_k[base + s:base + s + 1, :]))
                yb = q[sb] * jnp.exp2(jnp.minimum(b_loc[sb] - c_row[base + SUB + s:base + SUB + s + 1, :],
                                                  log2_k[base + SUB + s:base + SUB + s + 1, :]))
                ycat_ref[base:base + 2 * SUB, s * DK:(s + 1) * DK] = (
                    jnp.concatenate([ya, yb], axis=0).astype(BF16))
        lhs, rhs = [], []
        for l, h in enumerate(halves):
            q_c, k_c = q_lvl[l][ch], k_lvl[l][ch]
            for lo in range(0, C, 2 * h):
                lhs.append(placed(q_c[lo + h:lo + 2 * h], lo + h))
                rhs.append(placed(k_c[lo:lo + h], lo))
        a_offs.append(lax.dot_general(jnp.concatenate(lhs, axis=1).astype(BF16),
                                      jnp.concatenate(rhs, axis=1).astype(BF16),
                                      _NT, preferred_element_type=F32))
        u_ref[ci] = lax.dot_general(vb[ch], kd[ch], _TN, preferred_element_type=F32)
    a_diag = jnp.dot(ycat_ref[...], esum_ref[...], preferred_element_type=F32)
    for ci, ch in enumerate(chunks):
        a = jnp.where(diag_mask, a_diag[ch], 0.0) + a_offs[ci]
        o_intra.append(jnp.dot(a.astype(BF16), vb[ch], preferred_element_type=F32))

    st = st_ref[...]
    outs = []
    for ci, ch in enumerate(chunks):
        outs.append(o_intra[ci] + lax.dot_general(qe[ch], st.astype(BF16), _NT,
                                                  preferred_element_type=F32))
        decay = jnp.exp2(jnp.tile(b_tot[ci * C:ci * C + 8], (DK // 8, 1)))
        st = st * decay + u_ref[ci]
    st_ref[...] = st

    o = jnp.concatenate(outs, axis=0)
    ms = jnp.mean(o * o, axis=-1, keepdims=True)
    out = o * lax.rsqrt(ms + RMS_EPS) * gn_ref[...] * _silu(z_ref[...])
    o_ref[...] = out.astype(o_ref.dtype)


def _hgrn_rec(proj, log_lb, log_1m_lb, g_norm, esum):
    S = proj.shape[1]
    D, DK, H = D_MODEL, HG_HEAD_DIM, HG_HEADS
    T = min(HG_TIME_BLOCK, S)

    def col(off):
        return pl.BlockSpec((None, T, DK), lambda h, t, off=off: (off + h, t, 0))

    vec = pl.BlockSpec((1, DK), lambda h, t: (0, h))
    return pl.pallas_call(
        functools.partial(_hgrn_kernel, n_chunks=T // HG_CHUNK),
        grid=(H, S // T),
        in_specs=[col(0), col(H), col(2 * H), col(3 * H), vec, vec, vec,
                  pl.BlockSpec(esum.shape, lambda h, t: (0, 0))],
        out_specs=pl.BlockSpec((T, DK), lambda h, t: (t, h)),
        out_shape=jax.ShapeDtypeStruct((S, D), BF16),
        scratch_shapes=[pltpu.VMEM((DK, DK), F32),
                        pltpu.VMEM((T, HG_SUB * DK), BF16),
                        pltpu.VMEM((T // HG_CHUNK, DK, DK), F32)],
        compiler_params=_params("parallel", "arbitrary"),
        name="hgrn_rec",
    )(proj, proj, proj, proj, log_lb.reshape(1, D), log_1m_lb.reshape(1, D),
      g_norm.reshape(1, D), esum)


def _nsa_compress_kernel(x_ref, pos_ref, w1_ref, w2_ref, o_ref, *, transposed):
    DH = NSA_HEAD_DIM
    n = x_ref.shape[0] // CMP_STRIDE
    h1 = jnp.zeros((n, CMP_HIDDEN), F32)
    h2 = jnp.zeros((n, CMP_HIDDEN), F32)
    for l in range(CMP_STRIDE):
        x_l = x_ref[pl.ds(l, n, stride=CMP_STRIDE), :]
        l2 = CMP_STRIDE + l
        h1 = h1 + jnp.dot((x_l + pos_ref[l:l + 1, :]).astype(BF16),
                          w1_ref[l * DH:(l + 1) * DH, :].astype(BF16), preferred_element_type=F32)
        h2 = h2 + jnp.dot((x_l + pos_ref[l2:l2 + 1, :]).astype(BF16),
                          w1_ref[l2 * DH:(l2 + 1) * DH, :].astype(BF16), preferred_element_type=F32)
    hid = h1 + pltpu.roll(h2, n - 1, 0)
    act = _silu(hid).astype(BF16)
    if transposed:
        o_ref[...] = lax.dot_general(w2_ref[...].T.astype(BF16), act, _NT,
                                     preferred_element_type=F32).astype(o_ref.dtype)
    else:
        o_ref[...] = jnp.dot(act, w2_ref[...].astype(BF16),
                             preferred_element_type=F32).astype(o_ref.dtype)


def _nsa_compress(x, first, pos, w1, w2, *, transposed):
    G = NSA_KV_GROUPS
    _, S, DH = x.shape
    n = S // CMP_STRIDE
    whole = lambda a: pl.BlockSpec(a.shape, lambda g: (0,) * a.ndim)
    out_block = (None, DH, n) if transposed else (None, n, DH)
    return pl.pallas_call(
        functools.partial(_nsa_compress_kernel, transposed=transposed),
        grid=(G,),
        in_specs=[pl.BlockSpec((None, S, DH), lambda g: (first + g, 0, 0)),
                  whole(pos), whole(w1), whole(w2)],
        out_specs=pl.BlockSpec(out_block, lambda g: (g, 0, 0)),
        out_shape=jax.ShapeDtypeStruct((G,) + out_block[1:], BF16),
        compiler_params=_params("parallel"),
        name="nsa_compress",
    )(x, pos, w1, w2)


def _nsa_kernel(q_ref, kc_ref, vct_ref, ks_ref, vst_ref, kw_ref, vwt_ref, oh_ref, aggt_ref,
                gate_ref, z_ref, o_ref, m_ref, acc_ref, s_ref, *, tq, n_sub):
    R, DH = NSA_REP, NSA_HEAD_DIM
    ncp = kc_ref.shape[1]

    def per_head(x):
        return jnp.concatenate([x] * R, axis=1)

    def rows8(x, n):
        return jnp.tile(x, (n // 8, 1))

    def col_max8(x):
        mx = x[0:8]
        for i in range(1, x.shape[0] // 8):
            mx = jnp.maximum(mx, x[8 * i:8 * i + 8])
        for shift in (4, 2, 1):
            mx = jnp.maximum(mx, pltpu.roll(mx, shift, 0))
        return mx

    def key_rows(ref, kt):
        return ref[0, pl.ds(pl.multiple_of(kt * tq, tq), tq), :]

    c_k = lax.broadcasted_iota(jnp.int32, (tq, tq), 0)
    i_q = lax.broadcasted_iota(jnp.int32, (tq, tq), 1)
    causal = per_head(c_k <= i_q)
    aggt = aggt_ref[...]

    def front(sub):
        qb = pl.program_id(1) * n_sub + sub
        start = qb * tq
        q = jnp.concatenate([q_ref[r, sub] for r in range(R)], axis=1)

        n_c = lax.broadcasted_iota(jnp.int32, (ncp, tq), 0)
        t_c = start + lax.broadcasted_iota(jnp.int32, (ncp, tq), 1)
        cmask = per_head((n_c * CMP_STRIDE + CMP_BLOCK - 1) <= t_c)
        s = jnp.where(cmask, jnp.dot(kc_ref[0], q, preferred_element_type=F32), -jnp.inf)
        m = jnp.max(s, axis=0, keepdims=True)
        m = jnp.where(m == -jnp.inf, 0.0, m)
        e = jnp.exp2(s - m)
        p = e * (1.0 / jnp.maximum(jnp.sum(e, axis=0, keepdims=True), 1e-30))
        o_cmp = jnp.dot(vct_ref[0], p.astype(BF16), preferred_element_type=F32)
        p_sum = p[:, :tq]
        for r in range(1, R):
            p_sum = p_sum + p[:, r * tq:(r + 1) * tq]

        no_old = jnp.where(qb >= 2, 0, NSA_WINDOW)
        no_mid = jnp.where(qb >= 1, 0, NSA_WINDOW)
        old_ok = per_head((2 * tq + i_q - c_k + no_old) < NSA_WINDOW)
        mid_ok = per_head((tq + i_q - c_k + no_mid) < NSA_WINDOW)
        kt_old, kt_mid = jnp.maximum(qb - 2, 0), jnp.maximum(qb - 1, 0)
        k_win = jnp.concatenate([key_rows(kw_ref, kt_old), key_rows(kw_ref, kt_mid),
                                 key_rows(kw_ref, qb)], axis=0)
        s_w = jnp.where(jnp.concatenate([old_ok, mid_ok, causal], axis=0),
                        jnp.dot(k_win, q, preferred_element_type=F32), MASK_NEG)
        p_w = jnp.exp2(s_w - rows8(col_max8(s_w), 3 * tq)).astype(BF16)
        v_win = jnp.concatenate([vwt_ref[0, kt_old], vwt_ref[0, kt_mid], vwt_ref[0, qb]], axis=1)
        acc_w = jnp.dot(jnp.concatenate([v_win, jnp.ones((NSA_ONES_ROWS, 3 * tq), BF16)], axis=0),
                        p_w, preferred_element_type=F32)
        o_win = acc_w[:DH] * rows8(1.0 / acc_w[DH:DH + 8], DH)

        p_hi = p_sum.astype(BF16)
        p_lo = (p_sum - p_hi.astype(F32)).astype(BF16)
        imp = (jnp.dot(aggt, p_hi, preferred_element_type=F32)
               + jnp.dot(aggt, p_lo, preferred_element_type=F32))
        j_s = lax.broadcasted_iota(jnp.int32, (LANES, tq), 0)
        t_s = start + lax.broadcasted_iota(jnp.int32, (LANES, tq), 1)
        cur = lax.shift_right_logical(t_s, 6)
        forced = (j_s == 0) | (j_s == cur) | (j_s == cur - 1)
        allowed = j_s * SLC_BLOCK <= t_s
        sel = jnp.where(forced & allowed, 1.0, 0.0)
        val = jnp.where(allowed & ~forced, imp, -jnp.inf)
        row_f = j_s.astype(F32)
        for _ in range(SLC_TOPK - 3):
            mx = jnp.max(val, axis=0, keepdims=True)
            idx = jnp.min(jnp.where(val == mx, row_f, float(LANES)), axis=0, keepdims=True)
            pick = row_f == idx
            sel = jnp.where(pick, 1.0, sel)
            val = jnp.where(pick, -jnp.inf, val)
        m_neg = jnp.where(sel > 0.0, 0.0, MASK_NEG).astype(BF16)
        q_aug = jnp.concatenate([q, per_head(m_neg)], axis=0)
        return qb, q_aug, o_cmp, o_win

    ones_rows = jnp.ones((NSA_ONES_ROWS, tq), BF16)

    def flash_init():
        m_ref[...] = jnp.full(m_ref.shape, -jnp.inf, F32)
        acc_ref[...] = jnp.zeros_like(acc_ref)

    def flash_update(s, vt_tile):
        m_prev = m_ref[...]
        m_next = jnp.maximum(m_prev, col_max8(s))
        alpha = jnp.exp2(m_prev - m_next)
        p = jnp.exp2(s - rows8(m_next, tq)).astype(BF16)
        v_aug = jnp.concatenate([vt_tile, ones_rows], axis=0)
        acc_ref[...] = (acc_ref[...] * rows8(alpha, DH + NSA_ONES_ROWS)
                        + jnp.dot(v_aug, p, preferred_element_type=F32))
        m_ref[...] = m_next

    def flash_result():
        acc = acc_ref[...]
        return acc[:DH] * rows8(1.0 / acc[DH:DH + 8], DH)

    def back(sub, qb, q_aug, o_cmp, o_win):
        def slc_scores(kt):
            oh = oh_ref[pl.ds(pl.multiple_of(kt * tq, tq), tq), :]
            k_aug = jnp.concatenate([key_rows(ks_ref, kt), oh], axis=1)
            return jnp.dot(k_aug, q_aug, preferred_element_type=F32)

        flash_init()
        s_ref[...] = slc_scores(0)

        def slc_step(kt):
            s_cur = s_ref[...]
            s_ref[...] = slc_scores(kt + 1)
            flash_update(s_cur, vst_ref[0, kt])

        def slc_quad(i, carry):
            for u in range(4):
                slc_step(4 * i + u)
            return carry

        lax.fori_loop(0, lax.shift_right_logical(qb, 2), slc_quad, 0)
        done = qb & ~3

        @pl.when((qb & 2) != 0)
        def _():
            slc_step(done)
            slc_step(done + 1)

        @pl.when((qb & 1) != 0)
        def _():
            slc_step(qb - 1)

        flash_update(jnp.where(causal, s_ref[...], MASK_NEG), vst_ref[0, qb])
        o_slc = flash_result()

        head0 = pl.program_id(0) * R

        def gate(branch, r):
            logit = gate_ref[0, sub, pl.ds(branch * NSA_HEADS + head0 + r, 1), :]
            return jnp.broadcast_to(1.0 / (1.0 + jnp.exp(-logit)), (DH, tq))

        rows = slice(sub * tq, (sub + 1) * tq)
        for r in range(R):
            sl = slice(r * tq, (r + 1) * tq)
            g = [gate(b, r) for b in range(3)]
            o_t = g[0] * o_cmp[:, sl] + g[1] * o_slc[:, sl] + g[2] * o_win[:, sl]
            z = z_ref[rows, r * DH:(r + 1) * DH]
            o_ref[rows, r * DH:(r + 1) * DH] = (o_t.T * _silu(z)).astype(o_ref.dtype)

    fronts = [front(sub) for sub in range(n_sub)]
    for sub in range(n_sub):
        back(sub, *fronts[sub])


def _nsa_attn(q_t, kc, vc_t, ksw, vsw_t, onehot, agg_t, gate_t, z):
    G, R, DH = NSA_KV_GROUPS, NSA_REP, NSA_HEAD_DIM
    S = ksw.shape[1]
    tq, n_sub = NSA_TQ, NSA_BLOCKS_PER_STEP
    assert S % (n_sub * tq) == 0 and NSA_WINDOW <= 2 * tq and S // SLC_BLOCK <= LANES
    ncp = kc.shape[1]
    n_t = S // tq
    rows = lambda off: pl.BlockSpec((1, S, DH), lambda g, i, off=off: (off + g, 0, 0))
    tiles_t = lambda off: pl.BlockSpec((1, n_t, DH, tq), lambda g, i, off=off: (off + g, 0, 0, 0))
    return pl.pallas_call(
        functools.partial(_nsa_kernel, tq=tq, n_sub=n_sub),
        grid=(G, n_t // n_sub),
        in_specs=[pl.BlockSpec((R, n_sub, DH, tq), lambda g, i: (g, i, 0, 0)),
                  pl.BlockSpec((1, ncp, DH), lambda g, i: (g, 0, 0)),
                  pl.BlockSpec((1, DH, ncp), lambda g, i: (g, 0, 0)),
                  rows(0), tiles_t(0), rows(G), tiles_t(G),
                  pl.BlockSpec((S, LANES), lambda g, i: (0, 0)),
                  pl.BlockSpec((LANES, ncp), lambda g, i: (0, 0)),
                  pl.BlockSpec((1, n_sub, LANES, tq), lambda g, i: (0, i, 0, 0)),
                  pl.BlockSpec((n_sub * tq, R * DH), lambda g, i: (i, g))],
        out_specs=pl.BlockSpec((n_sub * tq, R * DH), lambda g, i: (i, g)),
        out_shape=jax.ShapeDtypeStruct((S, D_MODEL), BF16),
        scratch_shapes=[pltpu.VMEM((8, R * tq), F32),
                        pltpu.VMEM((DH + NSA_ONES_ROWS, R * tq), F32),
                        pltpu.VMEM((tq, R * tq), F32)],
        compiler_params=_params("parallel", "arbitrary"),
        name="nsa_attn",
    )(q_t, kc, vc_t, ksw, vsw_t, ksw, vsw_t, onehot, agg_t, gate_t, z)


def _swa_kernel(sink_ref, q_ref, kp_ref, kc_ref, vp_ref, vc_ref, z_ref, o_ref):
    W, R = SWA_WINDOW, SWA_REP
    n = pl.program_id(0)
    i_q = lax.broadcasted_iota(jnp.int32, (W, 2 * W), 0)
    c_k = lax.broadcasted_iota(jnp.int32, (W, 2 * W), 1)
    diff = i_q - (c_k - W)
    mask = (diff >= 0) & (diff < W) & ((n > 0) | (c_k >= W))
    lane = lax.broadcasted_iota(jnp.int32, (W, LANES), 1)
    low = lane < SWA_HEAD_DIM
    for g in range(SWA_KV_HEADS):
        gl = slice(g * LANES, (g + 1) * LANES)
        kk = jnp.concatenate([kp_ref[:, gl], kc_ref[:, gl]], axis=0)
        vv = jnp.concatenate([vp_ref[:, gl], vc_ref[:, gl]], axis=0)
        qs = []
        for r in range(R):
            h = g * R + r
            q2 = q_ref[:, (h // 2) * LANES:(h // 2 + 1) * LANES]
            qs.append(jnp.where(low if h % 2 == 0 else ~low, q2, jnp.zeros_like(q2)))
        s_all = lax.dot_general(jnp.concatenate(qs, axis=0), kk, _NT,
                                preferred_element_type=F32)
        ps, inv_l = [], []
        for r in range(R):
            sink = sink_ref[g * R + r]
            s = jnp.where(mask, s_all[r * W:(r + 1) * W], -jnp.inf)
            m = jnp.maximum(jnp.max(s, axis=1, keepdims=True), sink)
            e = jnp.exp(s - m)
            inv_l.append(1.0 / (jnp.sum(e, axis=1, keepdims=True) + jnp.exp(sink - m)))
            ps.append(e.astype(BF16))
        o_all = jnp.dot(jnp.concatenate(ps, axis=0), vv, preferred_element_type=F32)
        for pr in range(R // 2):
            o_even = o_all[(2 * pr) * W:(2 * pr + 1) * W] * inv_l[2 * pr]
            o_odd = o_all[(2 * pr + 1) * W:(2 * pr + 2) * W] * inv_l[2 * pr + 1]
            col = slice((g * R // 2 + pr) * LANES, (g * R // 2 + pr + 1) * LANES)
            o_ref[:, col] = (jnp.where(low, o_even, o_odd) * _silu(z_ref[:, col])).astype(o_ref.dtype)


def _swa_attn(sinks, q, kv_dup, z):
    S, D = q.shape
    W = SWA_WINDOW
    wide = SWA_KV_HEADS * LANES
    cur = lambda i: (i, 0)
    k_prev, k_cur = (lambda i: (jnp.maximum(i - 1, 0), 0)), cur
    v_prev, v_cur = (lambda i: (jnp.maximum(i - 1, 0), 1)), (lambda i: (i, 1))
    return pl.pallas_call(
        _swa_kernel,
        grid=(S // W,),
        in_specs=[pl.BlockSpec(memory_space=pltpu.SMEM),
                  pl.BlockSpec((W, D), cur),
                  pl.BlockSpec((W, wide), k_prev), pl.BlockSpec((W, wide), k_cur),
                  pl.BlockSpec((W, wide), v_prev), pl.BlockSpec((W, wide), v_cur),
                  pl.BlockSpec((W, D), cur)],
        out_specs=pl.BlockSpec((W, D), cur),
        out_shape=jax.ShapeDtypeStruct((S, D), BF16),
        compiler_params=_params("parallel"),
        name="swa_attn",
    )(sinks, q, kv_dup, kv_dup, kv_dup, kv_dup, z)


def _rope_tables(positions, head_dim, scale):
    rot = head_dim // ROPE_FRACTION
    half = rot // 2
    inv_freq = ROPE_THETA ** (-jnp.arange(0, rot, 2, dtype=F32) / rot)
    ang = positions.reshape(-1).astype(F32)[:, None] * inv_freq
    cos, sin = jnp.cos(ang), jnp.sin(ang)
    S = cos.shape[0]
    rest = head_dim - rot
    cos_h = jnp.concatenate([cos, cos, jnp.ones((S, rest), F32)], axis=1)
    sin_h = jnp.concatenate([-sin, sin, jnp.zeros((S, rest), F32)], axis=1)
    reps = LANES // head_dim
    return (jnp.tile(cos_h, (1, reps)) * scale, jnp.tile(sin_h, (1, reps)) * scale, half, head_dim)


def _hgrn_esum():
    row_s = np.arange(HG_SUB * HG_HEAD_DIM) // HG_HEAD_DIM
    col_s = np.arange(HG_CHUNK) % HG_SUB
    return jnp.asarray(row_s[:, None] == col_s[None, :], dtype=BF16)


def _nsa_constants(S):
    n_cmp = S // CMP_STRIDE - 1
    ncp = S // CMP_STRIDE
    n_slc = S // SLC_BLOCK
    ratio = SLC_BLOCK // CMP_STRIDE
    i = np.arange(ncp)[:, None]
    j = np.arange(LANES)[None, :]
    agg = ((i >= ratio * j - CMP_BLOCK // CMP_STRIDE + 1) & (i <= ratio * j + ratio - 1)
           & (i < n_cmp) & (j < n_slc))
    onehot = (np.arange(S)[:, None] // SLC_BLOCK) == j
    return jnp.asarray(onehot, dtype=BF16), jnp.asarray(agg.T, dtype=BF16)


def _rope_tables_t(positions, head_dim):
    rot = head_dim // ROPE_FRACTION
    inv_freq = ROPE_THETA ** (-jnp.arange(0, rot, 2, dtype=F32) / rot)
    ang = inv_freq[:, None] * positions.reshape(-1).astype(F32)[None, :]
    cos, sin = jnp.cos(ang), jnp.sin(ang)
    return jnp.concatenate([cos, cos], axis=0), jnp.concatenate([-sin, sin], axis=0), rot // 2


def _hgrn_layer(hb, w_in, g_norm, log_lb, log_1m_lb, esum):
    proj = _mm(hb, w_in, out_dtype=F32, head_major=True, tn=1024)
    return _hgrn_rec(proj, log_lb, log_1m_lb, g_norm, esum)


def _nsa_layer(hb, positions, w_in, pos_k, w1_k, w2_k, pos_v, w1_v, w2_v):
    S = hb.shape[0]
    H, G, R, DH = NSA_HEADS, NSA_KV_GROUPS, NSA_REP, NSA_HEAD_DIM
    gw = G * DH
    assert gw == 512 and (H * DH) % gw == 0
    t_kc = H * DH // gw
    o_gate = H * DH + 6 * gw
    o_z = o_gate + 3 * H
    rope_k = _rope_tables(positions, DH, 1.0)
    tq = NSA_TQ

    wide = 2 * gw
    q_t = _mm_t(hb, w_in, cols=(0, 1, H * DH // wide), out_dtype=BF16, tk=tq, tn=wide,
                rope=_rope_tables_t(positions, DH), scale=DH ** -0.5 * LOG2_E)
    kv_c = _mm(hb, w_in, cols=(t_kc, 1, 2), out_dtype=F32, rope=rope_k, rope_tiles=1,
               head_major=True)
    ksw = _mm(hb, w_in, cols=(t_kc + 2, 2, 2), out_dtype=BF16, rope=rope_k, head_major=True)
    vsw_t = _mm_t(hb, w_in, cols=(t_kc + 3, 2, 2), out_dtype=BF16, tk=tq)
    z = _mm(hb, w_in[:, o_z:], out_dtype=F32, tn=wide)
    w_gate = jnp.pad(w_in[:, o_gate:o_z], ((0, 0), (0, LANES - 3 * H)))
    gate_t = _mm_t(hb, w_gate, out_dtype=F32, tk=tq, tn=LANES)

    kc = _nsa_compress(kv_c, 0, pos_k, w1_k, w2_k, transposed=False)
    vc_t = _nsa_compress(kv_c, G, pos_v, w1_v, w2_v, transposed=True)
    onehot, agg_t = _nsa_constants(S)
    return _nsa_attn(q_t, kc, vc_t, ksw, vsw_t, onehot, agg_t, gate_t, z)


def _swa_layer(hb, positions, w_in, sinks):
    H, KV, DH = SWA_HEADS, SWA_KV_HEADS, SWA_HEAD_DIM
    o_k = H * DH
    o_v = o_k + KV * DH
    o_z = o_v + KV * DH
    tn = 512
    assert o_k % tn == 0 and o_z % tn == 0
    dup = np.concatenate([np.tile(np.arange(g * DH, (g + 1) * DH), 2) for g in range(KV)])
    rope_q = _rope_tables(positions, DH, DH ** -0.5)
    rope_k = _rope_tables(positions, DH, 1.0)
    q = _mm(hb, w_in, cols=(0, 1, o_k // (2 * tn)), out_dtype=BF16, rope=rope_q, tn=2 * tn)
    w_kv = jnp.concatenate([w_in[:, o_k:o_v][:, dup], w_in[:, o_v:o_z][:, dup]], axis=1)
    kv_dup = _mm(hb, w_kv, out_dtype=BF16, rope=rope_k, rope_tiles=1, tn=tn)
    z = _mm(hb, w_in, cols=(o_z // tn, 1, o_k // tn), out_dtype=F32, tn=tn)
    return _swa_attn(sinks, q, kv_dup, z)


def kernel(x, positions, hgrn_lb_logits, l0_w_in, l0_g_norm, l0_w_out, l0_ln_g, l0_ln_b, l1_w_in, l1_cmp_pos_k, l1_cmp_w1_k, l1_cmp_w2_k, l1_cmp_pos_v, l1_cmp_w1_v, l1_cmp_w2_v, l1_w_out, l1_ln_g, l1_ln_b, l2_w_in, l2_sinks, l2_w_out, l2_ln_g, l2_ln_b, l3_w_in, l3_g_norm, l3_w_out, l3_ln_g, l3_ln_b):
    B, S, D = x.shape
    assert B == 1 and D == D_MODEL
    lb = jnp.cumsum(jax.nn.softmax(hgrn_lb_logits.astype(F32), axis=0), axis=0)
    lb = lb - lb[0:1]
    log_lb, log_1m_lb = jnp.log(lb), jnp.log1p(-lb)
    esum = _hgrn_esum()

    h = x.reshape(S, D)
    hb = h.astype(BF16)

    a = _hgrn_layer(hb, l0_w_in, l0_g_norm, log_lb[0], log_1m_lb[0], esum)
    h, hb = _outproj_ln(a, l0_w_out.astype(BF16), h, l0_ln_g, l0_ln_b)

    a = _nsa_layer(hb, positions, l1_w_in, l1_cmp_pos_k, l1_cmp_w1_k, l1_cmp_w2_k,
                   l1_cmp_pos_v, l1_cmp_w1_v, l1_cmp_w2_v)
    h, hb = _outproj_ln(a, l1_w_out.astype(BF16), h, l1_ln_g, l1_ln_b)

    a = _swa_layer(hb, positions, l2_w_in, l2_sinks)
    h, hb = _outproj_ln(a, l2_w_out.astype(BF16), h, l2_ln_g, l2_ln_b)

    a = _hgrn_layer(hb, l3_w_in, l3_g_norm, log_lb[1], log_1m_lb[1], esum)
    h, hb = _outproj_ln(a, l3_w_out.astype(BF16), h, l3_ln_g, l3_ln_b)
    return h.reshape(B, S, D)
```

```python
import functools

import numpy as np
import jax
import jax.numpy as jnp
from jax import lax
from jax.experimental import pallas as pl
from jax.experimental.pallas import tpu as pltpu

F32 = jnp.float32
BF16 = jnp.bfloat16

D_MODEL = 2048
DEPTH = 4
N_MIXERS = 3
DEEPNORM_ALPHA = (2 * DEPTH) ** 0.25
LN_EPS = 1e-5
RMS_EPS = 1e-6
ROPE_THETA = 500000.0
ROPE_FRACTION = 4

HG_HEAD_DIM = 128
HG_HEADS = D_MODEL // HG_HEAD_DIM
HG_CHUNK = 64
HG_SUB = 8
HG_TIME_BLOCK = 1024
HG_PIECE = 512
LOG2_E = 1.4426950408889634

NSA_HEAD_DIM = 128
NSA_HEADS = D_MODEL // NSA_HEAD_DIM
NSA_KV_GROUPS = 4
NSA_REP = NSA_HEADS // NSA_KV_GROUPS
CMP_BLOCK = 32
CMP_STRIDE = 16
CMP_HIDDEN = 256
SLC_BLOCK = 64
SLC_TOPK = 16
NSA_WINDOW = 512
NSA_TQ = 256
NSA_ONES_ROWS = 16
NSA_BLOCKS_PER_STEP = 2
FORCE_BONUS = 1.0e4
MASK_NEG = -1.0e30

SWA_HEAD_DIM = 64
SWA_HEADS = D_MODEL // SWA_HEAD_DIM
SWA_KV_HEADS = 4
SWA_REP = SWA_HEADS // SWA_KV_HEADS
SWA_WINDOW = 128

LANES = 128
VMEM_LIMIT_BYTES = 48 * 1024 * 1024

_NT = (((1,), (1,)), ((), ()))
_TN = (((0,), (0,)), ((), ()))


def _params(*sem):
    return pltpu.CompilerParams(dimension_semantics=sem, vmem_limit_bytes=VMEM_LIMIT_BYTES)


def _silu(x):
    return x * (1.0 / (1.0 + jnp.exp(-x)))


def _mm_kernel(*refs, rope_half, rope_period, rope_tiles, head_major, n_chunks):
    if rope_half:
        x_ref, w_ref, c_ref, s_ref, o_ref, wb_ref = refs
    else:
        x_ref, w_ref, o_ref, wb_ref = refs

    @pl.when(pl.program_id(1) == 0)
    def _():
        wb_ref[...] = w_ref[...].astype(BF16)

    acc = jnp.dot(x_ref[...], wb_ref[...], preferred_element_type=F32)

    def write(with_rope):
        if with_rope:
            cos = c_ref[...]
            sin = s_ref[...]
            lane = lax.broadcasted_iota(jnp.int32, cos.shape, 1)
            first_half = (lane & (rope_period - 1)) < rope_half
        for j in range(n_chunks):
            a = acc[:, j * LANES:(j + 1) * LANES]
            if with_rope:
                up = pltpu.roll(a, LANES - rope_half, 1)
                dn = pltpu.roll(a, rope_half, 1)
                a = a * cos + jnp.where(first_half, up, dn) * sin
            if head_major:
                o_ref[j] = a.astype(o_ref.dtype)
            else:
                o_ref[:, j * LANES:(j + 1) * LANES] = a.astype(o_ref.dtype)

    if rope_half and rope_tiles is not None:
        pl.when(pl.program_id(0) < rope_tiles)(functools.partial(write, True))
        pl.when(pl.program_id(0) >= rope_tiles)(functools.partial(write, False))
    else:
        write(bool(rope_half))


def _col_tiles(w, cols, tn):
    return (0, 1, w.shape[1] // tn) if cols is None else cols


def _mm(x, w, *, out_dtype, cols=None, rope=None, rope_tiles=None, head_major=False, tn=512):
    S, K = x.shape
    tm = min(1024, S)
    first, stride, n_tiles = _col_tiles(w, cols, tn)
    N = n_tiles * tn
    n_chunks = tn // LANES
    in_specs = [pl.BlockSpec((tm, K), lambda j, i: (i, 0)),
                pl.BlockSpec((K, tn), lambda j, i: (0, first + stride * j))]
    args = [x, w]
    half = period = 0
    if rope is not None:
        cos_t, sin_t, half, period = rope
        in_specs += [pl.BlockSpec((tm, LANES), lambda j, i: (i, 0))] * 2
        args += [cos_t, sin_t]
    if head_major:
        out_shape = jax.ShapeDtypeStruct((N // LANES, S, LANES), out_dtype)
        out_spec = pl.BlockSpec((n_chunks, tm, LANES), lambda j, i: (j, i, 0))
    else:
        out_shape = jax.ShapeDtypeStruct((S, N), out_dtype)
        out_spec = pl.BlockSpec((tm, tn), lambda j, i: (i, j))
    return pl.pallas_call(
        functools.partial(_mm_kernel, rope_half=half, rope_period=period, rope_tiles=rope_tiles,
                          head_major=head_major, n_chunks=n_chunks),
        grid=(n_tiles, S // tm),
        in_specs=in_specs,
        out_specs=out_spec,
        out_shape=out_shape,
        scratch_shapes=[pltpu.VMEM((K, tn), BF16)],
        compiler_params=_params("arbitrary", "arbitrary"),
        name="in_proj",
    )(*args)


def _mm_t_kernel(*refs, rope_half, scale, n_heads, n_tiles, tk):
    if rope_half:
        w_ref, x_ref, c_ref, s_ref, o_ref, wt_ref = refs
    else:
        w_ref, x_ref, o_ref, wt_ref = refs

    @pl.when(pl.program_id(1) == 0)
    def _():
        wt_ref[...] = w_ref[...].T.astype(BF16)

    acc = lax.dot_general(wt_ref[...], x_ref[...], _NT, preferred_element_type=F32)
    for c in range(n_heads):
        a = acc[c * LANES:(c + 1) * LANES]
        if scale != 1.0:
            a = a * scale
        if rope_half:
            rot = 2 * rope_half
            swapped = jnp.concatenate([a[rope_half:rot], a[:rope_half]], axis=0)
            a = jnp.concatenate([a[:rot] * c_ref[...] + swapped * s_ref[...], a[rot:]], axis=0)
        for b in range(n_tiles):
            o_ref[c, b] = a[:, b * tk:(b + 1) * tk].astype(o_ref.dtype)


def _mm_t(x, w, *, out_dtype, tk, cols=None, rope=None, scale=1.0, tn=512):
    S, K = x.shape
    tm = min(1024, S)
    first, stride, n_tiles = _col_tiles(w, cols, tn)
    N = n_tiles * tn
    in_specs = [pl.BlockSpec((K, tn), lambda j, i: (0, first + stride * j)),
                pl.BlockSpec((tm, K), lambda j, i: (i, 0))]
    args = [w, x]
    half = 0
    if rope is not None:
        cos_t, sin_t, half = rope
        in_specs += [pl.BlockSpec((2 * half, tm), lambda j, i: (0, i))] * 2
        args += [cos_t, sin_t]
    return pl.pallas_call(
        functools.partial(_mm_t_kernel, rope_half=half, scale=scale, n_heads=tn // LANES,
                          n_tiles=tm // tk, tk=tk),
        grid=(n_tiles, S // tm),
        in_specs=in_specs,
        out_specs=pl.BlockSpec((tn // LANES, tm // tk, LANES, tk), lambda j, i: (j, i, 0, 0)),
        out_shape=jax.ShapeDtypeStruct((N // LANES, S // tk, LANES, tk), out_dtype),
        scratch_shapes=[pltpu.VMEM((tn, K), BF16)],
        compiler_params=_params("arbitrary", "arbitrary"),
        name="in_proj_t",
    )(*args)


def _outproj_ln_kernel(a_ref, w_ref, h_ref, g_ref, b_ref, o_ref, ob_ref):
    y = jnp.dot(a_ref[...], w_ref[...], preferred_element_type=F32)
    u = DEEPNORM_ALPHA * h_ref[...] + y
    mu = jnp.mean(u, axis=-1, keepdims=True)
    xc = u - mu
    var = jnp.mean(xc * xc, axis=-1, keepdims=True)
    out = xc * lax.rsqrt(var + LN_EPS) * g_ref[...] + b_ref[...]
    o_ref[...] = out
    ob_ref[...] = out.astype(BF16)


def _outproj_ln(a, w, h, g, b):
    S, D = h.shape
    tm = min(512, S)
    row = pl.BlockSpec((tm, D), lambda i: (i, 0))
    vec = pl.BlockSpec((1, D), lambda i: (0, 0))
    return pl.pallas_call(
        _outproj_ln_kernel,
        grid=(S // tm,),
        in_specs=[row, pl.BlockSpec((D, D), lambda i: (0, 0)), row, vec, vec],
        out_specs=[row, row],
        out_shape=[jax.ShapeDtypeStruct((S, D), F32), jax.ShapeDtypeStruct((S, D), BF16)],
        compiler_params=_params("parallel"),
        name="outproj_ln",
    )(a, w, h, g.reshape(1, D), b.reshape(1, D))


def _hgrn_kernel(x_ref, wq_ref, wf_ref, wv_ref, wz_ref, llb_ref, l1m_ref, gn_ref, esum_ref,
                 o_ref, st_ref, ycat_ref, u_ref, wb_ref, *, n_pieces, piece):
    C, SUB, DK = HG_CHUNK, HG_SUB, HG_HEAD_DIM
    n_lvl = (C // (2 * SUB)).bit_length()
    n_pc = piece // C

    @pl.when(pl.program_id(1) == 0)
    def _():
        st_ref[...] = jnp.zeros_like(st_ref)
        for i, w_ref in enumerate((wq_ref, wf_ref, wv_ref, wz_ref)):
            wb_ref[:, i * DK:(i + 1) * DK] = w_ref[...].astype(BF16)

    r = lax.broadcasted_iota(jnp.int32, (C, C), 0)
    c = lax.broadcasted_iota(jnp.int32, (C, C), 1)
    causal = c <= r
    halves = [SUB << lvl for lvl in range(n_lvl)]
    cum_rows = [causal, jnp.ones((C, C), jnp.bool_)]
    for h in halves:
        same_h = (r // h) == (c // h)
        cum_rows += [causal & same_h, same_h]
    cum_one = jnp.concatenate([m.astype(F32) for m in cum_rows], axis=0).astype(BF16)
    cum_mat = jnp.concatenate([cum_one] * 3, axis=1)
    diag_mask = causal & ((r // SUB) == (c // SUB))
    log_lb = llb_ref[...]
    log_1m_lb = l1m_ref[...]
    esum = esum_ref[...]
    chunks = [slice(ci * C, (ci + 1) * C) for ci in range(n_pc)]

    def rows_to_lanes(x):
        return jnp.concatenate([x[sl] for sl in chunks], axis=1)

    def lanes_to_rows(x):
        return jnp.concatenate([x[:, ci * DK:(ci + 1) * DK] for ci in range(n_pc)], axis=0)

    def placed(x, lo):
        parts = [jnp.zeros((lo, DK), F32), x, jnp.zeros((C - lo - x.shape[0], DK), F32)]
        return jnp.concatenate([p for p in parts if p.shape[0]], axis=0)

    def front(pi):
        rows = slice(pi * piece, (pi + 1) * piece)
        proj = jnp.dot(x_ref[rows, :], wb_ref[...], preferred_element_type=F32)
        q, fl, v, z = (proj[:, i * DK:(i + 1) * DK] for i in range(4))
        vb = v.astype(BF16)
        log_sig = jnp.minimum(fl, 0.0) - jnp.log(1.0 + jnp.exp(-jnp.abs(fl)))
        t = log_1m_lb + log_sig
        log_f = jnp.maximum(log_lb, t) + jnp.log(1.0 + jnp.exp(-jnp.abs(log_lb - t)))
        log2_k = (t - fl) * LOG2_E
        k = jnp.exp2(log2_k)
        lf = rows_to_lanes(log_f)
        lf_hi = lf.astype(BF16)
        lf_r1 = lf - lf_hi.astype(F32)
        lf_mid = lf_r1.astype(BF16)
        lf_lo = (lf_r1 - lf_mid.astype(F32)).astype(BF16)
        cums = jnp.dot(cum_mat, jnp.concatenate([lf_hi, lf_mid, lf_lo], axis=0),
                       preferred_element_type=F32) * LOG2_E
        cum = [lanes_to_rows(cums[i * C:(i + 1) * C]) for i in range(2 + 2 * n_lvl)]
        b_full, b_tot = cum[0], cum[1]
        qe = (q * jnp.exp2(b_full)).astype(BF16)
        kd = (k * jnp.exp2(b_tot - b_full)).astype(BF16)
        q_lvl = [q * jnp.exp2(cum[2 + 2 * l]) for l in range(n_lvl)]
        k_lvl = [k * jnp.exp2(cum[3 + 2 * l] - cum[2 + 2 * l]) for l in range(n_lvl)]
        b_loc = cum[2]
        c_row = b_loc - log2_k

        a_offs = []
        for ci, ch in enumerate(chunks):
            for i in range(C // (2 * SUB)):
                base = ci * C + i * 2 * SUB
                sa, sb = slice(base, base + SUB), slice(base + SUB, base + 2 * SUB)
                for s in range(SUB):
                    ya = q[sa] * jnp.exp2(jnp.minimum(
                        b_loc[sa] - c_row[base + s:base + s + 1, :], log2_k[base + s:base + s + 1, :]))
                    yb = q[sb] * jnp.exp2(jnp.minimum(
                        b_loc[sb] - c_row[base + SUB + s:base + SUB + s + 1, :],
                        log2_k[base + SUB + s:base + SUB + s + 1, :]))
                    ycat_ref[pi * piece + base:pi * piece + base + 2 * SUB, s * DK:(s + 1) * DK] = (
                        jnp.concatenate([ya, yb], axis=0).astype(BF16))
            lhs, rhs = [], []
            for l, h in enumerate(halves):
                q_c, k_c = q_lvl[l][ch], k_lvl[l][ch]
                for lo in range(0, C, 2 * h):
                    lhs.append(placed(q_c[lo + h:lo + 2 * h], lo + h))
                    rhs.append(placed(k_c[lo:lo + h], lo))
            a_offs.append(lax.dot_general(jnp.concatenate(lhs, axis=1).astype(BF16),
                                          jnp.concatenate(rhs, axis=1).astype(BF16),
                                          _NT, preferred_element_type=F32))
            u_ref[pi * n_pc + ci] = lax.dot_general(vb[ch], kd[ch], _TN,
                                                    preferred_element_type=F32)
        a_diag = jnp.dot(ycat_ref[rows, :], esum, preferred_element_type=F32)
        per_chunk = []
        for ci, ch in enumerate(chunks):
            a = jnp.where(diag_mask, a_diag[ch], 0.0) + a_offs[ci]
            o_intra = jnp.dot(a.astype(BF16), vb[ch], preferred_element_type=F32)
            per_chunk.append((qe[ch], o_intra, b_tot[ci * C:ci * C + 8]))
        return per_chunk, z

    fronts = [front(pi) for pi in range(n_pieces)]

    st = st_ref[...]
    outs = []
    for pi, (per_chunk, _) in enumerate(fronts):
        for ci, (qe_c, o_intra, b_tot8) in enumerate(per_chunk):
            outs.append(o_intra + lax.dot_general(qe_c, st.astype(BF16), _NT,
                                                  preferred_element_type=F32))
            st = st * jnp.exp2(jnp.tile(b_tot8, (DK // 8, 1))) + u_ref[pi * n_pc + ci]
    st_ref[...] = st

    o = jnp.concatenate(outs, axis=0)
    z = jnp.concatenate([z_p for _, z_p in fronts], axis=0)
    ms = jnp.mean(o * o, axis=-1, keepdims=True)
    out = o * lax.rsqrt(ms + RMS_EPS) * gn_ref[...] * _silu(z)
    o_ref[...] = out.astype(o_ref.dtype)


def _hgrn_mixer(hb, w_in, log_lb, log_1m_lb, g_norm, esum):
    S, K = hb.shape
    D, DK, H = D_MODEL, HG_HEAD_DIM, HG_HEADS
    T = min(HG_TIME_BLOCK, S)
    piece = min(HG_PIECE, T)

    def w_col(off):
        return pl.BlockSpec((K, DK), lambda h, t, off=off: (0, off + h))

    vec = pl.BlockSpec((1, DK), lambda h, t: (0, h))
    return pl.pallas_call(
        functools.partial(_hgrn_kernel, n_pieces=T // piece, piece=piece),
        grid=(H, S // T),
        in_specs=[pl.BlockSpec((T, K), lambda h, t: (t, 0)),
                  w_col(0), w_col(H), w_col(2 * H), w_col(3 * H), vec, vec, vec,
                  pl.BlockSpec(esum.shape, lambda h, t: (0, 0))],
        out_specs=pl.BlockSpec((T, DK), lambda h, t: (t, h)),
        out_shape=jax.ShapeDtypeStruct((S, D), BF16),
        scratch_shapes=[pltpu.VMEM((DK, DK), F32),
                        pltpu.VMEM((T, HG_SUB * DK), BF16),
                        pltpu.VMEM((T // HG_CHUNK, DK, DK), F32),
                        pltpu.VMEM((K, 4 * DK), BF16)],
        compiler_params=_params("parallel", "arbitrary"),
        name="hgrn_mixer",
    )(hb, w_in, w_in, w_in, w_in, log_lb.reshape(1, D), log_1m_lb.reshape(1, D),
      g_norm.reshape(1, D), esum)


def _nsa_compress_kernel(x_ref, pos_ref, w1_ref, w2_ref, o_ref, *, transposed):
    DH = NSA_HEAD_DIM
    n = x_ref.shape[0] // CMP_STRIDE
    h1 = jnp.zeros((n, CMP_HIDDEN), F32)
    h2 = jnp.zeros((n, CMP_HIDDEN), F32)
    for l in range(CMP_STRIDE):
        x_l = x_ref[pl.ds(l, n, stride=CMP_STRIDE), :]
        l2 = CMP_STRIDE + l
        h1 = h1 + jnp.dot((x_l + pos_ref[l:l + 1, :]).astype(BF16),
                          w1_ref[l * DH:(l + 1) * DH, :].astype(BF16), preferred_element_type=F32)
        h2 = h2 + jnp.dot((x_l + pos_ref[l2:l2 + 1, :]).astype(BF16),
                          w1_ref[l2 * DH:(l2 + 1) * DH, :].astype(BF16), preferred_element_type=F32)
    hid = h1 + pltpu.roll(h2, n - 1, 0)
    act = _silu(hid).astype(BF16)
    if transposed:
        o_ref[...] = lax.dot_general(w2_ref[...].T.astype(BF16), act, _NT,
                                     preferred_element_type=F32).astype(o_ref.dtype)
    else:
        o_ref[...] = jnp.dot(act, w2_ref[...].astype(BF16),
                             preferred_element_type=F32).astype(o_ref.dtype)


def _nsa_compress(x, first, pos, w1, w2, *, transposed):
    G = NSA_KV_GROUPS
    _, S, DH = x.shape
    n = S // CMP_STRIDE
    whole = lambda a: pl.BlockSpec(a.shape, lambda g: (0,) * a.ndim)
    out_block = (None, DH, n) if transposed else (None, n, DH)
    return pl.pallas_call(
        functools.partial(_nsa_compress_kernel, transposed=transposed),
        grid=(G,),
        in_specs=[pl.BlockSpec((None, S, DH), lambda g: (first + g, 0, 0)),
                  whole(pos), whole(w1), whole(w2)],
        out_specs=pl.BlockSpec(out_block, lambda g: (g, 0, 0)),
        out_shape=jax.ShapeDtypeStruct((G,) + out_block[1:], BF16),
        compiler_params=_params("parallel"),
        name="nsa_compress",
    )(x, pos, w1, w2)


def _nsa_kernel(q_ref, kc_ref, vct_ref, ks_ref, vst_ref, kw_ref, vwt_ref, oh_ref, aggt_ref,
                gate_ref, z_ref, o_ref, m_ref, acc_ref, s_ref, *, tq, n_sub):
    R, DH = NSA_REP, NSA_HEAD_DIM
    ncp = kc_ref.shape[1]

    def per_head(x):
        return jnp.concatenate([x] * R, axis=1)

    def rows8(x, n):
        return jnp.tile(x, (n // 8, 1))

    def col_max8(x):
        mx = x[0:8]
        for i in range(1, x.shape[0] // 8):
            mx = jnp.maximum(mx, x[8 * i:8 * i + 8])
        for shift in (4, 2, 1):
            mx = jnp.maximum(mx, pltpu.roll(mx, shift, 0))
        return mx

    def key_rows(ref, kt):
        return ref[0, pl.ds(pl.multiple_of(kt * tq, tq), tq), :]

    c_k = lax.broadcasted_iota(jnp.int32, (tq, tq), 0)
    i_q = lax.broadcasted_iota(jnp.int32, (tq, tq), 1)
    causal = per_head(c_k <= i_q)
    aggt = aggt_ref[...]

    def front(sub):
        qb = pl.program_id(1) * n_sub + sub
        start = qb * tq
        q = jnp.concatenate([q_ref[r, sub] for r in range(R)], axis=1)

        n_c = lax.broadcasted_iota(jnp.int32, (ncp, tq), 0)
        t_c = start + lax.broadcasted_iota(jnp.int32, (ncp, tq), 1)
        cmask = per_head((n_c * CMP_STRIDE + CMP_BLOCK - 1) <= t_c)
        s = jnp.where(cmask, jnp.dot(kc_ref[0], q, preferred_element_type=F32), -jnp.inf)
        m = jnp.max(s, axis=0, keepdims=True)
        m = jnp.where(m == -jnp.inf, 0.0, m)
        e = jnp.exp2(s - m)
        p = e * (1.0 / jnp.maximum(jnp.sum(e, axis=0, keepdims=True), 1e-30))
        o_cmp = jnp.dot(vct_ref[0], p.astype(BF16), preferred_element_type=F32)
        p_sum = p[:, :tq]
        for r in range(1, R):
            p_sum = p_sum + p[:, r * tq:(r + 1) * tq]

        no_old = jnp.where(qb >= 2, 0, NSA_WINDOW)
        no_mid = jnp.where(qb >= 1, 0, NSA_WINDOW)
        old_ok = per_head((2 * tq + i_q - c_k + no_old) < NSA_WINDOW)
        mid_ok = per_head((tq + i_q - c_k + no_mid) < NSA_WINDOW)
        kt_old, kt_mid = jnp.maximum(qb - 2, 0), jnp.maximum(qb - 1, 0)
        k_win = jnp.concatenate([key_rows(kw_ref, kt_old), key_rows(kw_ref, kt_mid),
                                 key_rows(kw_ref, qb)], axis=0)
        s_w = jnp.where(jnp.concatenate([old_ok, mid_ok, causal], axis=0),
                        jnp.dot(k_win, q, preferred_element_type=F32), MASK_NEG)
        p_w = jnp.exp2(s_w - rows8(col_max8(s_w), 3 * tq)).astype(BF16)
        v_win = jnp.concatenate([vwt_ref[0, kt_old], vwt_ref[0, kt_mid], vwt_ref[0, qb]], axis=1)
        acc_w = jnp.dot(jnp.concatenate([v_win, jnp.ones((NSA_ONES_ROWS, 3 * tq), BF16)], axis=0),
                        p_w, preferred_element_type=F32)
        o_win = acc_w[:DH] * rows8(1.0 / acc_w[DH:DH + 8], DH)

        p_hi = p_sum.astype(BF16)
        p_lo = (p_sum - p_hi.astype(F32)).astype(BF16)
        imp = (jnp.dot(aggt, p_hi, preferred_element_type=F32)
               + jnp.dot(aggt, p_lo, preferred_element_type=F32))
        j_s = lax.broadcasted_iota(jnp.int32, (LANES, tq), 0)
        t_s = start + lax.broadcasted_iota(jnp.int32, (LANES, tq), 1)
        cur = lax.shift_right_logical(t_s, 6)
        forced = (j_s == 0) | (j_s == cur) | (j_s == cur - 1)
        allowed = j_s * SLC_BLOCK <= t_s
        sel = jnp.where(forced & allowed, 1.0, 0.0)
        val = jnp.where(allowed & ~forced, imp, -jnp.inf)
        row_f = j_s.astype(F32)
        for _ in range(SLC_TOPK - 3):
            mx = jnp.max(val, axis=0, keepdims=True)
            idx = jnp.min(jnp.where(val == mx, row_f, float(LANES)), axis=0, keepdims=True)
            pick = row_f == idx
            sel = jnp.where(pick, 1.0, sel)
            val = jnp.where(pick, -jnp.inf, val)
        m_neg = jnp.where(sel > 0.0, 0.0, MASK_NEG).astype(BF16)
        q_aug = jnp.concatenate([q, per_head(m_neg)], axis=0)
        return qb, q_aug, o_cmp, o_win

    ones_rows = jnp.ones((NSA_ONES_ROWS, tq), BF16)

    def flash_init():
        m_ref[...] = jnp.full(m_ref.shape, -jnp.inf, F32)
        acc_ref[...] = jnp.zeros_like(acc_ref)

    def flash_update(s, vt_tile):
        m_prev = m_ref[...]
        m_next = jnp.maximum(m_prev, col_max8(s))
        alpha = jnp.exp2(m_prev - m_next)
        p = jnp.exp2(s - rows8(m_next, tq)).astype(BF16)
        v_aug = jnp.concatenate([vt_tile, ones_rows], axis=0)
        acc_ref[...] = (acc_ref[...] * rows8(alpha, DH + NSA_ONES_ROWS)
                        + jnp.dot(v_aug, p, preferred_element_type=F32))
        m_ref[...] = m_next

    def flash_result():
        acc = acc_ref[...]
        return acc[:DH] * rows8(1.0 / acc[DH:DH + 8], DH)

    def back(sub, qb, q_aug, o_cmp, o_win):
        def slc_scores(kt):
            oh = oh_ref[pl.ds(pl.multiple_of(kt * tq, tq), tq), :]
            k_aug = jnp.concatenate([key_rows(ks_ref, kt), oh], axis=1)
            return jnp.dot(k_aug, q_aug, preferred_element_type=F32)

        flash_init()
        s_ref[...] = slc_scores(0)

        def slc_step(kt):
            s_cur = s_ref[...]
            s_ref[...] = slc_scores(kt + 1)
            flash_update(s_cur, vst_ref[0, kt])

        def slc_quad(i, carry):
            for u in range(4):
                slc_step(4 * i + u)
            return carry

        lax.fori_loop(0, lax.shift_right_logical(qb, 2), slc_quad, 0)
        done = qb & ~3

        @pl.when((qb & 2) != 0)
        def _():
            slc_step(done)
            slc_step(done + 1)

        @pl.when((qb & 1) != 0)
        def _():
            slc_step(qb - 1)

        flash_update(jnp.where(causal, s_ref[...], MASK_NEG), vst_ref[0, qb])
        o_slc = flash_result()

        head0 = pl.program_id(0) * R

        def gate(branch, r):
            logit = gate_ref[0, sub, pl.ds(branch * NSA_HEADS + head0 + r, 1), :]
            return jnp.broadcast_to(1.0 / (1.0 + jnp.exp(-logit)), (DH, tq))

        rows = slice(sub * tq, (sub + 1) * tq)
        for r in range(R):
            sl = slice(r * tq, (r + 1) * tq)
            g = [gate(b, r) for b in range(3)]
            o_t = g[0] * o_cmp[:, sl] + g[1] * o_slc[:, sl] + g[2] * o_win[:, sl]
            z = z_ref[rows, r * DH:(r + 1) * DH]
            o_ref[rows, r * DH:(r + 1) * DH] = (o_t.T * _silu(z)).astype(o_ref.dtype)

    fronts = [front(sub) for sub in range(n_sub)]
    for sub in range(n_sub):
        back(sub, *fronts[sub])


def _nsa_attn(q_t, kc, vc_t, ksw, vsw_t, onehot, agg_t, gate_t, z):
    G, R, DH = NSA_KV_GROUPS, NSA_REP, NSA_HEAD_DIM
    S = ksw.shape[1]
    tq, n_sub = NSA_TQ, NSA_BLOCKS_PER_STEP
    assert S % (n_sub * tq) == 0 and NSA_WINDOW <= 2 * tq and S // SLC_BLOCK <= LANES
    ncp = kc.shape[1]
    n_t = S // tq
    rows = lambda off: pl.BlockSpec((1, S, DH), lambda g, i, off=off: (off + g, 0, 0))
    tiles_t = lambda off: pl.BlockSpec((1, n_t, DH, tq), lambda g, i, off=off: (off + g, 0, 0, 0))
    return pl.pallas_call(
        functools.partial(_nsa_kernel, tq=tq, n_sub=n_sub),
        grid=(G, n_t // n_sub),
        in_specs=[pl.BlockSpec((R, n_sub, DH, tq), lambda g, i: (g, i, 0, 0)),
                  pl.BlockSpec((1, ncp, DH), lambda g, i: (g, 0, 0)),
                  pl.BlockSpec((1, DH, ncp), lambda g, i: (g, 0, 0)),
                  rows(0), tiles_t(0), rows(G), tiles_t(G),
                  pl.BlockSpec((S, LANES), lambda g, i: (0, 0)),
                  pl.BlockSpec((LANES, ncp), lambda g, i: (0, 0)),
                  pl.BlockSpec((1, n_sub, LANES, tq), lambda g, i: (0, i, 0, 0)),
                  pl.BlockSpec((n_sub * tq, R * DH), lambda g, i: (i, g))],
        out_specs=pl.BlockSpec((n_sub * tq, R * DH), lambda g, i: (i, g)),
        out_shape=jax.ShapeDtypeStruct((S, D_MODEL), BF16),
        scratch_shapes=[pltpu.VMEM((8, R * tq), F32),
                        pltpu.VMEM((DH + NSA_ONES_ROWS, R * tq), F32),
                        pltpu.VMEM((tq, R * tq), F32)],
        compiler_params=_params("parallel", "arbitrary"),
        name="nsa_attn",
    )(q_t, kc, vc_t, ksw, vsw_t, ksw, vsw_t, onehot, agg_t, gate_t, z)


def _swa_kernel(sink_ref, q_ref, kp_ref, kc_ref, vp_ref, vc_ref, z_ref, o_ref):
    W, R = SWA_WINDOW, SWA_REP
    n = pl.program_id(0)
    i_q = lax.broadcasted_iota(jnp.int32, (W, 2 * W), 0)
    c_k = lax.broadcasted_iota(jnp.int32, (W, 2 * W), 1)
    diff = i_q - (c_k - W)
    mask = (diff >= 0) & (diff < W) & ((n > 0) | (c_k >= W))
    lane = lax.broadcasted_iota(jnp.int32, (W, LANES), 1)
    low = lane < SWA_HEAD_DIM
    for g in range(SWA_KV_HEADS):
        gl = slice(g * LANES, (g + 1) * LANES)
        kk = jnp.concatenate([kp_ref[:, gl], kc_ref[:, gl]], axis=0)
        vv = jnp.concatenate([vp_ref[:, gl], vc_ref[:, gl]], axis=0)
        qs = []
        for r in range(R):
            h = g * R + r
            q2 = q_ref[:, (h // 2) * LANES:(h // 2 + 1) * LANES]
            qs.append(jnp.where(low if h % 2 == 0 else ~low, q2, jnp.zeros_like(q2)))
        s_all = lax.dot_general(jnp.concatenate(qs, axis=0), kk, _NT,
                                preferred_element_type=F32)
        ps, inv_l = [], []
        for r in range(R):
            sink = sink_ref[g * R + r]
            s = jnp.where(mask, s_all[r * W:(r + 1) * W], -jnp.inf)
            m = jnp.maximum(jnp.max(s, axis=1, keepdims=True), sink)
            e = jnp.exp(s - m)
            inv_l.append(1.0 / (jnp.sum(e, axis=1, keepdims=True) + jnp.exp(sink - m)))
            ps.append(e.astype(BF16))
        o_all = jnp.dot(jnp.concatenate(ps, axis=0), vv, preferred_element_type=F32)
        for pr in range(R // 2):
            o_even = o_all[(2 * pr) * W:(2 * pr + 1) * W] * inv_l[2 * pr]
            o_odd = o_all[(2 * pr + 1) * W:(2 * pr + 2) * W] * inv_l[2 * pr + 1]
            col = slice((g * R // 2 + pr) * LANES, (g * R // 2 + pr + 1) * LANES)
            o_ref[:, col] = (jnp.where(low, o_even, o_odd) * _silu(z_ref[:, col])).astype(o_ref.dtype)


def _swa_attn(sinks, q, kv_dup, z):
    S, D = q.shape
    W = SWA_WINDOW
    wide = SWA_KV_HEADS * LANES
    cur = lambda i: (i, 0)
    k_prev, k_cur = (lambda i: (jnp.maximum(i - 1, 0), 0)), cur
    v_prev, v_cur = (lambda i: (jnp.maximum(i - 1, 0), 1)), (lambda i: (i, 1))
    return pl.pallas_call(
        _swa_kernel,
        grid=(S // W,),
        in_specs=[pl.BlockSpec(memory_space=pltpu.SMEM),
                  pl.BlockSpec((W, D), cur),
                  pl.BlockSpec((W, wide), k_prev), pl.BlockSpec((W, wide), k_cur),
                  pl.BlockSpec((W, wide), v_prev), pl.BlockSpec((W, wide), v_cur),
                  pl.BlockSpec((W, D), cur)],
        out_specs=pl.BlockSpec((W, D), cur),
        out_shape=jax.ShapeDtypeStruct((S, D), BF16),
        compiler_params=_params("parallel"),
        name="swa_attn",
    )(sinks, q, kv_dup, kv_dup, kv_dup, kv_dup, z)


def _rope_tables(positions, head_dim, scale):
    rot = head_dim // ROPE_FRACTION
    half = rot // 2
    inv_freq = ROPE_THETA ** (-jnp.arange(0, rot, 2, dtype=F32) / rot)
    ang = positions.reshape(-1).astype(F32)[:, None] * inv_freq
    cos, sin = jnp.cos(ang), jnp.sin(ang)
    S = cos.shape[0]
    rest = head_dim - rot
    cos_h = jnp.concatenate([cos, cos, jnp.ones((S, rest), F32)], axis=1)
    sin_h = jnp.concatenate([-sin, sin, jnp.zeros((S, rest), F32)], axis=1)
    reps = LANES // head_dim
    return (jnp.tile(cos_h, (1, reps)) * scale, jnp.tile(sin_h, (1, reps)) * scale, half, head_dim)


def _hgrn_esum():
    row_s = np.arange(HG_SUB * HG_HEAD_DIM) // HG_HEAD_DIM
    col_s = np.arange(HG_CHUNK) % HG_SUB
    return jnp.asarray(row_s[:, None] == col_s[None, :], dtype=BF16)


def _nsa_constants(S):
    n_cmp = S // CMP_STRIDE - 1
    ncp = S // CMP_STRIDE
    n_slc = S // SLC_BLOCK
    ratio = SLC_BLOCK // CMP_STRIDE
    i = np.arange(ncp)[:, None]
    j = np.arange(LANES)[None, :]
    agg = ((i >= ratio * j - CMP_BLOCK // CMP_STRIDE + 1) & (i <= ratio * j + ratio - 1)
           & (i < n_cmp) & (j < n_slc))
    onehot = (np.arange(S)[:, None] // SLC_BLOCK) == j
    return jnp.asarray(onehot, dtype=BF16), jnp.asarray(agg.T, dtype=BF16)


def _rope_tables_t(positions, head_dim):
    rot = head_dim // ROPE_FRACTION
    inv_freq = ROPE_THETA ** (-jnp.arange(0, rot, 2, dtype=F32) / rot)
    ang = inv_freq[:, None] * positions.reshape(-1).astype(F32)[None, :]
    cos, sin = jnp.cos(ang), jnp.sin(ang)
    return jnp.concatenate([cos, cos], axis=0), jnp.concatenate([-sin, sin], axis=0), rot // 2


def _hgrn_layer(hb, w_in, g_norm, log_lb, log_1m_lb, esum):
    return _hgrn_mixer(hb, w_in, log_lb, log_1m_lb, g_norm, esum)


def _nsa_layer(hb, positions, w_in, pos_k, w1_k, w2_k, pos_v, w1_v, w2_v):
    S = hb.shape[0]
    H, G, R, DH = NSA_HEADS, NSA_KV_GROUPS, NSA_REP, NSA_HEAD_DIM
    gw = G * DH
    assert gw == 512 and (H * DH) % gw == 0
    t_kc = H * DH // gw
    o_gate = H * DH + 6 * gw
    o_z = o_gate + 3 * H
    rope_k = _rope_tables(positions, DH, 1.0)
    tq = NSA_TQ

    wide = 2 * gw
    q_t = _mm_t(hb, w_in, cols=(0, 1, H * DH // wide), out_dtype=BF16, tk=tq, tn=wide,
                rope=_rope_tables_t(positions, DH), scale=DH ** -0.5 * LOG2_E)
    kv_c = _mm(hb, w_in, cols=(t_kc, 1, 2), out_dtype=F32, rope=rope_k, rope_tiles=1,
               head_major=True)
    ksw = _mm(hb, w_in, cols=(t_kc + 2, 2, 2), out_dtype=BF16, rope=rope_k, head_major=True)
    vsw_t = _mm_t(hb, w_in, cols=(t_kc + 3, 2, 2), out_dtype=BF16, tk=tq)
    z = _mm(hb, w_in[:, o_z:], out_dtype=F32, tn=wide)
    w_gate = jnp.pad(w_in[:, o_gate:o_z], ((0, 0), (0, LANES - 3 * H)))
    gate_t = _mm_t(hb, w_gate, out_dtype=F32, tk=tq, tn=LANES)

    kc = _nsa_compress(kv_c, 0, pos_k, w1_k, w2_k, transposed=False)
    vc_t = _nsa_compress(kv_c, G, pos_v, w1_v, w2_v, transposed=True)
    onehot, agg_t = _nsa_constants(S)
    return _nsa_attn(q_t, kc, vc_t, ksw, vsw_t, onehot, agg_t, gate_t, z)


def _swa_layer(hb, positions, w_in, sinks):
    H, KV, DH = SWA_HEADS, SWA_KV_HEADS, SWA_HEAD_DIM
    o_k = H * DH
    o_v = o_k + KV * DH
    o_z = o_v + KV * DH
    tn = 512
    assert o_k % tn == 0 and o_z % tn == 0
    dup = np.concatenate([np.tile(np.arange(g * DH, (g + 1) * DH), 2) for g in range(KV)])
    rope_q = _rope_tables(positions, DH, DH ** -0.5)
    rope_k = _rope_tables(positions, DH, 1.0)
    q = _mm(hb, w_in, cols=(0, 1, o_k // (2 * tn)), out_dtype=BF16, rope=rope_q, tn=2 * tn)
    w_kv = jnp.concatenate([w_in[:, o_k:o_v][:, dup], w_in[:, o_v:o_z][:, dup]], axis=1)
    kv_dup = _mm(hb, w_kv, out_dtype=BF16, rope=rope_k, rope_tiles=1, tn=tn)
    z = _mm(hb, w_in, cols=(o_z // tn, 1, o_k // tn), out_dtype=F32, tn=tn)
    return _swa_attn(sinks, q, kv_dup, z)


def kernel(x, positions, hgrn_lb_logits, l0_w_in, l0_g_norm, l0_w_out, l0_ln_g, l0_ln_b, l1_w_in, l1_cmp_pos_k, l1_cmp_w1_k, l1_cmp_w2_k, l1_cmp_pos_v, l1_cmp_w1_v, l1_cmp_w2_v, l1_w_out, l1_ln_g, l1_ln_b, l2_w_in, l2_sinks, l2_w_out, l2_ln_g, l2_ln_b, l3_w_in, l3_g_norm, l3_w_out, l3_ln_g, l3_ln_b):
    B, S, D = x.shape
    assert B == 1 and D == D_MODEL
    lb = jnp.cumsum(jax.nn.softmax(hgrn_lb_logits.astype(F32), axis=0), axis=0)
    lb = lb - lb[0:1]
    log_lb, log_1m_lb = jnp.log(lb), jnp.log1p(-lb)
    esum = _hgrn_esum()

    h = x.reshape(S, D)
    hb = h.astype(BF16)

    a = _hgrn_layer(hb, l0_w_in, l0_g_norm, log_lb[0], log_1m_lb[0], esum)
    h, hb = _outproj_ln(a, l0_w_out.astype(BF16), h, l0_ln_g, l0_ln_b)

    a = _nsa_layer(hb, positions, l1_w_in, l1_cmp_pos_k, l1_cmp_w1_k, l1_cmp_w2_k,
                   l1_cmp_pos_v, l1_cmp_w1_v, l1_cmp_w2_v)
    h, hb = _outproj_ln(a, l1_w_out.astype(BF16), h, l1_ln_g, l1_ln_b)

    a = _swa_layer(hb, positions, l2_w_in, l2_sinks)
    h, hb = _outproj_ln(a, l2_w_out.astype(BF16), h, l2_ln_g, l2_ln_b)

    a = _hgrn_layer(hb, l3_w_in, l3_g_norm, log_lb[1], log_1m_lb[1], esum)
    h, hb = _outproj_ln(a, l3_w_out.astype(BF16), h, l3_ln_g, l3_ln_b)
    return h.reshape(B, S, D)
```

```python
import functools

import numpy as np
import jax
import jax.numpy as jnp
from jax import lax
from jax.experimental import pallas as pl
from jax.experimental.pallas import tpu as pltpu

F32 = jnp.float32
BF16 = jnp.bfloat16

D_MODEL = 2048
DEPTH = 4
N_MIXERS = 3
DEEPNORM_ALPHA = (2 * DEPTH) ** 0.25
LN_EPS = 1e-5
RMS_EPS = 1e-6
ROPE_THETA = 500000.0
ROPE_FRACTION = 4

HG_HEAD_DIM = 128
HG_HEADS = D_MODEL // HG_HEAD_DIM
HG_CHUNK = 64
HG_SUB = 8
HG_TIME_BLOCK = 2048
HG_PIECE = 512
LOG2_E = 1.4426950408889634

NSA_HEAD_DIM = 128
NSA_HEADS = D_MODEL // NSA_HEAD_DIM
NSA_KV_GROUPS = 4
NSA_REP = NSA_HEADS // NSA_KV_GROUPS
CMP_BLOCK = 32
CMP_STRIDE = 16
CMP_HIDDEN = 256
SLC_BLOCK = 64
SLC_TOPK = 16
NSA_WINDOW = 512
NSA_TQ = 256
NSA_ONES_ROWS = 16
NSA_BLOCKS_PER_STEP = 2
FORCE_BONUS = 1.0e4
MASK_NEG = -1.0e30

SWA_HEAD_DIM = 64
SWA_HEADS = D_MODEL // SWA_HEAD_DIM
SWA_KV_HEADS = 4
SWA_REP = SWA_HEADS // SWA_KV_HEADS
SWA_WINDOW = 128

LANES = 128
VMEM_LIMIT_BYTES = 48 * 1024 * 1024

_NT = (((1,), (1,)), ((), ()))
_TN = (((0,), (0,)), ((), ()))


def _params(*sem):
    return pltpu.CompilerParams(dimension_semantics=sem, vmem_limit_bytes=VMEM_LIMIT_BYTES)


def _silu(x):
    return x * (1.0 / (1.0 + jnp.exp(-x)))


def _mm_kernel(*refs, rope_half, rope_period, rope_tiles, head_major, n_chunks):
    if rope_half:
        x_ref, w_ref, c_ref, s_ref, o_ref, wb_ref = refs
    else:
        x_ref, w_ref, o_ref, wb_ref = refs

    @pl.when(pl.program_id(1) == 0)
    def _():
        wb_ref[...] = w_ref[...].astype(BF16)

    acc = jnp.dot(x_ref[...], wb_ref[...], preferred_element_type=F32)

    def write(with_rope):
        if with_rope:
            cos = c_ref[...]
            sin = s_ref[...]
            lane = lax.broadcasted_iota(jnp.int32, cos.shape, 1)
            first_half = (lane & (rope_period - 1)) < rope_half
        for j in range(n_chunks):
            a = acc[:, j * LANES:(j + 1) * LANES]
            if with_rope:
                up = pltpu.roll(a, LANES - rope_half, 1)
                dn = pltpu.roll(a, rope_half, 1)
                a = a * cos + jnp.where(first_half, up, dn) * sin
            if head_major:
                o_ref[j] = a.astype(o_ref.dtype)
            else:
                o_ref[:, j * LANES:(j + 1) * LANES] = a.astype(o_ref.dtype)

    if rope_half and rope_tiles is not None:
        pl.when(pl.program_id(0) < rope_tiles)(functools.partial(write, True))
        pl.when(pl.program_id(0) >= rope_tiles)(functools.partial(write, False))
    else:
        write(bool(rope_half))


def _col_tiles(w, cols, tn):
    return (0, 1, w.shape[1] // tn) if cols is None else cols


def _mm(x, w, *, out_dtype, cols=None, rope=None, rope_tiles=None, head_major=False, tn=512):
    S, K = x.shape
    tm = min(1024, S)
    first, stride, n_tiles = _col_tiles(w, cols, tn)
    N = n_tiles * tn
    n_chunks = tn // LANES
    in_specs = [pl.BlockSpec((tm, K), lambda j, i: (i, 0)),
                pl.BlockSpec((K, tn), lambda j, i: (0, first + stride * j))]
    args = [x, w]
    half = period = 0
    if rope is not None:
        cos_t, sin_t, half, period = rope
        in_specs += [pl.BlockSpec((tm, LANES), lambda j, i: (i, 0))] * 2
        args += [cos_t, sin_t]
    if head_major:
        out_shape = jax.ShapeDtypeStruct((N // LANES, S, LANES), out_dtype)
        out_spec = pl.BlockSpec((n_chunks, tm, LANES), lambda j, i: (j, i, 0))
    else:
        out_shape = jax.ShapeDtypeStruct((S, N), out_dtype)
        out_spec = pl.BlockSpec((tm, tn), lambda j, i: (i, j))
    return pl.pallas_call(
        functools.partial(_mm_kernel, rope_half=half, rope_period=period, rope_tiles=rope_tiles,
                          head_major=head_major, n_chunks=n_chunks),
        grid=(n_tiles, S // tm),
        in_specs=in_specs,
        out_specs=out_spec,
        out_shape=out_shape,
        scratch_shapes=[pltpu.VMEM((K, tn), BF16)],
        compiler_params=_params("arbitrary", "arbitrary"),
        name="in_proj",
    )(*args)


def _mm_t_kernel(*refs, rope_half, scale, n_heads, n_tiles, tk):
    if rope_half:
        w_ref, x_ref, c_ref, s_ref, o_ref, wt_ref = refs
    else:
        w_ref, x_ref, o_ref, wt_ref = refs

    @pl.when(pl.program_id(1) == 0)
    def _():
        wt_ref[...] = w_ref[...].T.astype(BF16)

    acc = lax.dot_general(wt_ref[...], x_ref[...], _NT, preferred_element_type=F32)
    for c in range(n_heads):
        a = acc[c * LANES:(c + 1) * LANES]
        if scale != 1.0:
            a = a * scale
        if rope_half:
            rot = 2 * rope_half
            swapped = jnp.concatenate([a[rope_half:rot], a[:rope_half]], axis=0)
            a = jnp.concatenate([a[:rot] * c_ref[...] + swapped * s_ref[...], a[rot:]], axis=0)
        for b in range(n_tiles):
            o_ref[c, b] = a[:, b * tk:(b + 1) * tk].astype(o_ref.dtype)


def _mm_t(x, w, *, out_dtype, tk, cols=None, rope=None, scale=1.0, tn=512):
    S, K = x.shape
    tm = min(1024, S)
    first, stride, n_tiles = _col_tiles(w, cols, tn)
    N = n_tiles * tn
    in_specs = [pl.BlockSpec((K, tn), lambda j, i: (0, first + stride * j)),
                pl.BlockSpec((tm, K), lambda j, i: (i, 0))]
    args = [w, x]
    half = 0
    if rope is not None:
        cos_t, sin_t, half = rope
        in_specs += [pl.BlockSpec((2 * half, tm), lambda j, i: (0, i))] * 2
        args += [cos_t, sin_t]
    return pl.pallas_call(
        functools.partial(_mm_t_kernel, rope_half=half, scale=scale, n_heads=tn // LANES,
                          n_tiles=tm // tk, tk=tk),
        grid=(n_tiles, S // tm),
        in_specs=in_specs,
        out_specs=pl.BlockSpec((tn // LANES, tm // tk, LANES, tk), lambda j, i: (j, i, 0, 0)),
        out_shape=jax.ShapeDtypeStruct((N // LANES, S // tk, LANES, tk), out_dtype),
        scratch_shapes=[pltpu.VMEM((tn, K), BF16)],
        compiler_params=_params("arbitrary", "arbitrary"),
        name="in_proj_t",
    )(*args)


def _outproj_ln_kernel(a_ref, w_ref, h_ref, g_ref, b_ref, o_ref, ob_ref):
    y = jnp.dot(a_ref[...], w_ref[...], preferred_element_type=F32)
    u = DEEPNORM_ALPHA * h_ref[...] + y
    mu = jnp.mean(u, axis=-1, keepdims=True)
    xc = u - mu
    var = jnp.mean(xc * xc, axis=-1, keepdims=True)
    out = xc * lax.rsqrt(var + LN_EPS) * g_ref[...] + b_ref[...]
    o_ref[...] = out
    ob_ref[...] = out.astype(BF16)


def _outproj_ln(a, w, h, g, b):
    S, D = h.shape
    tm = min(512, S)
    row = pl.BlockSpec((tm, D), lambda i: (i, 0))
    vec = pl.BlockSpec((1, D), lambda i: (0, 0))
    return pl.pallas_call(
        _outproj_ln_kernel,
        grid=(S // tm,),
        in_specs=[row, pl.BlockSpec((D, D), lambda i: (0, 0)), row, vec, vec],
        out_specs=[row, row],
        out_shape=[jax.ShapeDtypeStruct((S, D), F32), jax.ShapeDtypeStruct((S, D), BF16)],
        compiler_params=_params("parallel"),
        name="outproj_ln",
    )(a, w, h, g.reshape(1, D), b.reshape(1, D))


def _hgrn_kernel(x_ref, wq_ref, wf_ref, wv_ref, wz_ref, llb_ref, l1m_ref, gn_ref,
                 o_ref, st_ref, u_ref, wb_ref, *, n_pieces, piece):
    C, SUB, DK = HG_CHUNK, HG_SUB, HG_HEAD_DIM
    n_lvl = (C // (2 * SUB)).bit_length()
    n_pc = piece // C

    @pl.when(pl.program_id(1) == 0)
    def _():
        st_ref[...] = jnp.zeros_like(st_ref)
        for i, w_ref in enumerate((wq_ref, wf_ref, wv_ref, wz_ref)):
            wb_ref[:, i * DK:(i + 1) * DK] = w_ref[...].astype(BF16)

    r = lax.broadcasted_iota(jnp.int32, (C, C), 0)
    c = lax.broadcasted_iota(jnp.int32, (C, C), 1)
    causal = c <= r
    halves = [SUB << lvl for lvl in range(n_lvl)]
    cum_rows = []
    for h in halves:
        same_h = (r // h) == (c // h)
        cum_rows += [causal & same_h, same_h]
    cum_one = jnp.concatenate([m.astype(F32) for m in cum_rows], axis=0).astype(BF16)
    cum_mat = jnp.concatenate([cum_one] * 3, axis=1)
    diag_mask = causal & ((r // SUB) == (c // SUB))
    log_lb = llb_ref[...]
    log_1m_lb = l1m_ref[...]
    lane_c = lax.broadcasted_iota(jnp.int32, (SUB, C), 1)
    chunks = [slice(ci * C, (ci + 1) * C) for ci in range(n_pc)]

    def rows_to_lanes(x):
        return jnp.concatenate([x[sl] for sl in chunks], axis=1)

    def lanes_to_rows(x):
        return jnp.concatenate([x[:, ci * DK:(ci + 1) * DK] for ci in range(n_pc)], axis=0)

    def placed(x, lo):
        parts = [jnp.zeros((lo, DK), F32), x, jnp.zeros((C - lo - x.shape[0], DK), F32)]
        return jnp.concatenate([p for p in parts if p.shape[0]], axis=0)

    def front(pi):
        rows = slice(pi * piece, (pi + 1) * piece)
        proj = jnp.dot(x_ref[rows, :], wb_ref[...], preferred_element_type=F32)
        q, fl, v, z = (proj[:, i * DK:(i + 1) * DK] for i in range(4))
        vb = v.astype(BF16)
        log_sig = jnp.minimum(fl, 0.0) - jnp.log(1.0 + jnp.exp(-jnp.abs(fl)))
        t = log_1m_lb + log_sig
        log_f = jnp.maximum(log_lb, t) + jnp.log(1.0 + jnp.exp(-jnp.abs(log_lb - t)))
        log2_k = (t - fl) * LOG2_E
        k = jnp.exp2(log2_k)
        lf = rows_to_lanes(log_f)
        lf_hi = lf.astype(BF16)
        lf_r1 = lf - lf_hi.astype(F32)
        lf_mid = lf_r1.astype(BF16)
        lf_lo = (lf_r1 - lf_mid.astype(F32)).astype(BF16)
        cums = jnp.dot(cum_mat, jnp.concatenate([lf_hi, lf_mid, lf_lo], axis=0),
                       preferred_element_type=F32) * LOG2_E
        lvl = [cums[i * C:(i + 1) * C] for i in range(2 * n_lvl)]
        loc_top, tot_top = lvl[-2], lvl[-1]
        tot_chunk = tot_top[:C // 2] + tot_top[C // 2:]
        cum = [lanes_to_rows(a) for a in [
            jnp.concatenate([loc_top[:C // 2], loc_top[C // 2:] + tot_top[:C // 2]], axis=0),
            jnp.concatenate([tot_chunk, tot_chunk], axis=0)] + lvl]
        b_full, b_tot = cum[0], cum[1]
        qe = (q * jnp.exp2(b_full)).astype(BF16)
        kd = (k * jnp.exp2(b_tot - b_full)).astype(BF16)
        q_lvl = [q * jnp.exp2(cum[2 + 2 * l]) for l in range(n_lvl)]
        k_lvl = [k * jnp.exp2(cum[3 + 2 * l] - cum[2 + 2 * l]) for l in range(n_lvl)]
        b_loc = cum[2]
        c_row = b_loc - log2_k

        a_offs, a_diags = [], []
        for ci, ch in enumerate(chunks):
            blocks = []
            for j in range(C // SUB):
                base = ci * C + j * SUB
                sl = slice(base, base + SUB)
                blk = jnp.zeros((SUB, C), F32)
                for s in range(SUB):
                    y = q[sl] * jnp.exp2(jnp.minimum(
                        b_loc[sl] - c_row[base + s:base + s + 1, :], log2_k[base + s:base + s + 1, :]))
                    blk = jnp.where(lane_c == j * SUB + s, jnp.sum(y, axis=-1, keepdims=True), blk)
                blocks.append(blk)
            a_diags.append(jnp.concatenate(blocks, axis=0))
            lhs, rhs = [], []
            for l, h in enumerate(halves):
                q_c, k_c = q_lvl[l][ch], k_lvl[l][ch]
                for lo in range(0, C, 2 * h):
                    lhs.append(placed(q_c[lo + h:lo + 2 * h], lo + h))
                    rhs.append(placed(k_c[lo:lo + h], lo))
            a_offs.append(lax.dot_general(jnp.concatenate(lhs, axis=1).astype(BF16),
                                          jnp.concatenate(rhs, axis=1).astype(BF16),
                                          _NT, preferred_element_type=F32))
            u_ref[pi * n_pc + ci] = lax.dot_general(vb[ch], kd[ch], _TN,
                                                    preferred_element_type=F32)
        per_chunk = []
        for ci, ch in enumerate(chunks):
            a = jnp.where(diag_mask, a_diags[ci], 0.0) + a_offs[ci]
            o_intra = jnp.dot(a.astype(BF16), vb[ch], preferred_element_type=F32)
            per_chunk.append((qe[ch], o_intra, b_tot[ci * C:ci * C + 8]))
        return per_chunk, z

    fronts = [front(pi) for pi in range(n_pieces)]

    st = st_ref[...]
    outs = []
    for pi, (per_chunk, _) in enumerate(fronts):
        for ci, (qe_c, o_intra, b_tot8) in enumerate(per_chunk):
            outs.append(o_intra + lax.dot_general(qe_c, st.astype(BF16), _NT,
                                                  preferred_element_type=F32))
            st = st * jnp.exp2(jnp.tile(b_tot8, (DK // 8, 1))) + u_ref[pi * n_pc + ci]
    st_ref[...] = st

    o = jnp.concatenate(outs, axis=0)
    z = jnp.concatenate([z_p for _, z_p in fronts], axis=0)
    ms = jnp.mean(o * o, axis=-1, keepdims=True)
    out = o * lax.rsqrt(ms + RMS_EPS) * gn_ref[...] * _silu(z)
    o_ref[...] = out.astype(o_ref.dtype)


def _hgrn_mixer(hb, w_in, log_lb, log_1m_lb, g_norm):
    S, K = hb.shape
    D, DK, H = D_MODEL, HG_HEAD_DIM, HG_HEADS
    T = min(HG_TIME_BLOCK, S)
    piece = min(HG_PIECE, T)

    def w_col(off):
        return pl.BlockSpec((K, DK), lambda h, t, off=off: (0, off + h))

    vec = pl.BlockSpec((1, DK), lambda h, t: (0, h))
    return pl.pallas_call(
        functools.partial(_hgrn_kernel, n_pieces=T // piece, piece=piece),
        grid=(H, S // T),
        in_specs=[pl.BlockSpec((T, K), lambda h, t: (t, 0)),
                  w_col(0), w_col(H), w_col(2 * H), w_col(3 * H), vec, vec, vec],
        out_specs=pl.BlockSpec((T, DK), lambda h, t: (t, h)),
        out_shape=jax.ShapeDtypeStruct((S, D), BF16),
        scratch_shapes=[pltpu.VMEM((DK, DK), F32),
                        pltpu.VMEM((T // HG_CHUNK, DK, DK), F32),
                        pltpu.VMEM((K, 4 * DK), BF16)],
        compiler_params=_params("parallel", "arbitrary"),
        name="hgrn_mixer",
    )(hb, w_in, w_in, w_in, w_in, log_lb.reshape(1, D), log_1m_lb.reshape(1, D),
      g_norm.reshape(1, D))


def _nsa_compress_kernel(x_ref, pos_ref, w1_ref, w2_ref, o_ref, *, transposed):
    DH = NSA_HEAD_DIM
    n = x_ref.shape[0] // CMP_STRIDE
    h1 = jnp.zeros((n, CMP_HIDDEN), F32)
    h2 = jnp.zeros((n, CMP_HIDDEN), F32)
    for l in range(CMP_STRIDE):
        x_l = x_ref[pl.ds(l, n, stride=CMP_STRIDE), :]
        l2 = CMP_STRIDE + l
        h1 = h1 + jnp.dot((x_l + pos_ref[l:l + 1, :]).astype(BF16),
                          w1_ref[l * DH:(l + 1) * DH, :].astype(BF16), preferred_element_type=F32)
        h2 = h2 + jnp.dot((x_l + pos_ref[l2:l2 + 1, :]).astype(BF16),
                          w1_ref[l2 * DH:(l2 + 1) * DH, :].astype(BF16), preferred_element_type=F32)
    hid = h1 + pltpu.roll(h2, n - 1, 0)
    act = _silu(hid).astype(BF16)
    if transposed:
        o_ref[...] = lax.dot_general(w2_ref[...].T.astype(BF16), act, _NT,
                                     preferred_element_type=F32).astype(o_ref.dtype)
    else:
        o_ref[...] = jnp.dot(act, w2_ref[...].astype(BF16),
                             preferred_element_type=F32).astype(o_ref.dtype)


def _nsa_compress(x, first, pos, w1, w2, *, transposed):
    G = NSA_KV_GROUPS
    _, S, DH = x.shape
    n = S // CMP_STRIDE
    whole = lambda a: pl.BlockSpec(a.shape, lambda g: (0,) * a.ndim)
    out_block = (None, DH, n) if transposed else (None, n, DH)
    return pl.pallas_call(
        functools.partial(_nsa_compress_kernel, transposed=transposed),
        grid=(G,),
        in_specs=[pl.BlockSpec((None, S, DH), lambda g: (first + g, 0, 0)),
                  whole(pos), whole(w1), whole(w2)],
        out_specs=pl.BlockSpec(out_block, lambda g: (g, 0, 0)),
        out_shape=jax.ShapeDtypeStruct((G,) + out_block[1:], BF16),
        compiler_params=_params("parallel"),
        name="nsa_compress",
    )(x, pos, w1, w2)


def _nsa_kernel(q_ref, kc_ref, vct_ref, ks_ref, vst_ref, kw_ref, vwt_ref, oh_ref, aggt_ref,
                gate_ref, z_ref, o_ref, m_ref, acc_ref, s_ref, *, tq, n_sub):
    R, DH = NSA_REP, NSA_HEAD_DIM
    ncp = kc_ref.shape[1]

    def per_head(x):
        return jnp.concatenate([x] * R, axis=1)

    def rows8(x, n):
        return jnp.tile(x, (n // 8, 1))

    def col_max8(x):
        mx = x[0:8]
        for i in range(1, x.shape[0] // 8):
            mx = jnp.maximum(mx, x[8 * i:8 * i + 8])
        for shift in (4, 2, 1):
            mx = jnp.maximum(mx, pltpu.roll(mx, shift, 0))
        return mx

    def key_rows(ref, kt):
        return ref[0, pl.ds(pl.multiple_of(kt * tq, tq), tq), :]

    c_k = lax.broadcasted_iota(jnp.int32, (tq, tq), 0)
    i_q = lax.broadcasted_iota(jnp.int32, (tq, tq), 1)
    causal = per_head(c_k <= i_q)
    aggt = aggt_ref[...]

    def front(sub):
        qb = pl.program_id(1) * n_sub + sub
        start = qb * tq
        q = jnp.concatenate([q_ref[r, sub] for r in range(R)], axis=1)

        n_c = lax.broadcasted_iota(jnp.int32, (ncp, tq), 0)
        t_c = start + lax.broadcasted_iota(jnp.int32, (ncp, tq), 1)
        cmask = per_head((n_c * CMP_STRIDE + CMP_BLOCK - 1) <= t_c)
        s = jnp.where(cmask, jnp.dot(kc_ref[0], q, preferred_element_type=F32), -jnp.inf)
        m = jnp.max(s, axis=0, keepdims=True)
        m = jnp.where(m == -jnp.inf, 0.0, m)
        e = jnp.exp2(s - m)
        p = e * (1.0 / jnp.maximum(jnp.sum(e, axis=0, keepdims=True), 1e-30))
        o_cmp = jnp.dot(vct_ref[0], p.astype(BF16), preferred_element_type=F32)
        p_sum = p[:, :tq]
        for r in range(1, R):
            p_sum = p_sum + p[:, r * tq:(r + 1) * tq]

        no_old = jnp.where(qb >= 2, 0, NSA_WINDOW)
        no_mid = jnp.where(qb >= 1, 0, NSA_WINDOW)
        old_ok = per_head((2 * tq + i_q - c_k + no_old) < NSA_WINDOW)
        mid_ok = per_head((tq + i_q - c_k + no_mid) < NSA_WINDOW)
        kt_old, kt_mid = jnp.maximum(qb - 2, 0), jnp.maximum(qb - 1, 0)
        k_win = jnp.concatenate([key_rows(kw_ref, kt_old), key_rows(kw_ref, kt_mid),
                                 key_rows(kw_ref, qb)], axis=0)
        s_w = jnp.where(jnp.concatenate([old_ok, mid_ok, causal], axis=0),
                        jnp.dot(k_win, q, preferred_element_type=F32), MASK_NEG)
        p_w = jnp.exp2(s_w - rows8(col_max8(s_w), 3 * tq)).astype(BF16)
        v_win = jnp.concatenate([vwt_ref[0, kt_old], vwt_ref[0, kt_mid], vwt_ref[0, qb]], axis=1)
        acc_w = jnp.dot(jnp.concatenate([v_win, jnp.ones((NSA_ONES_ROWS, 3 * tq), BF16)], axis=0),
                        p_w, preferred_element_type=F32)
        o_win = acc_w[:DH] * rows8(1.0 / acc_w[DH:DH + 8], DH)

        p_hi = p_sum.astype(BF16)
        p_lo = (p_sum - p_hi.astype(F32)).astype(BF16)
        imp = (jnp.dot(aggt, p_hi, preferred_element_type=F32)
               + jnp.dot(aggt, p_lo, preferred_element_type=F32))
        j_s = lax.broadcasted_iota(jnp.int32, (LANES, tq), 0)
        t_s = start + lax.broadcasted_iota(jnp.int32, (LANES, tq), 1)
        cur = lax.shift_right_logical(t_s, 6)
        forced = (j_s == 0) | (j_s == cur) | (j_s == cur - 1)
        allowed = j_s * SLC_BLOCK <= t_s
        sel = jnp.where(forced & allowed, 1.0, 0.0)
        val = jnp.where(allowed & ~forced, imp, -jnp.inf)
        row_f = j_s.astype(F32)
        for _ in range(SLC_TOPK - 3):
            mx = jnp.max(val, axis=0, keepdims=True)
            idx = jnp.min(jnp.where(val == mx, row_f, float(LANES)), axis=0, keepdims=True)
            pick = row_f == idx
            sel = jnp.where(pick, 1.0, sel)
            val = jnp.where(pick, -jnp.inf, val)
        m_neg = jnp.where(sel > 0.0, 0.0, MASK_NEG).astype(BF16)
        q_aug = jnp.concatenate([q, per_head(m_neg)], axis=0)
        return qb, q_aug, o_cmp, o_win

    ones_rows = jnp.ones((NSA_ONES_ROWS, tq), BF16)

    def flash_init():
        m_ref[...] = jnp.full(m_ref.shape, -jnp.inf, F32)
        acc_ref[...] = jnp.zeros_like(acc_ref)

    def flash_update(s, vt_tile):
        m_prev = m_ref[...]
        m_next = jnp.maximum(m_prev, col_max8(s))
        alpha = jnp.exp2(m_prev - m_next)
        p = jnp.exp2(s - rows8(m_next, tq)).astype(BF16)
        v_aug = jnp.concatenate([vt_tile, ones_rows], axis=0)
        acc_ref[...] = (acc_ref[...] * rows8(alpha, DH + NSA_ONES_ROWS)
                        + jnp.dot(v_aug, p, preferred_element_type=F32))
        m_ref[...] = m_next

    def flash_result():
        acc = acc_ref[...]
        return acc[:DH] * rows8(1.0 / acc[DH:DH + 8], DH)

    def back(sub, qb, q_aug, o_cmp, o_win):
        def slc_scores(kt):
            oh = oh_ref[pl.ds(pl.multiple_of(kt * tq, tq), tq), :]
            k_aug = jnp.concatenate([key_rows(ks_ref, kt), oh], axis=1)
            return jnp.dot(k_aug, q_aug, preferred_element_type=F32)

        flash_init()
        s_ref[...] = slc_scores(0)

        def slc_step(kt):
            s_cur = s_ref[...]
            s_ref[...] = slc_scores(kt + 1)
            flash_update(s_cur, vst_ref[0, kt])

        def slc_quad(i, carry):
            for u in range(4):
                slc_step(4 * i + u)
            return carry

        lax.fori_loop(0, lax.shift_right_logical(qb, 2), slc_quad, 0)
        done = qb & ~3

        @pl.when((qb & 2) != 0)
        def _():
            slc_step(done)
            slc_step(done + 1)

        @pl.when((qb & 1) != 0)
        def _():
            slc_step(qb - 1)

        flash_update(jnp.where(causal, s_ref[...], MASK_NEG), vst_ref[0, qb])
        o_slc = flash_result()

        head0 = pl.program_id(0) * R

        def gate(branch, r):
            logit = gate_ref[0, sub, pl.ds(branch * NSA_HEADS + head0 + r, 1), :]
            return jnp.broadcast_to(1.0 / (1.0 + jnp.exp(-logit)), (DH, tq))

        rows = slice(sub * tq, (sub + 1) * tq)
        for r in range(R):
            sl = slice(r * tq, (r + 1) * tq)
            g = [gate(b, r) for b in range(3)]
            o_t = g[0] * o_cmp[:, sl] + g[1] * o_slc[:, sl] + g[2] * o_win[:, sl]
            z = z_ref[rows, r * DH:(r + 1) * DH]
            o_ref[rows, r * DH:(r + 1) * DH] = (o_t.T * _silu(z)).astype(o_ref.dtype)

    fronts = [front(sub) for sub in range(n_sub)]
    for sub in range(n_sub):
        back(sub, *fronts[sub])


def _nsa_attn(q_t, kc, vc_t, ksw, vsw_t, onehot, agg_t, gate_t, z):
    G, R, DH = NSA_KV_GROUPS, NSA_REP, NSA_HEAD_DIM
    S = ksw.shape[1]
    tq, n_sub = NSA_TQ, NSA_BLOCKS_PER_STEP
    assert S % (n_sub * tq) == 0 and NSA_WINDOW <= 2 * tq and S // SLC_BLOCK <= LANES
    ncp = kc.shape[1]
    n_t = S // tq
    rows = lambda off: pl.BlockSpec((1, S, DH), lambda g, i, off=off: (off + g, 0, 0))
    tiles_t = lambda off: pl.BlockSpec((1, n_t, DH, tq), lambda g, i, off=off: (off + g, 0, 0, 0))
    return pl.pallas_call(
        functools.partial(_nsa_kernel, tq=tq, n_sub=n_sub),
        grid=(G, n_t // n_sub),
        in_specs=[pl.BlockSpec((R, n_sub, DH, tq), lambda g, i: (g, i, 0, 0)),
                  pl.BlockSpec((1, ncp, DH), lambda g, i: (g, 0, 0)),
                  pl.BlockSpec((1, DH, ncp), lambda g, i: (g, 0, 0)),
                  rows(0), tiles_t(0), rows(G), tiles_t(G),
                  pl.BlockSpec((S, LANES), lambda g, i: (0, 0)),
                  pl.BlockSpec((LANES, ncp), lambda g, i: (0, 0)),
                  pl.BlockSpec((1, n_sub, LANES, tq), lambda g, i: (0, i, 0, 0)),
                  pl.BlockSpec((n_sub * tq, R * DH), lambda g, i: (i, g))],
        out_specs=pl.BlockSpec((n_sub * tq, R * DH), lambda g, i: (i, g)),
        out_shape=jax.ShapeDtypeStruct((S, D_MODEL), BF16),
        scratch_shapes=[pltpu.VMEM((8, R * tq), F32),
                        pltpu.VMEM((DH + NSA_ONES_ROWS, R * tq), F32),
                        pltpu.VMEM((tq, R * tq), F32)],
        compiler_params=_params("parallel", "arbitrary"),
        name="nsa_attn",
    )(q_t, kc, vc_t, ksw, vsw_t, ksw, vsw_t, onehot, agg_t, gate_t, z)


def _swa_kernel(sink_ref, q_ref, kp_ref, kc_ref, vp_ref, vc_ref, z_ref, o_ref):
    W, R = SWA_WINDOW, SWA_REP
    n = pl.program_id(0)
    i_q = lax.broadcasted_iota(jnp.int32, (W, 2 * W), 0)
    c_k = lax.broadcasted_iota(jnp.int32, (W, 2 * W), 1)
    diff = i_q - (c_k - W)
    mask = (diff >= 0) & (diff < W) & ((n > 0) | (c_k >= W))
    lane = lax.broadcasted_iota(jnp.int32, (W, LANES), 1)
    low = lane < SWA_HEAD_DIM
    for g in range(SWA_KV_HEADS):
        gl = slice(g * LANES, (g + 1) * LANES)
        kk = jnp.concatenate([kp_ref[:, gl], kc_ref[:, gl]], axis=0)
        vv = jnp.concatenate([vp_ref[:, gl], vc_ref[:, gl]], axis=0)
        qs = []
        for r in range(R):
            h = g * R + r
            q2 = q_ref[:, (h // 2) * LANES:(h // 2 + 1) * LANES]
            qs.append(jnp.where(low if h % 2 == 0 else ~low, q2, jnp.zeros_like(q2)))
        s_all = lax.dot_general(jnp.concatenate(qs, axis=0), kk, _NT,
                                preferred_element_type=F32)
        ps, inv_l = [], []
        for r in range(R):
            sink = sink_ref[g * R + r]
            s = jnp.where(mask, s_all[r * W:(r + 1) * W], -jnp.inf)
            m = jnp.maximum(jnp.max(s, axis=1, keepdims=True), sink)
            e = jnp.exp(s - m)
            inv_l.append(1.0 / (jnp.sum(e, axis=1, keepdims=True) + jnp.exp(sink - m)))
            ps.append(e.astype(BF16))
        o_all = jnp.dot(jnp.concatenate(ps, axis=0), vv, preferred_element_type=F32)
        for pr in range(R // 2):
            o_even = o_all[(2 * pr) * W:(2 * pr + 1) * W] * inv_l[2 * pr]
            o_odd = o_all[(2 * pr + 1) * W:(2 * pr + 2) * W] * inv_l[2 * pr + 1]
            col = slice((g * R // 2 + pr) * LANES, (g * R // 2 + pr + 1) * LANES)
            o_ref[:, col] = (jnp.where(low, o_even, o_odd) * _silu(z_ref[:, col])).astype(o_ref.dtype)


def _swa_attn(sinks, q, kv_dup, z):
    S, D = q.shape
    W = SWA_WINDOW
    wide = SWA_KV_HEADS * LANES
    cur = lambda i: (i, 0)
    k_prev, k_cur = (lambda i: (jnp.maximum(i - 1, 0), 0)), cur
    v_prev, v_cur = (lambda i: (jnp.maximum(i - 1, 0), 1)), (lambda i: (i, 1))
    return pl.pallas_call(
        _swa_kernel,
        grid=(S // W,),
        in_specs=[pl.BlockSpec(memory_space=pltpu.SMEM),
                  pl.BlockSpec((W, D), cur),
                  pl.BlockSpec((W, wide), k_prev), pl.BlockSpec((W, wide), k_cur),
                  pl.BlockSpec((W, wide), v_prev), pl.BlockSpec((W, wide), v_cur),
                  pl.BlockSpec((W, D), cur)],
        out_specs=pl.BlockSpec((W, D), cur),
        out_shape=jax.ShapeDtypeStruct((S, D), BF16),
        compiler_params=_params("parallel"),
        name="swa_attn",
    )(sinks, q, kv_dup, kv_dup, kv_dup, kv_dup, z)


def _rope_tables(positions, head_dim, scale):
    rot = head_dim // ROPE_FRACTION
    half = rot // 2
    inv_freq = ROPE_THETA ** (-jnp.arange(0, rot, 2, dtype=F32) / rot)
    ang = positions.reshape(-1).astype(F32)[:, None] * inv_freq
    cos, sin = jnp.cos(ang), jnp.sin(ang)
    S = cos.shape[0]
    rest = head_dim - rot
    cos_h = jnp.concatenate([cos, cos, jnp.ones((S, rest), F32)], axis=1)
    sin_h = jnp.concatenate([-sin, sin, jnp.zeros((S, rest), F32)], axis=1)
    reps = LANES // head_dim
    return (jnp.tile(cos_h, (1, reps)) * scale, jnp.tile(sin_h, (1, reps)) * scale, half, head_dim)


def _nsa_constants(S):
    n_cmp = S // CMP_STRIDE - 1
    ncp = S // CMP_STRIDE
    n_slc = S // SLC_BLOCK
    ratio = SLC_BLOCK // CMP_STRIDE
    i = np.arange(ncp)[:, None]
    j = np.arange(LANES)[None, :]
    agg = ((i >= ratio * j - CMP_BLOCK // CMP_STRIDE + 1) & (i <= ratio * j + ratio - 1)
           & (i < n_cmp) & (j < n_slc))
    onehot = (np.arange(S)[:, None] // SLC_BLOCK) == j
    return jnp.asarray(onehot, dtype=BF16), jnp.asarray(agg.T, dtype=BF16)


def _rope_tables_t(positions, head_dim):
    rot = head_dim // ROPE_FRACTION
    inv_freq = ROPE_THETA ** (-jnp.arange(0, rot, 2, dtype=F32) / rot)
    ang = inv_freq[:, None] * positions.reshape(-1).astype(F32)[None, :]
    cos, sin = jnp.cos(ang), jnp.sin(ang)
    return jnp.concatenate([cos, cos], axis=0), jnp.concatenate([-sin, sin], axis=0), rot // 2


def _hgrn_layer(hb, w_in, g_norm, log_lb, log_1m_lb):
    return _hgrn_mixer(hb, w_in, log_lb, log_1m_lb, g_norm)


def _nsa_layer(hb, positions, w_in, pos_k, w1_k, w2_k, pos_v, w1_v, w2_v):
    S = hb.shape[0]
    H, G, R, DH = NSA_HEADS, NSA_KV_GROUPS, NSA_REP, NSA_HEAD_DIM
    gw = G * DH
    assert gw == 512 and (H * DH) % gw == 0
    t_kc = H * DH // gw
    o_gate = H * DH + 6 * gw
    o_z = o_gate + 3 * H
    rope_k = _rope_tables(positions, DH, 1.0)
    tq = NSA_TQ

    wide = 2 * gw
    q_t = _mm_t(hb, w_in, cols=(0, 1, H * DH // wide), out_dtype=BF16, tk=tq, tn=wide,
                rope=_rope_tables_t(positions, DH), scale=DH ** -0.5 * LOG2_E)
    kv_c = _mm(hb, w_in, cols=(t_kc, 1, 2), out_dtype=F32, rope=rope_k, rope_tiles=1,
               head_major=True)
    ksw = _mm(hb, w_in, cols=(t_kc + 2, 2, 2), out_dtype=BF16, rope=rope_k, head_major=True)
    vsw_t = _mm_t(hb, w_in, cols=(t_kc + 3, 2, 2), out_dtype=BF16, tk=tq)
    z = _mm(hb, w_in[:, o_z:], out_dtype=F32, tn=wide)
    w_gate = jnp.pad(w_in[:, o_gate:o_z], ((0, 0), (0, LANES - 3 * H)))
    gate_t = _mm_t(hb, w_gate, out_dtype=F32, tk=tq, tn=LANES)

    kc = _nsa_compress(kv_c, 0, pos_k, w1_k, w2_k, transposed=False)
    vc_t = _nsa_compress(kv_c, G, pos_v, w1_v, w2_v, transposed=True)
    onehot, agg_t = _nsa_constants(S)
    return _nsa_attn(q_t, kc, vc_t, ksw, vsw_t, onehot, agg_t, gate_t, z)


def _swa_layer(hb, positions, w_in, sinks):
    H, KV, DH = SWA_HEADS, SWA_KV_HEADS, SWA_HEAD_DIM
    o_k = H * DH
    o_v = o_k + KV * DH
    o_z = o_v + KV * DH
    tn = 512
    assert o_k % tn == 0 and o_z % tn == 0
    dup = np.concatenate([np.tile(np.arange(g * DH, (g + 1) * DH), 2) for g in range(KV)])
    rope_q = _rope_tables(positions, DH, DH ** -0.5)
    rope_k = _rope_tables(positions, DH, 1.0)
    q = _mm(hb, w_in, cols=(0, 1, o_k // (2 * tn)), out_dtype=BF16, rope=rope_q, tn=2 * tn)
    w_kv = jnp.concatenate([w_in[:, o_k:o_v][:, dup], w_in[:, o_v:o_z][:, dup]], axis=1)
    kv_dup = _mm(hb, w_kv, out_dtype=BF16, rope=rope_k, rope_tiles=1, tn=tn)
    z = _mm(hb, w_in, cols=(o_z // tn, 1, o_k // tn), out_dtype=F32, tn=tn)
    return _swa_attn(sinks, q, kv_dup, z)


def kernel(x, positions, hgrn_lb_logits, l0_w_in, l0_g_norm, l0_w_out, l0_ln_g, l0_ln_b, l1_w_in, l1_cmp_pos_k, l1_cmp_w1_k, l1_cmp_w2_k, l1_cmp_pos_v, l1_cmp_w1_v, l1_cmp_w2_v, l1_w_out, l1_ln_g, l1_ln_b, l2_w_in, l2_sinks, l2_w_out, l2_ln_g, l2_ln_b, l3_w_in, l3_g_norm, l3_w_out, l3_ln_g, l3_ln_b):
    B, S, D = x.shape
    assert B == 1 and D == D_MODEL
    lb = jnp.cumsum(jax.nn.softmax(hgrn_lb_logits.astype(F32), axis=0), axis=0)
    lb = lb - lb[0:1]
    log_lb, log_1m_lb = jnp.log(lb), jnp.log1p(-lb)

    h = x.reshape(S, D)
    hb = h.astype(BF16)

    a = _hgrn_layer(hb, l0_w_in, l0_g_norm, log_lb[0], log_1m_lb[0])
    h, hb = _outproj_ln(a, l0_w_out.astype(BF16), h, l0_ln_g, l0_ln_b)

    a = _nsa_layer(hb, positions, l1_w_in, l1_cmp_pos_k, l1_cmp_w1_k, l1_cmp_w2_k,
                   l1_cmp_pos_v, l1_cmp_w1_v, l1_cmp_w2_v)
    h, hb = _outproj_ln(a, l1_w_out.astype(BF16), h, l1_ln_g, l1_ln_b)

    a = _swa_layer(hb, positions, l2_w_in, l2_sinks)
    h, hb = _outproj_ln(a, l2_w_out.astype(BF16), h, l2_ln_g, l2_ln_b)

    a = _hgrn_layer(hb, l3_w_in, l3_g_norm, log_lb[1], log_1m_lb[1])
    h, hb = _outproj_ln(a, l3_w_out.astype(BF16), h, l3_ln_g, l3_ln_b)
    return h.reshape(B, S, D)
```

```python
import functools

import numpy as np
import jax
import jax.numpy as jnp
from jax import lax
from jax.experimental import pallas as pl
from jax.experimental.pallas import tpu as pltpu

F32 = jnp.float32
BF16 = jnp.bfloat16

D_MODEL = 2048
DEPTH = 4
N_MIXERS = 3
DEEPNORM_ALPHA = (2 * DEPTH) ** 0.25
LN_EPS = 1e-5
RMS_EPS = 1e-6
ROPE_THETA = 500000.0
ROPE_FRACTION = 4

HG_HEAD_DIM = 128
HG_HEADS = D_MODEL // HG_HEAD_DIM
HG_CHUNK = 64
HG_SUB = 8
HG_TIME_BLOCK = 2048
HG_PIECE = 512
LOG2_E = 1.4426950408889634

NSA_HEAD_DIM = 128
NSA_HEADS = D_MODEL // NSA_HEAD_DIM
NSA_KV_GROUPS = 4
NSA_REP = NSA_HEADS // NSA_KV_GROUPS
CMP_BLOCK = 32
CMP_STRIDE = 16
CMP_HIDDEN = 256
SLC_BLOCK = 64
SLC_TOPK = 16
NSA_WINDOW = 512
NSA_TQ = 256
NSA_ONES_ROWS = 16
NSA_BLOCKS_PER_STEP = 2
FORCE_BONUS = 1.0e4
MASK_NEG = -1.0e30

SWA_HEAD_DIM = 64
SWA_HEADS = D_MODEL // SWA_HEAD_DIM
SWA_KV_HEADS = 4
SWA_REP = SWA_HEADS // SWA_KV_HEADS
SWA_WINDOW = 128

LANES = 128
VMEM_LIMIT_BYTES = 48 * 1024 * 1024

_NT = (((1,), (1,)), ((), ()))
_TN = (((0,), (0,)), ((), ()))


def _params(*sem):
    return pltpu.CompilerParams(dimension_semantics=sem, vmem_limit_bytes=VMEM_LIMIT_BYTES)


def _silu(x):
    return x * (1.0 / (1.0 + jnp.exp(-x)))


def _mm_kernel(*refs, rope_half, rope_period, rope_tiles, head_major, n_chunks):
    if rope_half:
        x_ref, w_ref, c_ref, s_ref, o_ref, wb_ref = refs
    else:
        x_ref, w_ref, o_ref, wb_ref = refs

    @pl.when(pl.program_id(1) == 0)
    def _():
        wb_ref[...] = w_ref[...].astype(BF16)

    acc = jnp.dot(x_ref[...], wb_ref[...], preferred_element_type=F32)

    def write(with_rope):
        if with_rope:
            cos = c_ref[...]
            sin = s_ref[...]
            lane = lax.broadcasted_iota(jnp.int32, cos.shape, 1)
            first_half = (lane & (rope_period - 1)) < rope_half
        for j in range(n_chunks):
            a = acc[:, j * LANES:(j + 1) * LANES]
            if with_rope:
                up = pltpu.roll(a, LANES - rope_half, 1)
                dn = pltpu.roll(a, rope_half, 1)
                a = a * cos + jnp.where(first_half, up, dn) * sin
            if head_major:
                o_ref[j] = a.astype(o_ref.dtype)
            else:
                o_ref[:, j * LANES:(j + 1) * LANES] = a.astype(o_ref.dtype)

    if rope_half and rope_tiles is not None:
        pl.when(pl.program_id(0) < rope_tiles)(functools.partial(write, True))
        pl.when(pl.program_id(0) >= rope_tiles)(functools.partial(write, False))
    else:
        write(bool(rope_half))


def _col_tiles(w, cols, tn):
    return (0, 1, w.shape[1] // tn) if cols is None else cols


def _mm(x, w, *, out_dtype, cols=None, rope=None, rope_tiles=None, head_major=False, tn=512):
    S, K = x.shape
    tm = min(1024, S)
    first, stride, n_tiles = _col_tiles(w, cols, tn)
    N = n_tiles * tn
    n_chunks = tn // LANES
    in_specs = [pl.BlockSpec((tm, K), lambda j, i: (i, 0)),
                pl.BlockSpec((K, tn), lambda j, i: (0, first + stride * j))]
    args = [x, w]
    half = period = 0
    if rope is not None:
        cos_t, sin_t, half, period = rope
        in_specs += [pl.BlockSpec((tm, LANES), lambda j, i: (i, 0))] * 2
        args += [cos_t, sin_t]
    if head_major:
        out_shape = jax.ShapeDtypeStruct((N // LANES, S, LANES), out_dtype)
        out_spec = pl.BlockSpec((n_chunks, tm, LANES), lambda j, i: (j, i, 0))
    else:
        out_shape = jax.ShapeDtypeStruct((S, N), out_dtype)
        out_spec = pl.BlockSpec((tm, tn), lambda j, i: (i, j))
    return pl.pallas_call(
        functools.partial(_mm_kernel, rope_half=half, rope_period=period, rope_tiles=rope_tiles,
                          head_major=head_major, n_chunks=n_chunks),
        grid=(n_tiles, S // tm),
        in_specs=in_specs,
        out_specs=out_spec,
        out_shape=out_shape,
        scratch_shapes=[pltpu.VMEM((K, tn), BF16)],
        compiler_params=_params("arbitrary", "arbitrary"),
        name="in_proj",
    )(*args)


def _mm_t_kernel(*refs, rope_half, scale, n_heads, n_tiles, tk):
    if rope_half:
        w_ref, x_ref, c_ref, s_ref, o_ref, wt_ref = refs
    else:
        w_ref, x_ref, o_ref, wt_ref = refs

    @pl.when(pl.program_id(1) == 0)
    def _():
        wt_ref[...] = w_ref[...].T.astype(BF16)

    acc = lax.dot_general(wt_ref[...], x_ref[...], _NT, preferred_element_type=F32)
    for c in range(n_heads):
        a = acc[c * LANES:(c + 1) * LANES]
        if scale != 1.0:
            a = a * scale
        if rope_half:
            rot = 2 * rope_half
            swapped = jnp.concatenate([a[rope_half:rot], a[:rope_half]], axis=0)
            a = jnp.concatenate([a[:rot] * c_ref[...] + swapped * s_ref[...], a[rot:]], axis=0)
        for b in range(n_tiles):
            o_ref[c, b] = a[:, b * tk:(b + 1) * tk].astype(o_ref.dtype)


def _mm_t(x, w, *, out_dtype, tk, cols=None, rope=None, scale=1.0, tn=512):
    S, K = x.shape
    tm = min(1024, S)
    first, stride, n_tiles = _col_tiles(w, cols, tn)
    N = n_tiles * tn
    in_specs = [pl.BlockSpec((K, tn), lambda j, i: (0, first + stride * j)),
                pl.BlockSpec((tm, K), lambda j, i: (i, 0))]
    args = [w, x]
    half = 0
    if rope is not None:
        cos_t, sin_t, half = rope
        in_specs += [pl.BlockSpec((2 * half, tm), lambda j, i: (0, i))] * 2
        args += [cos_t, sin_t]
    return pl.pallas_call(
        functools.partial(_mm_t_kernel, rope_half=half, scale=scale, n_heads=tn // LANES,
                          n_tiles=tm // tk, tk=tk),
        grid=(n_tiles, S // tm),
        in_specs=in_specs,
        out_specs=pl.BlockSpec((tn // LANES, tm // tk, LANES, tk), lambda j, i: (j, i, 0, 0)),
        out_shape=jax.ShapeDtypeStruct((N // LANES, S // tk, LANES, tk), out_dtype),
        scratch_shapes=[pltpu.VMEM((tn, K), BF16)],
        compiler_params=_params("arbitrary", "arbitrary"),
        name="in_proj_t",
    )(*args)


def _outproj_ln_kernel(a_ref, w_ref, h_ref, g_ref, b_ref, o_ref, ob_ref):
    y = jnp.dot(a_ref[...], w_ref[...], preferred_element_type=F32)
    u = DEEPNORM_ALPHA * h_ref[...] + y
    mu = jnp.mean(u, axis=-1, keepdims=True)
    xc = u - mu
    var = jnp.mean(xc * xc, axis=-1, keepdims=True)
    out = xc * lax.rsqrt(var + LN_EPS) * g_ref[...] + b_ref[...]
    o_ref[...] = out
    ob_ref[...] = out.astype(BF16)


def _outproj_ln(a, w, h, g, b):
    S, D = h.shape
    tm = min(512, S)
    row = pl.BlockSpec((tm, D), lambda i: (i, 0))
    vec = pl.BlockSpec((1, D), lambda i: (0, 0))
    return pl.pallas_call(
        _outproj_ln_kernel,
        grid=(S // tm,),
        in_specs=[row, pl.BlockSpec((D, D), lambda i: (0, 0)), row, vec, vec],
        out_specs=[row, row],
        out_shape=[jax.ShapeDtypeStruct((S, D), F32), jax.ShapeDtypeStruct((S, D), BF16)],
        compiler_params=_params("parallel"),
        name="outproj_ln",
    )(a, w, h, g.reshape(1, D), b.reshape(1, D))


def _hgrn_kernel(x_ref, wq_ref, wf_ref, wv_ref, wz_ref, llb_ref, l1m_ref, gn_ref,
                 o_ref, st_ref, u_ref, wb_ref, *, n_pieces, piece):
    C, SUB, DK = HG_CHUNK, HG_SUB, HG_HEAD_DIM
    n_lvl = (C // (2 * SUB)).bit_length()
    n_pc = piece // C

    @pl.when(pl.program_id(1) == 0)
    def _():
        st_ref[...] = jnp.zeros_like(st_ref)
        for i, w_ref in enumerate((wq_ref, wf_ref, wv_ref, wz_ref)):
            wb_ref[:, i * DK:(i + 1) * DK] = w_ref[...].astype(BF16)

    r = lax.broadcasted_iota(jnp.int32, (C, C), 0)
    c = lax.broadcasted_iota(jnp.int32, (C, C), 1)
    causal = c <= r
    halves = [SUB << lvl for lvl in range(n_lvl)]
    cum_rows = []
    for h in halves:
        same_h = (r // h) == (c // h)
        cum_rows += [causal & same_h, same_h]
    cum_one = jnp.concatenate([m.astype(F32) for m in cum_rows], axis=0).astype(BF16)
    cum_mat = jnp.concatenate([cum_one] * 3, axis=1)
    diag_mask = causal & ((r // SUB) == (c // SUB))
    log_lb = llb_ref[...]
    log_1m_lb = l1m_ref[...]
    lane_c = lax.broadcasted_iota(jnp.int32, (SUB, C), 1)
    chunks = [slice(ci * C, (ci + 1) * C) for ci in range(n_pc)]

    def rows_to_lanes(x):
        return jnp.concatenate([x[sl] for sl in chunks], axis=1)

    def lanes_to_rows(x):
        return jnp.concatenate([x[:, ci * DK:(ci + 1) * DK] for ci in range(n_pc)], axis=0)

    def placed(x, lo):
        parts = [jnp.zeros((lo, DK), F32), x, jnp.zeros((C - lo - x.shape[0], DK), F32)]
        return jnp.concatenate([p for p in parts if p.shape[0]], axis=0)

    def front(pi):
        rows = slice(pi * piece, (pi + 1) * piece)
        proj = jnp.dot(x_ref[rows, :], wb_ref[...], preferred_element_type=F32)
        q, fl, v, z = (proj[:, i * DK:(i + 1) * DK] for i in range(4))
        vb = v.astype(BF16)
        log_sig = jnp.minimum(fl, 0.0) - jnp.log(1.0 + jnp.exp(-jnp.abs(fl)))
        t = log_1m_lb + log_sig
        log_f = jnp.maximum(log_lb, t) + jnp.log(1.0 + jnp.exp(-jnp.abs(log_lb - t)))
        log2_k = (t - fl) * LOG2_E
        k = jnp.exp2(log2_k)
        lf = rows_to_lanes(log_f)
        lf_hi = lf.astype(BF16)
        lf_r1 = lf - lf_hi.astype(F32)
        lf_mid = lf_r1.astype(BF16)
        lf_lo = (lf_r1 - lf_mid.astype(F32)).astype(BF16)
        cums = jnp.dot(cum_mat, jnp.concatenate([lf_hi, lf_mid, lf_lo], axis=0),
                       preferred_element_type=F32) * LOG2_E
        lvl = [cums[i * C:(i + 1) * C] for i in range(2 * n_lvl)]
        loc_top, tot_top = lvl[-2], lvl[-1]
        tot_chunk = tot_top[:C // 2] + tot_top[C // 2:]
        cum = [lanes_to_rows(a) for a in [
            jnp.concatenate([loc_top[:C // 2], loc_top[C // 2:] + tot_top[:C // 2]], axis=0),
            jnp.concatenate([tot_chunk, tot_chunk], axis=0)] + lvl]
        b_full, b_tot = cum[0], cum[1]
        qe = (q * jnp.exp2(b_full)).astype(BF16)
        kd = (k * jnp.exp2(b_tot - b_full)).astype(BF16)
        q_lvl = [q * jnp.exp2(cum[2 + 2 * l]) for l in range(n_lvl)]
        k_lvl = [k * jnp.exp2(cum[3 + 2 * l] - cum[2 + 2 * l]) for l in range(n_lvl)]
        b_loc = cum[2]
        c_row = b_loc - log2_k

        a_offs, a_diags = [], []
        for ci, ch in enumerate(chunks):
            blocks = []
            for j in range(C // SUB):
                base = ci * C + j * SUB
                sl = slice(base, base + SUB)
                blk = jnp.zeros((SUB, C), F32)
                for s in range(SUB):
                    y = q[sl] * jnp.exp2(jnp.minimum(
                        b_loc[sl] - c_row[base + s:base + s + 1, :], log2_k[base + s:base + s + 1, :]))
                    blk = jnp.where(lane_c == j * SUB + s, jnp.sum(y, axis=-1, keepdims=True), blk)
                blocks.append(blk)
            a_diags.append(jnp.concatenate(blocks, axis=0))
            lhs, rhs = [], []
            for l, h in enumerate(halves):
                q_c, k_c = q_lvl[l][ch], k_lvl[l][ch]
                for lo in range(0, C, 2 * h):
                    lhs.append(placed(q_c[lo + h:lo + 2 * h], lo + h))
                    rhs.append(placed(k_c[lo:lo + h], lo))
            a_offs.append(lax.dot_general(jnp.concatenate(lhs, axis=1).astype(BF16),
                                          jnp.concatenate(rhs, axis=1).astype(BF16),
                                          _NT, preferred_element_type=F32))
            u_ref[pi * n_pc + ci] = lax.dot_general(vb[ch], kd[ch], _TN,
                                                    preferred_element_type=F32)
        per_chunk = []
        for ci, ch in enumerate(chunks):
            a = jnp.where(diag_mask, a_diags[ci], 0.0) + a_offs[ci]
            o_intra = jnp.dot(a.astype(BF16), vb[ch], preferred_element_type=F32)
            per_chunk.append((qe[ch], o_intra, b_tot[ci * C:ci * C + 8]))
        return per_chunk, z

    fronts = [front(pi) for pi in range(n_pieces)]

    st = st_ref[...]
    outs = []
    for pi, (per_chunk, _) in enumerate(fronts):
        for ci, (qe_c, o_intra, b_tot8) in enumerate(per_chunk):
            outs.append(o_intra + lax.dot_general(qe_c, st.astype(BF16), _NT,
                                                  preferred_element_type=F32))
            st = st * jnp.exp2(jnp.tile(b_tot8, (DK // 8, 1))) + u_ref[pi * n_pc + ci]
    st_ref[...] = st

    o = jnp.concatenate(outs, axis=0)
    z = jnp.concatenate([z_p for _, z_p in fronts], axis=0)
    ms = jnp.mean(o * o, axis=-1, keepdims=True)
    out = o * lax.rsqrt(ms + RMS_EPS) * gn_ref[...] * _silu(z)
    o_ref[...] = out.astype(o_ref.dtype)


def _hgrn_mixer(hb, w_in, log_lb, log_1m_lb, g_norm):
    S, K = hb.shape
    D, DK, H = D_MODEL, HG_HEAD_DIM, HG_HEADS
    T = min(HG_TIME_BLOCK, S)
    piece = min(HG_PIECE, T)

    def w_col(off):
        return pl.BlockSpec((K, DK), lambda h, t, off=off: (0, off + h))

    vec = pl.BlockSpec((1, DK), lambda h, t: (0, h))
    return pl.pallas_call(
        functools.partial(_hgrn_kernel, n_pieces=T // piece, piece=piece),
        grid=(H, S // T),
        in_specs=[pl.BlockSpec((T, K), lambda h, t: (t, 0)),
                  w_col(0), w_col(H), w_col(2 * H), w_col(3 * H), vec, vec, vec],
        out_specs=pl.BlockSpec((T, DK), lambda h, t: (t, h)),
        out_shape=jax.ShapeDtypeStruct((S, D), BF16),
        scratch_shapes=[pltpu.VMEM((DK, DK), F32),
                        pltpu.VMEM((T // HG_CHUNK, DK, DK), F32),
                        pltpu.VMEM((K, 4 * DK), BF16)],
        compiler_params=_params("parallel", "arbitrary"),
        name="hgrn_mixer",
    )(hb, w_in, w_in, w_in, w_in, log_lb.reshape(1, D), log_1m_lb.reshape(1, D),
      g_norm.reshape(1, D))


def _nsa_compress_kernel(x_ref, pos_ref, w1_ref, w2_ref, o_ref, *, transposed):
    DH = NSA_HEAD_DIM
    n = x_ref.shape[0] // CMP_STRIDE
    h1 = jnp.zeros((n, CMP_HIDDEN), F32)
    h2 = jnp.zeros((n, CMP_HIDDEN), F32)
    for l in range(CMP_STRIDE):
        x_l = x_ref[pl.ds(l, n, stride=CMP_STRIDE), :]
        l2 = CMP_STRIDE + l
        h1 = h1 + jnp.dot((x_l + pos_ref[l:l + 1, :]).astype(BF16),
                          w1_ref[l * DH:(l + 1) * DH, :].astype(BF16), preferred_element_type=F32)
        h2 = h2 + jnp.dot((x_l + pos_ref[l2:l2 + 1, :]).astype(BF16),
                          w1_ref[l2 * DH:(l2 + 1) * DH, :].astype(BF16), preferred_element_type=F32)
    hid = h1 + pltpu.roll(h2, n - 1, 0)
    act = _silu(hid).astype(BF16)
    if transposed:
        o_ref[...] = lax.dot_general(w2_ref[...].T.astype(BF16), act, _NT,
                                     preferred_element_type=F32).astype(o_ref.dtype)
    else:
        o_ref[...] = jnp.dot(act, w2_ref[...].astype(BF16),
                             preferred_element_type=F32).astype(o_ref.dtype)


def _nsa_compress(x, first, pos, w1, w2, *, transposed):
    G = NSA_KV_GROUPS
    _, S, DH = x.shape
    n = S // CMP_STRIDE
    whole = lambda a: pl.BlockSpec(a.shape, lambda g: (0,) * a.ndim)
    out_block = (None, DH, n) if transposed else (None, n, DH)
    return pl.pallas_call(
        functools.partial(_nsa_compress_kernel, transposed=transposed),
        grid=(G,),
        in_specs=[pl.BlockSpec((None, S, DH), lambda g: (first + g, 0, 0)),
                  whole(pos), whole(w1), whole(w2)],
        out_specs=pl.BlockSpec(out_block, lambda g: (g, 0, 0)),
        out_shape=jax.ShapeDtypeStruct((G,) + out_block[1:], BF16),
        compiler_params=_params("parallel"),
        name="nsa_compress",
    )(x, pos, w1, w2)


def _nsa_kernel(x0_ref, cos0_ref, sin0_ref, x1_ref, cos1_ref, sin1_ref, wq_ref, kc_ref, vct_ref,
                ks_ref, vst_ref, kw_ref, vwt_ref, oh_ref, aggt_ref, gate_ref, z_ref, o_ref,
                m_ref, acc_ref, s_ref, wqt_ref, q_ref, *, tq, n_sub, q_scale):
    R, DH = NSA_REP, NSA_HEAD_DIM
    ncp = kc_ref.shape[1]
    rot = cos0_ref.shape[0]

    def project_q(xr, cr, sr, sub):
        t_sl = slice(sub * tq, (sub + 1) * tq)
        q_t = lax.dot_general(wqt_ref[...], xr[t_sl, :], _NT,
                              preferred_element_type=F32) * q_scale
        cos, sin = cr[:, t_sl], sr[:, t_sl]
        heads = []
        for r in range(R):
            a = q_t[r * DH:(r + 1) * DH]
            swapped = jnp.concatenate([a[rot // 2:rot], a[:rot // 2]], axis=0)
            heads.append(jnp.concatenate([a[:rot] * cos + swapped * sin, a[rot:]],
                                         axis=0).astype(BF16))
        return jnp.concatenate(heads, axis=1)

    @pl.when(pl.program_id(1) == 0)
    def _():
        wqt_ref[...] = wq_ref[...].T.astype(BF16)
        for sub in range(n_sub):
            q_ref[sub] = project_q(x0_ref, cos0_ref, sin0_ref, sub)

    def per_head(x):
        return jnp.concatenate([x] * R, axis=1)

    def rows8(x, n):
        return jnp.tile(x, (n // 8, 1))

    def col_max8(x):
        mx = x[0:8]
        for i in range(1, x.shape[0] // 8):
            mx = jnp.maximum(mx, x[8 * i:8 * i + 8])
        for shift in (4, 2, 1):
            mx = jnp.maximum(mx, pltpu.roll(mx, shift, 0))
        return mx

    def key_rows(ref, kt):
        return ref[0, pl.ds(pl.multiple_of(kt * tq, tq), tq), :]

    c_k = lax.broadcasted_iota(jnp.int32, (tq, tq), 0)
    i_q = lax.broadcasted_iota(jnp.int32, (tq, tq), 1)
    causal = per_head(c_k <= i_q)
    aggt = aggt_ref[...]

    def front(sub):
        qb = pl.program_id(1) * n_sub + sub
        start = qb * tq
        q = q_ref[sub]

        n_c = lax.broadcasted_iota(jnp.int32, (ncp, tq), 0)
        t_c = start + lax.broadcasted_iota(jnp.int32, (ncp, tq), 1)
        cmask = per_head((n_c * CMP_STRIDE + CMP_BLOCK - 1) <= t_c)
        s = jnp.where(cmask, jnp.dot(kc_ref[0], q, preferred_element_type=F32), -jnp.inf)
        m = jnp.max(s, axis=0, keepdims=True)
        m = jnp.where(m == -jnp.inf, 0.0, m)
        e = jnp.exp2(s - m)
        p = e * (1.0 / jnp.maximum(jnp.sum(e, axis=0, keepdims=True), 1e-30))
        o_cmp = jnp.dot(vct_ref[0], p.astype(BF16), preferred_element_type=F32)
        p_sum = p[:, :tq]
        for r in range(1, R):
            p_sum = p_sum + p[:, r * tq:(r + 1) * tq]

        no_old = jnp.where(qb >= 2, 0, NSA_WINDOW)
        no_mid = jnp.where(qb >= 1, 0, NSA_WINDOW)
        old_ok = per_head((2 * tq + i_q - c_k + no_old) < NSA_WINDOW)
        mid_ok = per_head((tq + i_q - c_k + no_mid) < NSA_WINDOW)
        kt_old, kt_mid = jnp.maximum(qb - 2, 0), jnp.maximum(qb - 1, 0)
        k_win = jnp.concatenate([key_rows(kw_ref, kt_old), key_rows(kw_ref, kt_mid),
                                 key_rows(kw_ref, qb)], axis=0)
        s_w = jnp.where(jnp.concatenate([old_ok, mid_ok, causal], axis=0),
                        jnp.dot(k_win, q, preferred_element_type=F32), MASK_NEG)
        p_w = jnp.exp2(s_w - rows8(col_max8(s_w), 3 * tq)).astype(BF16)
        v_win = jnp.concatenate([vwt_ref[0, kt_old], vwt_ref[0, kt_mid], vwt_ref[0, qb]], axis=1)
        acc_w = jnp.dot(jnp.concatenate([v_win, jnp.ones((NSA_ONES_ROWS, 3 * tq), BF16)], axis=0),
                        p_w, preferred_element_type=F32)
        o_win = acc_w[:DH] * rows8(1.0 / acc_w[DH:DH + 8], DH)

        p_hi = p_sum.astype(BF16)
        p_lo = (p_sum - p_hi.astype(F32)).astype(BF16)
        imp = (jnp.dot(aggt, p_hi, preferred_element_type=F32)
               + jnp.dot(aggt, p_lo, preferred_element_type=F32))
        j_s = lax.broadcasted_iota(jnp.int32, (LANES, tq), 0)
        t_s = start + lax.broadcasted_iota(jnp.int32, (LANES, tq), 1)
        cur = lax.shift_right_logical(t_s, 6)
        forced = (j_s == 0) | (j_s == cur) | (j_s == cur - 1)
        allowed = j_s * SLC_BLOCK <= t_s
        sel = jnp.where(forced & allowed, 1.0, 0.0)
        val = jnp.where(allowed & ~forced, imp, -jnp.inf)
        row_f = j_s.astype(F32)
        for _ in range(SLC_TOPK - 3):
            mx = jnp.max(val, axis=0, keepdims=True)
            idx = jnp.min(jnp.where(val == mx, row_f, float(LANES)), axis=0, keepdims=True)
            pick = row_f == idx
            sel = jnp.where(pick, 1.0, sel)
            val = jnp.where(pick, -jnp.inf, val)
        m_neg = jnp.where(sel > 0.0, 0.0, MASK_NEG).astype(BF16)
        q_aug = jnp.concatenate([q, per_head(m_neg)], axis=0)
        return qb, q_aug, o_cmp, o_win

    ones_rows = jnp.ones((NSA_ONES_ROWS, tq), BF16)

    def flash_init():
        m_ref[...] = jnp.full(m_ref.shape, -jnp.inf, F32)
        acc_ref[...] = jnp.zeros_like(acc_ref)

    def flash_update(s, vt_tile):
        m_prev = m_ref[...]
        m_next = jnp.maximum(m_prev, col_max8(s))
        alpha = jnp.exp2(m_prev - m_next)
        p = jnp.exp2(s - rows8(m_next, tq)).astype(BF16)
        v_aug = jnp.concatenate([vt_tile, ones_rows], axis=0)
        acc_ref[...] = (acc_ref[...] * rows8(alpha, DH + NSA_ONES_ROWS)
                        + jnp.dot(v_aug, p, preferred_element_type=F32))
        m_ref[...] = m_next

    def flash_result():
        acc = acc_ref[...]
        return acc[:DH] * rows8(1.0 / acc[DH:DH + 8], DH)

    def back(sub, qb, q_aug, o_cmp, o_win):
        def slc_scores(kt):
            oh = oh_ref[pl.ds(pl.multiple_of(kt * tq, tq), tq), :]
            k_aug = jnp.concatenate([key_rows(ks_ref, kt), oh], axis=1)
            return jnp.dot(k_aug, q_aug, preferred_element_type=F32)

        flash_init()
        s_ref[...] = slc_scores(0)

        def slc_step(kt):
            s_cur = s_ref[...]
            s_ref[...] = slc_scores(kt + 1)
            flash_update(s_cur, vst_ref[0, kt])

        def slc_quad(i, carry):
            for u in range(4):
                slc_step(4 * i + u)
            return carry

        lax.fori_loop(0, lax.shift_right_logical(qb, 2), slc_quad, 0)
        done = qb & ~3

        @pl.when((qb & 2) != 0)
        def _():
            slc_step(done)
            slc_step(done + 1)

        @pl.when((qb & 1) != 0)
        def _():
            slc_step(qb - 1)

        flash_update(jnp.where(causal, s_ref[...], MASK_NEG), vst_ref[0, qb])
        o_slc = flash_result()

        head0 = pl.program_id(0) * R

        def gate(branch, r):
            logit = gate_ref[0, sub, pl.ds(branch * NSA_HEADS + head0 + r, 1), :]
            return jnp.broadcast_to(1.0 / (1.0 + jnp.exp(-logit)), (DH, tq))

        rows = slice(sub * tq, (sub + 1) * tq)
        for r in range(R):
            sl = slice(r * tq, (r + 1) * tq)
            g = [gate(b, r) for b in range(3)]
            o_t = g[0] * o_cmp[:, sl] + g[1] * o_slc[:, sl] + g[2] * o_win[:, sl]
            z = z_ref[rows, r * DH:(r + 1) * DH]
            o_ref[rows, r * DH:(r + 1) * DH] = (o_t.T * _silu(z)).astype(o_ref.dtype)

    fronts = [front(sub) for sub in range(n_sub)]
    for sub in range(n_sub):
        q_ref[sub] = project_q(x1_ref, cos1_ref, sin1_ref, sub)
    for sub in range(n_sub):
        back(sub, *fronts[sub])


def _nsa_attn(hb, w_in, rope_t, kc, vc_t, ksw, vsw_t, onehot, agg_t, gate_t, z):
    G, R, DH = NSA_KV_GROUPS, NSA_REP, NSA_HEAD_DIM
    S = ksw.shape[1]
    tq, n_sub = NSA_TQ, NSA_BLOCKS_PER_STEP
    assert S % (n_sub * tq) == 0 and NSA_WINDOW <= 2 * tq and S // SLC_BLOCK <= LANES
    ncp = kc.shape[1]
    n_t = S // tq
    K = hb.shape[1]
    cos_t, sin_t = rope_t
    rot = cos_t.shape[0]
    step = n_sub * tq
    nxt = lambda i: jnp.minimum(i + 1, S // step - 1)
    rows = lambda off: pl.BlockSpec((1, S, DH), lambda g, i, off=off: (off + g, 0, 0))
    tiles_t = lambda off: pl.BlockSpec((1, n_t, DH, tq), lambda g, i, off=off: (off + g, 0, 0, 0))
    return pl.pallas_call(
        functools.partial(_nsa_kernel, tq=tq, n_sub=n_sub, q_scale=DH ** -0.5 * LOG2_E),
        grid=(G, n_t // n_sub),
        in_specs=[pl.BlockSpec((step, K), lambda g, i: (0, 0)),
                  pl.BlockSpec((rot, step), lambda g, i: (0, 0)),
                  pl.BlockSpec((rot, step), lambda g, i: (0, 0)),
                  pl.BlockSpec((step, K), lambda g, i: (nxt(i), 0)),
                  pl.BlockSpec((rot, step), lambda g, i: (0, nxt(i))),
                  pl.BlockSpec((rot, step), lambda g, i: (0, nxt(i))),
                  pl.BlockSpec((K, R * DH), lambda g, i: (0, g)),
                  pl.BlockSpec((1, ncp, DH), lambda g, i: (g, 0, 0)),
                  pl.BlockSpec((1, DH, ncp), lambda g, i: (g, 0, 0)),
                  rows(0), tiles_t(0), rows(G), tiles_t(G),
                  pl.BlockSpec((S, LANES), lambda g, i: (0, 0)),
                  pl.BlockSpec((LANES, ncp), lambda g, i: (0, 0)),
                  pl.BlockSpec((1, n_sub, LANES, tq), lambda g, i: (0, i, 0, 0)),
                  pl.BlockSpec((n_sub * tq, R * DH), lambda g, i: (i, g))],
        out_specs=pl.BlockSpec((n_sub * tq, R * DH), lambda g, i: (i, g)),
        out_shape=jax.ShapeDtypeStruct((S, D_MODEL), BF16),
        scratch_shapes=[pltpu.VMEM((8, R * tq), F32),
                        pltpu.VMEM((DH + NSA_ONES_ROWS, R * tq), F32),
                        pltpu.VMEM((tq, R * tq), F32),
                        pltpu.VMEM((R * DH, K), BF16),
                        pltpu.VMEM((n_sub, DH, R * tq), BF16)],
        compiler_params=_params("parallel", "arbitrary"),
        name="nsa_attn",
    )(hb, cos_t, sin_t, hb, cos_t, sin_t, w_in, kc, vc_t, ksw, vsw_t, ksw, vsw_t, onehot, agg_t,
      gate_t, z)


def _swa_kernel(sink_ref, q_ref, kp_ref, kc_ref, vp_ref, vc_ref, z_ref, o_ref):
    W, R = SWA_WINDOW, SWA_REP
    n = pl.program_id(0)
    i_q = lax.broadcasted_iota(jnp.int32, (W, 2 * W), 0)
    c_k = lax.broadcasted_iota(jnp.int32, (W, 2 * W), 1)
    diff = i_q - (c_k - W)
    mask = (diff >= 0) & (diff < W) & ((n > 0) | (c_k >= W))
    lane = lax.broadcasted_iota(jnp.int32, (W, LANES), 1)
    low = lane < SWA_HEAD_DIM
    ones_v = jnp.ones((2 * W, LANES), BF16)
    for g in range(SWA_KV_HEADS):
        gl = slice(g * LANES, (g + 1) * LANES)
        kk = jnp.concatenate([kp_ref[:, gl], kc_ref[:, gl]], axis=0)
        vv = jnp.concatenate([vp_ref[:, gl], vc_ref[:, gl]], axis=0)
        qs = []
        for r in range(R):
            h = g * R + r
            q2 = q_ref[:, (h // 2) * LANES:(h // 2 + 1) * LANES]
            qs.append(jnp.where(low if h % 2 == 0 else ~low, q2, jnp.zeros_like(q2)))
        s_all = lax.dot_general(jnp.concatenate(qs, axis=0), kk, _NT,
                                preferred_element_type=F32)
        ps, sink_p = [], []
        for r in range(R):
            sink = sink_ref[g * R + r] * LOG2_E
            s = jnp.where(mask, s_all[r * W:(r + 1) * W], -jnp.inf)
            m = jnp.maximum(jnp.max(s, axis=1, keepdims=True), sink)
            ps.append(jnp.exp2(s - m).astype(BF16))
            sink_p.append(jnp.exp2(sink - m))
        o_all = jnp.dot(jnp.concatenate(ps, axis=0), jnp.concatenate([vv, ones_v], axis=1),
                        preferred_element_type=F32)

        def head_out(r):
            rows = slice(r * W, (r + 1) * W)
            return o_all[rows, :LANES] * (1.0 / (o_all[rows, LANES:] + sink_p[r]))

        for pr in range(R // 2):
            o_even, o_odd = head_out(2 * pr), head_out(2 * pr + 1)
            col = slice((g * R // 2 + pr) * LANES, (g * R // 2 + pr + 1) * LANES)
            o_ref[:, col] = (jnp.where(low, o_even, o_odd) * _silu(z_ref[:, col])).astype(o_ref.dtype)


def _swa_attn(sinks, q, kv_dup, z):
    S, D = q.shape
    W = SWA_WINDOW
    wide = SWA_KV_HEADS * LANES
    cur = lambda i: (i, 0)
    k_prev, k_cur = (lambda i: (jnp.maximum(i - 1, 0), 0)), cur
    v_prev, v_cur = (lambda i: (jnp.maximum(i - 1, 0), 1)), (lambda i: (i, 1))
    return pl.pallas_call(
        _swa_kernel,
        grid=(S // W,),
        in_specs=[pl.BlockSpec(memory_space=pltpu.SMEM),
                  pl.BlockSpec((W, D), cur),
                  pl.BlockSpec((W, wide), k_prev), pl.BlockSpec((W, wide), k_cur),
                  pl.BlockSpec((W, wide), v_prev), pl.BlockSpec((W, wide), v_cur),
                  pl.BlockSpec((W, D), cur)],
        out_specs=pl.BlockSpec((W, D), cur),
        out_shape=jax.ShapeDtypeStruct((S, D), BF16),
        compiler_params=_params("parallel"),
        name="swa_attn",
    )(sinks, q, kv_dup, kv_dup, kv_dup, kv_dup, z)


def _rope_tables(positions, head_dim, scale):
    rot = head_dim // ROPE_FRACTION
    half = rot // 2
    inv_freq = ROPE_THETA ** (-jnp.arange(0, rot, 2, dtype=F32) / rot)
    ang = positions.reshape(-1).astype(F32)[:, None] * inv_freq
    cos, sin = jnp.cos(ang), jnp.sin(ang)
    S = cos.shape[0]
    rest = head_dim - rot
    cos_h = jnp.concatenate([cos, cos, jnp.ones((S, rest), F32)], axis=1)
    sin_h = jnp.concatenate([-sin, sin, jnp.zeros((S, rest), F32)], axis=1)
    reps = LANES // head_dim
    return (jnp.tile(cos_h, (1, reps)) * scale, jnp.tile(sin_h, (1, reps)) * scale, half, head_dim)


def _nsa_constants(S):
    n_cmp = S // CMP_STRIDE - 1
    ncp = S // CMP_STRIDE
    n_slc = S // SLC_BLOCK
    ratio = SLC_BLOCK // CMP_STRIDE
    i = np.arange(ncp)[:, None]
    j = np.arange(LANES)[None, :]
    agg = ((i >= ratio * j - CMP_BLOCK // CMP_STRIDE + 1) & (i <= ratio * j + ratio - 1)
           & (i < n_cmp) & (j < n_slc))
    onehot = (np.arange(S)[:, None] // SLC_BLOCK) == j
    return jnp.asarray(onehot, dtype=BF16), jnp.asarray(agg.T, dtype=BF16)


def _rope_tables_t(positions, head_dim):
    rot = head_dim // ROPE_FRACTION
    inv_freq = ROPE_THETA ** (-jnp.arange(0, rot, 2, dtype=F32) / rot)
    ang = inv_freq[:, None] * positions.reshape(-1).astype(F32)[None, :]
    cos, sin = jnp.cos(ang), jnp.sin(ang)
    return jnp.concatenate([cos, cos], axis=0), jnp.concatenate([-sin, sin], axis=0), rot // 2


def _hgrn_layer(hb, w_in, g_norm, log_lb, log_1m_lb):
    return _hgrn_mixer(hb, w_in, log_lb, log_1m_lb, g_norm)


def _nsa_layer(hb, positions, w_in, pos_k, w1_k, w2_k, pos_v, w1_v, w2_v):
    S = hb.shape[0]
    H, G, R, DH = NSA_HEADS, NSA_KV_GROUPS, NSA_REP, NSA_HEAD_DIM
    gw = G * DH
    assert gw == 512 and (H * DH) % gw == 0
    t_kc = H * DH // gw
    o_gate = H * DH + 6 * gw
    o_z = o_gate + 3 * H
    rope_k = _rope_tables(positions, DH, 1.0)
    tq = NSA_TQ

    wide = 2 * gw
    cos_t, sin_t, _ = _rope_tables_t(positions, DH)
    kv_c = _mm(hb, w_in, cols=(t_kc, 1, 2), out_dtype=F32, rope=rope_k, rope_tiles=1,
               head_major=True)
    ksw = _mm(hb, w_in, cols=(t_kc + 2, 2, 2), out_dtype=BF16, rope=rope_k, head_major=True)
    vsw_t = _mm_t(hb, w_in, cols=(t_kc + 3, 2, 2), out_dtype=BF16, tk=tq)
    z = _mm(hb, w_in[:, o_z:], out_dtype=F32, tn=wide)
    w_gate = jnp.pad(w_in[:, o_gate:o_z], ((0, 0), (0, LANES - 3 * H)))
    gate_t = _mm_t(hb, w_gate, out_dtype=F32, tk=tq, tn=LANES)

    kc = _nsa_compress(kv_c, 0, pos_k, w1_k, w2_k, transposed=False)
    vc_t = _nsa_compress(kv_c, G, pos_v, w1_v, w2_v, transposed=True)
    onehot, agg_t = _nsa_constants(S)
    return _nsa_attn(hb, w_in, (cos_t, sin_t), kc, vc_t, ksw, vsw_t, onehot, agg_t, gate_t, z)


def _swa_layer(hb, positions, w_in, sinks):
    H, KV, DH = SWA_HEADS, SWA_KV_HEADS, SWA_HEAD_DIM
    o_k = H * DH
    o_v = o_k + KV * DH
    o_z = o_v + KV * DH
    tn = 512
    assert o_k % tn == 0 and o_z % tn == 0
    dup = np.concatenate([np.tile(np.arange(g * DH, (g + 1) * DH), 2) for g in range(KV)])
    rope_q = _rope_tables(positions, DH, DH ** -0.5 * LOG2_E)
    rope_k = _rope_tables(positions, DH, 1.0)
    q = _mm(hb, w_in, cols=(0, 1, o_k // (2 * tn)), out_dtype=BF16, rope=rope_q, tn=2 * tn)
    w_kv = jnp.concatenate([w_in[:, o_k:o_v][:, dup], w_in[:, o_v:o_z][:, dup]], axis=1)
    kv_dup = _mm(hb, w_kv, out_dtype=BF16, rope=rope_k, rope_tiles=1, tn=tn)
    z = _mm(hb, w_in, cols=(o_z // tn, 1, o_k // tn), out_dtype=F32, tn=tn)
    return _swa_attn(sinks, q, kv_dup, z)


def kernel(x, positions, hgrn_lb_logits, l0_w_in, l0_g_norm, l0_w_out, l0_ln_g, l0_ln_b, l1_w_in, l1_cmp_pos_k, l1_cmp_w1_k, l1_cmp_w2_k, l1_cmp_pos_v, l1_cmp_w1_v, l1_cmp_w2_v, l1_w_out, l1_ln_g, l1_ln_b, l2_w_in, l2_sinks, l2_w_out, l2_ln_g, l2_ln_b, l3_w_in, l3_g_norm, l3_w_out, l3_ln_g, l3_ln_b):
    B, S, D = x.shape
    assert B == 1 and D == D_MODEL
    lb = jnp.cumsum(jax.nn.softmax(hgrn_lb_logits.astype(F32), axis=0), axis=0)
    lb = lb - lb[0:1]
    log_lb, log_1m_lb = jnp.log(lb), jnp.log1p(-lb)

    h = x.reshape(S, D)
    hb = h.astype(BF16)

    a = _hgrn_layer(hb, l0_w_in, l0_g_norm, log_lb[0], log_1m_lb[0])
    h, hb = _outproj_ln(a, l0_w_out.astype(BF16), h, l0_ln_g, l0_ln_b)

    a = _nsa_layer(hb, positions, l1_w_in, l1_cmp_pos_k, l1_cmp_w1_k, l1_cmp_w2_k,
                   l1_cmp_pos_v, l1_cmp_w1_v, l1_cmp_w2_v)
    h, hb = _outproj_ln(a, l1_w_out.astype(BF16), h, l1_ln_g, l1_ln_b)

    a = _swa_layer(hb, positions, l2_w_in, l2_sinks)
    h, hb = _outproj_ln(a, l2_w_out.astype(BF16), h, l2_ln_g, l2_ln_b)

    a = _hgrn_layer(hb, l3_w_in, l3_g_norm, log_lb[1], log_1m_lb[1])
    h, hb = _outproj_ln(a, l3_w_out.astype(BF16), h, l3_ln_g, l3_ln_b)
    return h.reshape(B, S, D)
```

```python
import functools

import numpy as np
import jax
import jax.numpy as jnp
from jax import lax
from jax.experimental import pallas as pl
from jax.experimental.pallas import tpu as pltpu

F32 = jnp.float32
BF16 = jnp.bfloat16

D_MODEL = 2048
DEPTH = 4
DEEPNORM_ALPHA = (2 * DEPTH) ** 0.25
LN_EPS = 1e-5
RMS_EPS = 1e-6
ROPE_THETA = 500000.0
ROPE_FRACTION = 4

HG_HEAD_DIM = 128
HG_HEADS = D_MODEL // HG_HEAD_DIM
HG_CHUNK = 64
HG_SUB = 8
HG_TIME_BLOCK = 2048
HG_PIECE = 512
LOG2_E = 1.4426950408889634

NSA_HEAD_DIM = 128
NSA_HEADS = D_MODEL // NSA_HEAD_DIM
NSA_KV_GROUPS = 4
NSA_REP = NSA_HEADS // NSA_KV_GROUPS
CMP_BLOCK = 32
CMP_STRIDE = 16
CMP_HIDDEN = 256
SLC_BLOCK = 64
SLC_TOPK = 16
NSA_WINDOW = 512
NSA_TQ = 256
NSA_ONES_ROWS = 16
NSA_BLOCKS_PER_STEP = 2
FORCE_BONUS = 1.0e4
MASK_NEG = -1.0e30

SWA_HEAD_DIM = 64
SWA_HEADS = D_MODEL // SWA_HEAD_DIM
SWA_KV_HEADS = 4
SWA_REP = SWA_HEADS // SWA_KV_HEADS
SWA_WINDOW = 128

LANES = 128
VMEM_LIMIT_BYTES = 48 * 1024 * 1024

_NT = (((1,), (1,)), ((), ()))
_TN = (((0,), (0,)), ((), ()))


def _params(*sem):
    return pltpu.CompilerParams(dimension_semantics=sem, vmem_limit_bytes=VMEM_LIMIT_BYTES)


def _silu(x):
    return x * (1.0 / (1.0 + jnp.exp(-x)))


def _mm_kernel(*refs, rope_half, rope_period, rope_tiles, head_major, n_chunks):
    if rope_half:
        x_ref, w_ref, c_ref, s_ref, o_ref, wb_ref = refs
    else:
        x_ref, w_ref, o_ref, wb_ref = refs

    @pl.when(pl.program_id(1) == 0)
    def _():
        wb_ref[...] = w_ref[...].astype(BF16)

    acc = jnp.dot(x_ref[...], wb_ref[...], preferred_element_type=F32)

    def write(with_rope):
        if with_rope:
            cos = c_ref[...]
            sin = s_ref[...]
            lane = lax.broadcasted_iota(jnp.int32, cos.shape, 1)
            first_half = (lane & (rope_period - 1)) < rope_half
        for j in range(n_chunks):
            a = acc[:, j * LANES:(j + 1) * LANES]
            if with_rope:
                up = pltpu.roll(a, LANES - rope_half, 1)
                dn = pltpu.roll(a, rope_half, 1)
                a = a * cos + jnp.where(first_half, up, dn) * sin
            if head_major:
                o_ref[j] = a.astype(o_ref.dtype)
            else:
                o_ref[:, j * LANES:(j + 1) * LANES] = a.astype(o_ref.dtype)

    if rope_half and rope_tiles is not None:
        pl.when(pl.program_id(0) < rope_tiles)(functools.partial(write, True))
        pl.when(pl.program_id(0) >= rope_tiles)(functools.partial(write, False))
    else:
        write(bool(rope_half))


def _col_tiles(w, cols, tn):
    return (0, 1, w.shape[1] // tn) if cols is None else cols


def _mm(x, w, *, out_dtype, cols=None, rope=None, rope_tiles=None, head_major=False, tn=512):
    S, K = x.shape
    tm = min(1024, S)
    first, stride, n_tiles = _col_tiles(w, cols, tn)
    N = n_tiles * tn
    n_chunks = tn // LANES
    in_specs = [pl.BlockSpec((tm, K), lambda j, i: (i, 0)),
                pl.BlockSpec((K, tn), lambda j, i: (0, first + stride * j))]
    args = [x, w]
    half = period = 0
    if rope is not None:
        cos_t, sin_t, half, period = rope
        in_specs += [pl.BlockSpec((tm, LANES), lambda j, i: (i, 0))] * 2
        args += [cos_t, sin_t]
    if head_major:
        out_shape = jax.ShapeDtypeStruct((N // LANES, S, LANES), out_dtype)
        out_spec = pl.BlockSpec((n_chunks, tm, LANES), lambda j, i: (j, i, 0))
    else:
        out_shape = jax.ShapeDtypeStruct((S, N), out_dtype)
        out_spec = pl.BlockSpec((tm, tn), lambda j, i: (i, j))
    return pl.pallas_call(
        functools.partial(_mm_kernel, rope_half=half, rope_period=period, rope_tiles=rope_tiles,
                          head_major=head_major, n_chunks=n_chunks),
        grid=(n_tiles, S // tm),
        in_specs=in_specs,
        out_specs=out_spec,
        out_shape=out_shape,
        scratch_shapes=[pltpu.VMEM((K, tn), BF16)],
        compiler_params=_params("arbitrary", "arbitrary"),
        name="in_proj",
    )(*args)


def _mm_t_kernel(w_ref, x_ref, o_ref, wt_ref, *, n_heads, n_tiles, tk):
    @pl.when(pl.program_id(1) == 0)
    def _():
        wt_ref[...] = w_ref[...].T.astype(BF16)

    acc = lax.dot_general(wt_ref[...], x_ref[...], _NT, preferred_element_type=F32)
    for c in range(n_heads):
        for b in range(n_tiles):
            o_ref[c, b] = acc[c * LANES:(c + 1) * LANES, b * tk:(b + 1) * tk].astype(o_ref.dtype)


def _mm_t(x, w, *, out_dtype, tk, cols=None, tn=512):
    S, K = x.shape
    tm = min(1024, S)
    first, stride, n_tiles = _col_tiles(w, cols, tn)
    N = n_tiles * tn
    return pl.pallas_call(
        functools.partial(_mm_t_kernel, n_heads=tn // LANES, n_tiles=tm // tk, tk=tk),
        grid=(n_tiles, S // tm),
        in_specs=[pl.BlockSpec((K, tn), lambda j, i: (0, first + stride * j)),
                  pl.BlockSpec((tm, K), lambda j, i: (i, 0))],
        out_specs=pl.BlockSpec((tn // LANES, tm // tk, LANES, tk), lambda j, i: (j, i, 0, 0)),
        out_shape=jax.ShapeDtypeStruct((N // LANES, S // tk, LANES, tk), out_dtype),
        scratch_shapes=[pltpu.VMEM((tn, K), BF16)],
        compiler_params=_params("arbitrary", "arbitrary"),
        name="in_proj_t",
    )(w, x)


def _outproj_ln_kernel(a_ref, w_ref, h_ref, g_ref, b_ref, o_ref, ob_ref):
    y = jnp.dot(a_ref[...], w_ref[...], preferred_element_type=F32)
    u = DEEPNORM_ALPHA * h_ref[...] + y
    mu = jnp.mean(u, axis=-1, keepdims=True)
    xc = u - mu
    var = jnp.mean(xc * xc, axis=-1, keepdims=True)
    out = xc * lax.rsqrt(var + LN_EPS) * g_ref[...] + b_ref[...]
    o_ref[...] = out
    ob_ref[...] = out.astype(BF16)


def _outproj_ln(a, w, h, g, b):
    S, D = h.shape
    tm = min(512, S)
    row = pl.BlockSpec((tm, D), lambda i: (i, 0))
    vec = pl.BlockSpec((1, D), lambda i: (0, 0))
    return pl.pallas_call(
        _outproj_ln_kernel,
        grid=(S // tm,),
        in_specs=[row, pl.BlockSpec((D, D), lambda i: (0, 0)), row, vec, vec],
        out_specs=[row, row],
        out_shape=[jax.ShapeDtypeStruct((S, D), F32), jax.ShapeDtypeStruct((S, D), BF16)],
        compiler_params=_params("parallel"),
        name="outproj_ln",
    )(a, w, h, g.reshape(1, D), b.reshape(1, D))


def _hgrn_kernel(x_ref, xn_ref, wq_ref, wf_ref, wv_ref, wz_ref, llb_ref, l1m_ref, gn_ref,
                 o_ref, st_ref, u_ref, wb_ref, p0_ref, *, n_pieces, piece):
    C, SUB, DK = HG_CHUNK, HG_SUB, HG_HEAD_DIM
    n_lvl = (C // (2 * SUB)).bit_length()
    n_pc = piece // C

    @pl.when(pl.program_id(1) == 0)
    def _():
        st_ref[...] = jnp.zeros_like(st_ref)
        for i, w_ref in enumerate((wq_ref, wf_ref, wv_ref, wz_ref)):
            wb_ref[:, i * DK:(i + 1) * DK] = w_ref[...].astype(BF16)
        p0_ref[...] = jnp.dot(x_ref[:piece, :], wb_ref[...], preferred_element_type=F32)

    r = lax.broadcasted_iota(jnp.int32, (C, C), 0)
    c = lax.broadcasted_iota(jnp.int32, (C, C), 1)
    causal = c <= r
    halves = [SUB << lvl for lvl in range(n_lvl)]
    cum_rows = []
    for h in halves:
        same_h = (r // h) == (c // h)
        cum_rows += [causal & same_h, same_h]
    cum_one = jnp.concatenate([m.astype(F32) for m in cum_rows], axis=0).astype(BF16)
    cum_mat = jnp.concatenate([cum_one] * 3, axis=1)
    diag_mask = causal & ((r // SUB) == (c // SUB))
    lvl_masks = [((r // (2 * h)) == (c // (2 * h))) & ((r % (2 * h)) >= h) & ((c % (2 * h)) < h)
                 for h in halves]
    log_lb = llb_ref[...]
    log_1m_lb = l1m_ref[...]
    lane_c = lax.broadcasted_iota(jnp.int32, (SUB, C), 1)
    chunks = [slice(ci * C, (ci + 1) * C) for ci in range(n_pc)]

    def rows_to_lanes(x):
        return jnp.concatenate([x[sl] for sl in chunks], axis=1)

    def lanes_to_rows(x):
        return jnp.concatenate([x[:, ci * DK:(ci + 1) * DK] for ci in range(n_pc)], axis=0)

    def front(pi):
        rows = slice(pi * piece, (pi + 1) * piece)
        proj = p0_ref[...] if pi == 0 else jnp.dot(
            x_ref[rows, :], wb_ref[...], preferred_element_type=F32)
        q, fl, v, z = (proj[:, i * DK:(i + 1) * DK] for i in range(4))
        vb = v.astype(BF16)
        log_sig = jnp.minimum(fl, 0.0) - jnp.log(1.0 + jnp.exp(-jnp.abs(fl)))
        t = log_1m_lb + log_sig
        log_f = jnp.maximum(log_lb, t) + jnp.log(1.0 + jnp.exp(-jnp.abs(log_lb - t)))
        log2_k = (t - fl) * LOG2_E
        k = jnp.exp2(log2_k)
        lf = rows_to_lanes(log_f)
        lf_hi = lf.astype(BF16)
        lf_r1 = lf - lf_hi.astype(F32)
        lf_mid = lf_r1.astype(BF16)
        lf_lo = (lf_r1 - lf_mid.astype(F32)).astype(BF16)
        cums = jnp.dot(cum_mat, jnp.concatenate([lf_hi, lf_mid, lf_lo], axis=0),
                       preferred_element_type=F32) * LOG2_E
        lvl = [cums[i * C:(i + 1) * C] for i in range(2 * n_lvl)]
        loc_top, tot_top = lvl[-2], lvl[-1]
        tot_chunk = tot_top[:C // 2] + tot_top[C // 2:]
        cum = [lanes_to_rows(a) for a in [
            jnp.concatenate([loc_top[:C // 2], loc_top[C // 2:] + tot_top[:C // 2]], axis=0),
            jnp.concatenate([tot_chunk, tot_chunk], axis=0)] + lvl]
        b_full, b_tot = cum[0], cum[1]
        qe = (q * jnp.exp2(b_full)).astype(BF16)
        kd = (k * jnp.exp2(b_tot - b_full)).astype(BF16)
        q_lvl = [(q * jnp.exp2(cum[2 + 2 * l])).astype(BF16) for l in range(n_lvl)]
        k_lvl = [(k * jnp.exp2(cum[3 + 2 * l] - cum[2 + 2 * l])).astype(BF16) for l in range(n_lvl)]
        b_loc = cum[2]
        c_row = b_loc - log2_k

        a_offs, a_diags = [], []
        for ci, ch in enumerate(chunks):
            blocks = []
            for j in range(C // SUB):
                base = ci * C + j * SUB
                sl = slice(base, base + SUB)
                blk = jnp.zeros((SUB, C), F32)
                for s in range(SUB):
                    y = q[sl] * jnp.exp2(jnp.minimum(
                        b_loc[sl] - c_row[base + s:base + s + 1, :], log2_k[base + s:base + s + 1, :]))
                    blk = jnp.where(lane_c == j * SUB + s, jnp.sum(y, axis=-1, keepdims=True), blk)
                blocks.append(blk)
            a_diags.append(jnp.concatenate(blocks, axis=0))
            a_off = jnp.zeros((C, C), F32)
            for l in range(n_lvl):
                a_off = a_off + jnp.where(lvl_masks[l], lax.dot_general(
                    q_lvl[l][ch], k_lvl[l][ch], _NT, preferred_element_type=F32), 0.0)
            a_offs.append(a_off)
            u_ref[pi * n_pc + ci] = lax.dot_general(vb[ch], kd[ch], _TN,
                                                    preferred_element_type=F32)
        per_chunk = []
        for ci, ch in enumerate(chunks):
            a = jnp.where(diag_mask, a_diags[ci], 0.0) + a_offs[ci]
            o_intra = jnp.dot(a.astype(BF16), vb[ch], preferred_element_type=F32)
            per_chunk.append((qe[ch], o_intra, b_tot[ci * C:ci * C + 8]))
        return per_chunk, z

    fronts = [front(pi) for pi in range(n_pieces)]
    p0_ref[...] = jnp.dot(xn_ref[...], wb_ref[...], preferred_element_type=F32)

    st = st_ref[...]
    outs = []
    for pi, (per_chunk, _) in enumerate(fronts):
        for ci, (qe_c, o_intra, b_tot8) in enumerate(per_chunk):
            outs.append(o_intra + lax.dot_general(qe_c, st.astype(BF16), _NT,
                                                  preferred_element_type=F32))
            st = st * jnp.exp2(jnp.tile(b_tot8, (DK // 8, 1))) + u_ref[pi * n_pc + ci]
    st_ref[...] = st

    o = jnp.concatenate(outs, axis=0)
    z = jnp.concatenate([z_p for _, z_p in fronts], axis=0)
    ms = jnp.mean(o * o, axis=-1, keepdims=True)
    out = o * lax.rsqrt(ms + RMS_EPS) * gn_ref[...] * _silu(z)
    o_ref[...] = out.astype(o_ref.dtype)


def _hgrn_mixer(hb, w_in, log_lb, log_1m_lb, g_norm):
    S, K = hb.shape
    D, DK, H = D_MODEL, HG_HEAD_DIM, HG_HEADS
    T = min(HG_TIME_BLOCK, S)
    piece = min(HG_PIECE, T)

    def w_col(off):
        return pl.BlockSpec((K, DK), lambda h, t, off=off: (0, off + h))

    vec = pl.BlockSpec((1, DK), lambda h, t: (0, h))
    return pl.pallas_call(
        functools.partial(_hgrn_kernel, n_pieces=T // piece, piece=piece),
        grid=(H, S // T),
        in_specs=[pl.BlockSpec((T, K), lambda h, t: (t, 0)),
                  pl.BlockSpec((piece, K), lambda h, t: (jnp.minimum((t + 1) * (T // piece),
                                                                     S // piece - 1), 0)),
                  w_col(0), w_col(H), w_col(2 * H), w_col(3 * H), vec, vec, vec],
        out_specs=pl.BlockSpec((T, DK), lambda h, t: (t, h)),
        out_shape=jax.ShapeDtypeStruct((S, D), BF16),
        scratch_shapes=[pltpu.VMEM((DK, DK), F32),
                        pltpu.VMEM((T // HG_CHUNK, DK, DK), F32),
                        pltpu.VMEM((K, 4 * DK), BF16),
                        pltpu.VMEM((piece, 4 * DK), F32)],
        compiler_params=_params("parallel", "arbitrary"),
        name="hgrn_mixer",
    )(hb, hb, w_in, w_in, w_in, w_in, log_lb.reshape(1, D), log_1m_lb.reshape(1, D),
      g_norm.reshape(1, D))


def _nsa_compress_kernel(x_ref, pos_ref, w1_ref, w2_ref, o_ref, *, transposed):
    DH = NSA_HEAD_DIM
    n = x_ref.shape[0] // CMP_STRIDE
    h1 = jnp.zeros((n, CMP_HIDDEN), F32)
    h2 = jnp.zeros((n, CMP_HIDDEN), F32)
    for l in range(CMP_STRIDE):
        x_l = x_ref[pl.ds(l, n, stride=CMP_STRIDE), :]
        l2 = CMP_STRIDE + l
        h1 = h1 + jnp.dot((x_l + pos_ref[l:l + 1, :]).astype(BF16),
                          w1_ref[l * DH:(l + 1) * DH, :].astype(BF16), preferred_element_type=F32)
        h2 = h2 + jnp.dot((x_l + pos_ref[l2:l2 + 1, :]).astype(BF16),
                          w1_ref[l2 * DH:(l2 + 1) * DH, :].astype(BF16), preferred_element_type=F32)
    hid = h1 + pltpu.roll(h2, n - 1, 0)
    act = _silu(hid).astype(BF16)
    if transposed:
        o_ref[...] = lax.dot_general(w2_ref[...].T.astype(BF16), act, _NT,
                                     preferred_element_type=F32).astype(o_ref.dtype)
    else:
        o_ref[...] = jnp.dot(act, w2_ref[...].astype(BF16),
                             preferred_element_type=F32).astype(o_ref.dtype)


def _nsa_compress(x, first, pos, w1, w2, *, transposed):
    G = NSA_KV_GROUPS
    _, S, DH = x.shape
    n = S // CMP_STRIDE
    whole = lambda a: pl.BlockSpec(a.shape, lambda g: (0,) * a.ndim)
    out_block = (None, DH, n) if transposed else (None, n, DH)
    return pl.pallas_call(
        functools.partial(_nsa_compress_kernel, transposed=transposed),
        grid=(G,),
        in_specs=[pl.BlockSpec((None, S, DH), lambda g: (first + g, 0, 0)),
                  whole(pos), whole(w1), whole(w2)],
        out_specs=pl.BlockSpec(out_block, lambda g: (g, 0, 0)),
        out_shape=jax.ShapeDtypeStruct((G,) + out_block[1:], BF16),
        compiler_params=_params("parallel"),
        name="nsa_compress",
    )(x, pos, w1, w2)


def _nsa_kernel(x0_ref, cos0_ref, sin0_ref, x1_ref, cos1_ref, sin1_ref, wq_ref, kc_ref, vct_ref,
                ks_ref, vst_ref, kw_ref, vwt_ref, oh_ref, aggt_ref, gate_ref, z_ref, o_ref,
                m_ref, acc_ref, s_ref, wqt_ref, q_ref, *, tq, n_sub, q_scale):
    R, DH = NSA_REP, NSA_HEAD_DIM
    ncp = kc_ref.shape[1]
    rot = cos0_ref.shape[0]

    def project_q(xr, cr, sr, sub):
        t_sl = slice(sub * tq, (sub + 1) * tq)
        q_t = lax.dot_general(wqt_ref[...], xr[t_sl, :], _NT,
                              preferred_element_type=F32) * q_scale
        cos, sin = cr[:, t_sl], sr[:, t_sl]
        heads = []
        for r in range(R):
            a = q_t[r * DH:(r + 1) * DH]
            swapped = jnp.concatenate([a[rot // 2:rot], a[:rot // 2]], axis=0)
            heads.append(jnp.concatenate([a[:rot] * cos + swapped * sin, a[rot:]],
                                         axis=0).astype(BF16))
        return jnp.concatenate(heads, axis=1)

    @pl.when(pl.program_id(1) == 0)
    def _():
        wqt_ref[...] = wq_ref[...].T.astype(BF16)
        for sub in range(n_sub):
            q_ref[sub] = project_q(x0_ref, cos0_ref, sin0_ref, sub)

    def per_head(x):
        return jnp.concatenate([x] * R, axis=1)

    def rows8(x, n):
        return jnp.tile(x, (n // 8, 1))

    def col_max8(x):
        mx = x[0:8]
        for i in range(1, x.shape[0] // 8):
            mx = jnp.maximum(mx, x[8 * i:8 * i + 8])
        for shift in (4, 2, 1):
            mx = jnp.maximum(mx, pltpu.roll(mx, shift, 0))
        return mx

    def key_rows(ref, kt):
        return ref[0, pl.ds(pl.multiple_of(kt * tq, tq), tq), :]

    c_k = lax.broadcasted_iota(jnp.int32, (tq, tq), 0)
    i_q = lax.broadcasted_iota(jnp.int32, (tq, tq), 1)
    causal = per_head(c_k <= i_q)
    aggt = aggt_ref[...]

    def front(sub):
        qb = pl.program_id(1) * n_sub + sub
        start = qb * tq
        q = q_ref[sub]

        n_c = lax.broadcasted_iota(jnp.int32, (ncp, tq), 0)
        t_c = start + lax.broadcasted_iota(jnp.int32, (ncp, tq), 1)
        cmask = per_head((n_c * CMP_STRIDE + CMP_BLOCK - 1) <= t_c)
        s = jnp.where(cmask, jnp.dot(kc_ref[0], q, preferred_element_type=F32), -jnp.inf)
        m = jnp.max(s, axis=0, keepdims=True)
        m = jnp.where(m == -jnp.inf, 0.0, m)
        e = jnp.exp2(s - m)
        p = e * (1.0 / jnp.maximum(jnp.sum(e, axis=0, keepdims=True), 1e-30))
        o_cmp = jnp.dot(vct_ref[0], p.astype(BF16), preferred_element_type=F32)
        p_sum = p[:, :tq]
        for r in range(1, R):
            p_sum = p_sum + p[:, r * tq:(r + 1) * tq]

        no_old = jnp.where(qb >= 2, 0, NSA_WINDOW)
        no_mid = jnp.where(qb >= 1, 0, NSA_WINDOW)
        old_ok = per_head((2 * tq + i_q - c_k + no_old) < NSA_WINDOW)
        mid_ok = per_head((tq + i_q - c_k + no_mid) < NSA_WINDOW)
        kt_old, kt_mid = jnp.maximum(qb - 2, 0), jnp.maximum(qb - 1, 0)
        k_win = jnp.concatenate([key_rows(kw_ref, kt_old), key_rows(kw_ref, kt_mid),
                                 key_rows(kw_ref, qb)], axis=0)
        s_w = jnp.where(jnp.concatenate([old_ok, mid_ok, causal], axis=0),
                        jnp.dot(k_win, q, preferred_element_type=F32), MASK_NEG)
        p_w = jnp.exp2(s_w - rows8(col_max8(s_w), 3 * tq)).astype(BF16)
        v_win = jnp.concatenate([vwt_ref[0, kt_old], vwt_ref[0, kt_mid], vwt_ref[0, qb]], axis=1)
        acc_w = jnp.dot(jnp.concatenate([v_win, jnp.ones((NSA_ONES_ROWS, 3 * tq), BF16)], axis=0),
                        p_w, preferred_element_type=F32)
        o_win = acc_w[:DH] * rows8(1.0 / acc_w[DH:DH + 8], DH)

        p_hi = p_sum.astype(BF16)
        p_lo = (p_sum - p_hi.astype(F32)).astype(BF16)
        imp = (jnp.dot(aggt, p_hi, preferred_element_type=F32)
               + jnp.dot(aggt, p_lo, preferred_element_type=F32))
        j_s = lax.broadcasted_iota(jnp.int32, (LANES, tq), 0)
        t_s = start + lax.broadcasted_iota(jnp.int32, (LANES, tq), 1)
        cur = lax.shift_right_logical(t_s, 6)
        forced = (j_s == 0) | (j_s == cur) | (j_s == cur - 1)
        allowed = j_s * SLC_BLOCK <= t_s
        sel = jnp.where(forced & allowed, 1.0, 0.0)
        val = jnp.where(allowed & ~forced, imp, -jnp.inf)
        row_f = j_s.astype(F32)
        for _ in range(SLC_TOPK - 3):
            mx = jnp.max(val, axis=0, keepdims=True)
            idx = jnp.min(jnp.where(val == mx, row_f, float(LANES)), axis=0, keepdims=True)
            pick = row_f == idx
            sel = jnp.where(pick, 1.0, sel)
            val = jnp.where(pick, -jnp.inf, val)
        m_neg = jnp.where(sel > 0.0, 0.0, MASK_NEG).astype(BF16)
        q_aug = jnp.concatenate([q, per_head(m_neg)], axis=0)
        return qb, q_aug, o_cmp, o_win

    ones_rows = jnp.ones((NSA_ONES_ROWS, tq), BF16)

    def flash_init():
        m_ref[...] = jnp.full(m_ref.shape, -jnp.inf, F32)
        acc_ref[...] = jnp.zeros_like(acc_ref)

    def flash_update(s, vt_tile):
        m_prev = m_ref[...]
        m_next = jnp.maximum(m_prev, col_max8(s))
        alpha = jnp.exp2(m_prev - m_next)
        p = jnp.exp2(s - rows8(m_next, tq)).astype(BF16)
        v_aug = jnp.concatenate([vt_tile, ones_rows], axis=0)
        acc_ref[...] = (acc_ref[...] * rows8(alpha, DH + NSA_ONES_ROWS)
                        + jnp.dot(v_aug, p, preferred_element_type=F32))
        m_ref[...] = m_next

    def flash_result():
        acc = acc_ref[...]
        return acc[:DH] * rows8(1.0 / acc[DH:DH + 8], DH)

    def back(sub, qb, q_aug, o_cmp, o_win):
        def slc_scores(kt):
            oh = oh_ref[pl.ds(pl.multiple_of(kt * tq, tq), tq), :]
            k_aug = jnp.concatenate([key_rows(ks_ref, kt), oh], axis=1)
            return jnp.dot(k_aug, q_aug, preferred_element_type=F32)

        flash_init()
        s_ref[...] = slc_scores(0)

        def slc_step(kt):
            s_cur = s_ref[...]
            s_ref[...] = slc_scores(kt + 1)
            flash_update(s_cur, vst_ref[0, kt])

        def slc_quad(i, carry):
            for u in range(4):
                slc_step(4 * i + u)
            return carry

        lax.fori_loop(0, lax.shift_right_logical(qb, 2), slc_quad, 0)
        done = qb & ~3

        @pl.when((qb & 2) != 0)
        def _():
            slc_step(done)
            slc_step(done + 1)

        @pl.when((qb & 1) != 0)
        def _():
            slc_step(qb - 1)

        flash_update(jnp.where(causal, s_ref[...], MASK_NEG), vst_ref[0, qb])
        o_slc = flash_result()

        head0 = pl.program_id(0) * R

        def gate(branch, r):
            logit = gate_ref[0, sub, pl.ds(branch * NSA_HEADS + head0 + r, 1), :]
            return jnp.broadcast_to(1.0 / (1.0 + jnp.exp(-logit)), (DH, tq))

        rows = slice(sub * tq, (sub + 1) * tq)
        for r in range(R):
            sl = slice(r * tq, (r + 1) * tq)
            g = [gate(b, r) for b in range(3)]
            o_t = g[0] * o_cmp[:, sl] + g[1] * o_slc[:, sl] + g[2] * o_win[:, sl]
            z = z_ref[rows, r * DH:(r + 1) * DH]
            o_ref[rows, r * DH:(r + 1) * DH] = (o_t.T * _silu(z)).astype(o_ref.dtype)

    fronts = [front(sub) for sub in range(n_sub)]
    for sub in range(n_sub):
        q_ref[sub] = project_q(x1_ref, cos1_ref, sin1_ref, sub)
    for sub in range(n_sub):
        back(sub, *fronts[sub])


def _nsa_attn(hb, w_in, rope_t, kc, vc_t, ksw, vsw_t, onehot, agg_t, gate_t, z):
    G, R, DH = NSA_KV_GROUPS, NSA_REP, NSA_HEAD_DIM
    S = ksw.shape[1]
    tq, n_sub = NSA_TQ, NSA_BLOCKS_PER_STEP
    assert S % (n_sub * tq) == 0 and NSA_WINDOW <= 2 * tq and S // SLC_BLOCK <= LANES
    ncp = kc.shape[1]
    n_t = S // tq
    K = hb.shape[1]
    cos_t, sin_t = rope_t
    rot = cos_t.shape[0]
    step = n_sub * tq
    nxt = lambda i: jnp.minimum(i + 1, S // step - 1)
    rows = lambda off: pl.BlockSpec((1, S, DH), lambda g, i, off=off: (off + g, 0, 0))
    tiles_t = lambda off: pl.BlockSpec((1, n_t, DH, tq), lambda g, i, off=off: (off + g, 0, 0, 0))
    return pl.pallas_call(
        functools.partial(_nsa_kernel, tq=tq, n_sub=n_sub, q_scale=DH ** -0.5 * LOG2_E),
        grid=(G, n_t // n_sub),
        in_specs=[pl.BlockSpec((step, K), lambda g, i: (0, 0)),
                  pl.BlockSpec((rot, step), lambda g, i: (0, 0)),
                  pl.BlockSpec((rot, step), lambda g, i: (0, 0)),
                  pl.BlockSpec((step, K), lambda g, i: (nxt(i), 0)),
                  pl.BlockSpec((rot, step), lambda g, i: (0, nxt(i))),
                  pl.BlockSpec((rot, step), lambda g, i: (0, nxt(i))),
                  pl.BlockSpec((K, R * DH), lambda g, i: (0, g)),
                  pl.BlockSpec((1, ncp, DH), lambda g, i: (g, 0, 0)),
                  pl.BlockSpec((1, DH, ncp), lambda g, i: (g, 0, 0)),
                  rows(0), tiles_t(0), rows(G), tiles_t(G),
                  pl.BlockSpec((S, LANES), lambda g, i: (0, 0)),
                  pl.BlockSpec((LANES, ncp), lambda g, i: (0, 0)),
                  pl.BlockSpec((1, n_sub, LANES, tq), lambda g, i: (0, i, 0, 0)),
                  pl.BlockSpec((n_sub * tq, R * DH), lambda g, i: (i, g))],
        out_specs=pl.BlockSpec((n_sub * tq, R * DH), lambda g, i: (i, g)),
        out_shape=jax.ShapeDtypeStruct((S, D_MODEL), BF16),
        scratch_shapes=[pltpu.VMEM((8, R * tq), F32),
                        pltpu.VMEM((DH + NSA_ONES_ROWS, R * tq), F32),
                        pltpu.VMEM((tq, R * tq), F32),
                        pltpu.VMEM((R * DH, K), BF16),
                        pltpu.VMEM((n_sub, DH, R * tq), BF16)],
        compiler_params=_params("parallel", "arbitrary"),
        name="nsa_attn",
    )(hb, cos_t, sin_t, hb, cos_t, sin_t, w_in, kc, vc_t, ksw, vsw_t, ksw, vsw_t, onehot, agg_t,
      gate_t, z)


def _swa_kernel(sink_ref, q_ref, kp_ref, kc_ref, vp_ref, vc_ref, z_ref, o_ref):
    W, R = SWA_WINDOW, SWA_REP
    n = pl.program_id(0)
    i_q = lax.broadcasted_iota(jnp.int32, (W, 2 * W), 0)
    c_k = lax.broadcasted_iota(jnp.int32, (W, 2 * W), 1)
    diff = i_q - (c_k - W)
    mask = (diff >= 0) & (diff < W) & ((n > 0) | (c_k >= W))
    lane = lax.broadcasted_iota(jnp.int32, (W, LANES), 1)
    low = lane < SWA_HEAD_DIM
    ones_v = jnp.ones((2 * W, LANES), BF16)
    for g in range(SWA_KV_HEADS):
        gl = slice(g * LANES, (g + 1) * LANES)
        kk = jnp.concatenate([kp_ref[:, gl], kc_ref[:, gl]], axis=0)
        vv = jnp.concatenate([vp_ref[:, gl], vc_ref[:, gl]], axis=0)
        qs = []
        for r in range(R):
            h = g * R + r
            q2 = q_ref[:, (h // 2) * LANES:(h // 2 + 1) * LANES]
            qs.append(jnp.where(low if h % 2 == 0 else ~low, q2, jnp.zeros_like(q2)))
        s_all = lax.dot_general(jnp.concatenate(qs, axis=0), kk, _NT,
                                preferred_element_type=F32)
        ps, sink_p = [], []
        for r in range(R):
            sink = sink_ref[g * R + r] * LOG2_E
            s = jnp.where(mask, s_all[r * W:(r + 1) * W], -jnp.inf)
            m = jnp.maximum(jnp.max(s, axis=1, keepdims=True), sink)
            ps.append(jnp.exp2(s - m).astype(BF16))
            sink_p.append(jnp.exp2(sink - m))
        o_all = jnp.dot(jnp.concatenate(ps, axis=0), jnp.concatenate([vv, ones_v], axis=1),
                        preferred_element_type=F32)

        def head_out(r):
            rows = slice(r * W, (r + 1) * W)
            return o_all[rows, :LANES] * (1.0 / (o_all[rows, LANES:] + sink_p[r]))

        for pr in range(R // 2):
            o_even, o_odd = head_out(2 * pr), head_out(2 * pr + 1)
            col = slice((g * R // 2 + pr) * LANES, (g * R // 2 + pr + 1) * LANES)
            o_ref[:, col] = (jnp.where(low, o_even, o_odd) * _silu(z_ref[:, col])).astype(o_ref.dtype)


def _swa_attn(sinks, q, kv_dup, z):
    S, D = q.shape
    W = SWA_WINDOW
    wide = SWA_KV_HEADS * LANES
    cur = lambda i: (i, 0)
    k_prev, k_cur = (lambda i: (jnp.maximum(i - 1, 0), 0)), cur
    v_prev, v_cur = (lambda i: (jnp.maximum(i - 1, 0), 1)), (lambda i: (i, 1))
    return pl.pallas_call(
        _swa_kernel,
        grid=(S // W,),
        in_specs=[pl.BlockSpec(memory_space=pltpu.SMEM),
                  pl.BlockSpec((W, D), cur),
                  pl.BlockSpec((W, wide), k_prev), pl.BlockSpec((W, wide), k_cur),
                  pl.BlockSpec((W, wide), v_prev), pl.BlockSpec((W, wide), v_cur),
                  pl.BlockSpec((W, D), cur)],
        out_specs=pl.BlockSpec((W, D), cur),
        out_shape=jax.ShapeDtypeStruct((S, D), BF16),
        compiler_params=_params("parallel"),
        name="swa_attn",
    )(sinks, q, kv_dup, kv_dup, kv_dup, kv_dup, z)


def _rope_tables(positions, head_dim, scale):
    rot = head_dim // ROPE_FRACTION
    half = rot // 2
    inv_freq = ROPE_THETA ** (-jnp.arange(0, rot, 2, dtype=F32) / rot)
    ang = positions.reshape(-1).astype(F32)[:, None] * inv_freq
    cos, sin = jnp.cos(ang), jnp.sin(ang)
    S = cos.shape[0]
    rest = head_dim - rot
    cos_h = jnp.concatenate([cos, cos, jnp.ones((S, rest), F32)], axis=1)
    sin_h = jnp.concatenate([-sin, sin, jnp.zeros((S, rest), F32)], axis=1)
    reps = LANES // head_dim
    return (jnp.tile(cos_h, (1, reps)) * scale, jnp.tile(sin_h, (1, reps)) * scale, half, head_dim)


def _nsa_constants(S):
    n_cmp = S // CMP_STRIDE - 1
    ncp = S // CMP_STRIDE
    n_slc = S // SLC_BLOCK
    ratio = SLC_BLOCK // CMP_STRIDE
    i = np.arange(ncp)[:, None]
    j = np.arange(LANES)[None, :]
    agg = ((i >= ratio * j - CMP_BLOCK // CMP_STRIDE + 1) & (i <= ratio * j + ratio - 1)
           & (i < n_cmp) & (j < n_slc))
    onehot = (np.arange(S)[:, None] // SLC_BLOCK) == j
    return jnp.asarray(onehot, dtype=BF16), jnp.asarray(agg.T, dtype=BF16)


def _rope_tables_t(positions, head_dim):
    rot = head_dim // ROPE_FRACTION
    inv_freq = ROPE_THETA ** (-jnp.arange(0, rot, 2, dtype=F32) / rot)
    ang = inv_freq[:, None] * positions.reshape(-1).astype(F32)[None, :]
    cos, sin = jnp.cos(ang), jnp.sin(ang)
    return jnp.concatenate([cos, cos], axis=0), jnp.concatenate([-sin, sin], axis=0)


def _hgrn_layer(hb, w_in, g_norm, log_lb, log_1m_lb):
    return _hgrn_mixer(hb, w_in, log_lb, log_1m_lb, g_norm)


def _nsa_layer(hb, positions, w_in, pos_k, w1_k, w2_k, pos_v, w1_v, w2_v):
    S = hb.shape[0]
    H, G, DH = NSA_HEADS, NSA_KV_GROUPS, NSA_HEAD_DIM
    gw = G * DH
    assert gw == 512 and (H * DH) % gw == 0
    t_kc = H * DH // gw
    o_gate = H * DH + 6 * gw
    o_z = o_gate + 3 * H
    rope_k = _rope_tables(positions, DH, 1.0)
    tq = NSA_TQ

    wide = 2 * gw
    rope_t = _rope_tables_t(positions, DH)
    kv_c = _mm(hb, w_in, cols=(t_kc, 1, 2), out_dtype=F32, rope=rope_k, rope_tiles=1,
               head_major=True)
    ksw = _mm(hb, w_in, cols=(t_kc + 2, 2, 2), out_dtype=BF16, rope=rope_k, head_major=True)
    vsw_t = _mm_t(hb, w_in, cols=(t_kc + 3, 2, 2), out_dtype=BF16, tk=tq)
    z = _mm(hb, w_in[:, o_z:], out_dtype=F32, tn=wide)
    w_gate = jnp.pad(w_in[:, o_gate:o_z], ((0, 0), (0, LANES - 3 * H)))
    gate_t = _mm_t(hb, w_gate, out_dtype=F32, tk=tq, tn=LANES)

    kc = _nsa_compress(kv_c, 0, pos_k, w1_k, w2_k, transposed=False)
    vc_t = _nsa_compress(kv_c, G, pos_v, w1_v, w2_v, transposed=True)
    onehot, agg_t = _nsa_constants(S)
    return _nsa_attn(hb, w_in, rope_t, kc, vc_t, ksw, vsw_t, onehot, agg_t, gate_t, z)


def _swa_layer(hb, positions, w_in, sinks):
    H, KV, DH = SWA_HEADS, SWA_KV_HEADS, SWA_HEAD_DIM
    o_k = H * DH
    o_v = o_k + KV * DH
    o_z = o_v + KV * DH
    tn = 512
    assert o_k % tn == 0 and o_z % tn == 0
    dup = np.concatenate([np.tile(np.arange(g * DH, (g + 1) * DH), 2) for g in range(KV)])
    rope_q = _rope_tables(positions, DH, DH ** -0.5 * LOG2_E)
    rope_k = _rope_tables(positions, DH, 1.0)
    q = _mm(hb, w_in, cols=(0, 1, o_k // (2 * tn)), out_dtype=BF16, rope=rope_q, tn=2 * tn)
    w_kv = jnp.concatenate([w_in[:, o_k:o_v][:, dup], w_in[:, o_v:o_z][:, dup]], axis=1)
    kv_dup = _mm(hb, w_kv, out_dtype=BF16, rope=rope_k, rope_tiles=1, tn=tn)
    z = _mm(hb, w_in, cols=(o_z // tn, 1, o_k // tn), out_dtype=F32, tn=tn)
    return _swa_attn(sinks, q, kv_dup, z)


def kernel(x, positions, hgrn_lb_logits, l0_w_in, l0_g_norm, l0_w_out, l0_ln_g, l0_ln_b, l1_w_in, l1_cmp_pos_k, l1_cmp_w1_k, l1_cmp_w2_k, l1_cmp_pos_v, l1_cmp_w1_v, l1_cmp_w2_v, l1_w_out, l1_ln_g, l1_ln_b, l2_w_in, l2_sinks, l2_w_out, l2_ln_g, l2_ln_b, l3_w_in, l3_g_norm, l3_w_out, l3_ln_g, l3_ln_b):
    B, S, D = x.shape
    assert B == 1 and D == D_MODEL
    lb = jnp.cumsum(jax.nn.softmax(hgrn_lb_logits.astype(F32), axis=0), axis=0)
    lb = lb - lb[0:1]
    log_lb, log_1m_lb = jnp.log(lb), jnp.log1p(-lb)

    h = x.reshape(S, D)
    hb = h.astype(BF16)

    a = _hgrn_layer(hb, l0_w_in, l0_g_norm, log_lb[0], log_1m_lb[0])
    h, hb = _outproj_ln(a, l0_w_out.astype(BF16), h, l0_ln_g, l0_ln_b)

    a = _nsa_layer(hb, positions, l1_w_in, l1_cmp_pos_k, l1_cmp_w1_k, l1_cmp_w2_k,
                   l1_cmp_pos_v, l1_cmp_w1_v, l1_cmp_w2_v)
    h, hb = _outproj_ln(a, l1_w_out.astype(BF16), h, l1_ln_g, l1_ln_b)

    a = _swa_layer(hb, positions, l2_w_in, l2_sinks)
    h, hb = _outproj_ln(a, l2_w_out.astype(BF16), h, l2_ln_g, l2_ln_b)

    a = _hgrn_layer(hb, l3_w_in, l3_g_norm, log_lb[1], log_1m_lb[1])
    h, hb = _outproj_ln(a, l3_w_out.astype(BF16), h, l3_ln_g, l3_ln_b)
    return h.reshape(B, S, D)
```

```python
import functools

import numpy as np
import jax
import jax.numpy as jnp
from jax import lax
from jax.experimental import pallas as pl
from jax.experimental.pallas import tpu as pltpu

F32 = jnp.float32
BF16 = jnp.bfloat16

D_MODEL = 2048
DEPTH = 4
DEEPNORM_ALPHA = (2 * DEPTH) ** 0.25
LN_EPS = 1e-5
RMS_EPS = 1e-6
ROPE_THETA = 500000.0
ROPE_FRACTION = 4

HG_HEAD_DIM = 128
HG_HEADS = D_MODEL // HG_HEAD_DIM
HG_CHUNK = 64
HG_SUB = 8
HG_TIME_BLOCK = 2048
HG_PIECE = 512
LOG2_E = 1.4426950408889634

NSA_HEAD_DIM = 128
NSA_HEADS = D_MODEL // NSA_HEAD_DIM
NSA_KV_GROUPS = 4
NSA_REP = NSA_HEADS // NSA_KV_GROUPS
CMP_BLOCK = 32
CMP_STRIDE = 16
CMP_HIDDEN = 256
SLC_BLOCK = 64
SLC_TOPK = 16
NSA_WINDOW = 512
NSA_TQ = 256
NSA_ONES_ROWS = 16
NSA_BLOCKS_PER_STEP = 2
FORCE_BONUS = 1.0e4
MASK_NEG = -1.0e30

SWA_HEAD_DIM = 64
SWA_HEADS = D_MODEL // SWA_HEAD_DIM
SWA_KV_HEADS = 4
SWA_REP = SWA_HEADS // SWA_KV_HEADS
SWA_WINDOW = 128

LANES = 128
VMEM_LIMIT_BYTES = 48 * 1024 * 1024

_NT = (((1,), (1,)), ((), ()))
_TN = (((0,), (0,)), ((), ()))


def _params(*sem):
    return pltpu.CompilerParams(dimension_semantics=sem, vmem_limit_bytes=VMEM_LIMIT_BYTES)


def _silu(x):
    return x * (1.0 / (1.0 + jnp.exp(-x)))


def _mm_kernel(*refs, rope_half, rope_period, rope_tiles, head_major, n_chunks):
    if rope_half:
        x_ref, w_ref, c_ref, s_ref, o_ref, wb_ref = refs
    else:
        x_ref, w_ref, o_ref, wb_ref = refs

    @pl.when(pl.program_id(1) == 0)
    def _():
        wb_ref[...] = w_ref[...].astype(BF16)

    acc = jnp.dot(x_ref[...], wb_ref[...], preferred_element_type=F32)

    def write(with_rope):
        if with_rope:
            cos = c_ref[...]
            sin = s_ref[...]
            lane = lax.broadcasted_iota(jnp.int32, cos.shape, 1)
            first_half = (lane & (rope_period - 1)) < rope_half
        for j in range(n_chunks):
            a = acc[:, j * LANES:(j + 1) * LANES]
            if with_rope:
                up = pltpu.roll(a, LANES - rope_half, 1)
                dn = pltpu.roll(a, rope_half, 1)
                a = a * cos + jnp.where(first_half, up, dn) * sin
            if head_major:
                o_ref[j] = a.astype(o_ref.dtype)
            else:
                o_ref[:, j * LANES:(j + 1) * LANES] = a.astype(o_ref.dtype)

    if rope_half and rope_tiles is not None:
        pl.when(pl.program_id(0) < rope_tiles)(functools.partial(write, True))
        pl.when(pl.program_id(0) >= rope_tiles)(functools.partial(write, False))
    else:
        write(bool(rope_half))


def _col_tiles(w, cols, tn):
    return (0, 1, w.shape[1] // tn) if cols is None else cols


def _mm(x, w, *, out_dtype, cols=None, rope=None, rope_tiles=None, head_major=False, tn=512):
    S, K = x.shape
    tm = min(1024, S)
    first, stride, n_tiles = _col_tiles(w, cols, tn)
    N = n_tiles * tn
    n_chunks = tn // LANES
    in_specs = [pl.BlockSpec((tm, K), lambda j, i: (i, 0)),
                pl.BlockSpec((K, tn), lambda j, i: (0, first + stride * j))]
    args = [x, w]
    half = period = 0
    if rope is not None:
        cos_t, sin_t, half, period = rope
        in_specs += [pl.BlockSpec((tm, LANES), lambda j, i: (i, 0))] * 2
        args += [cos_t, sin_t]
    if head_major:
        out_shape = jax.ShapeDtypeStruct((N // LANES, S, LANES), out_dtype)
        out_spec = pl.BlockSpec((n_chunks, tm, LANES), lambda j, i: (j, i, 0))
    else:
        out_shape = jax.ShapeDtypeStruct((S, N), out_dtype)
        out_spec = pl.BlockSpec((tm, tn), lambda j, i: (i, j))
    return pl.pallas_call(
        functools.partial(_mm_kernel, rope_half=half, rope_period=period, rope_tiles=rope_tiles,
                          head_major=head_major, n_chunks=n_chunks),
        grid=(n_tiles, S // tm),
        in_specs=in_specs,
        out_specs=out_spec,
        out_shape=out_shape,
        scratch_shapes=[pltpu.VMEM((K, tn), BF16)],
        compiler_params=_params("arbitrary", "arbitrary"),
        name="in_proj",
    )(*args)


def _mm_t_kernel(w_ref, x_ref, o_ref, wt_ref, *, n_heads, n_tiles, tk):
    @pl.when(pl.program_id(1) == 0)
    def _():
        wt_ref[...] = w_ref[...].T.astype(BF16)

    acc = lax.dot_general(wt_ref[...], x_ref[...], _NT, preferred_element_type=F32)
    for c in range(n_heads):
        for b in range(n_tiles):
            o_ref[c, b] = acc[c * LANES:(c + 1) * LANES, b * tk:(b + 1) * tk].astype(o_ref.dtype)


def _mm_t(x, w, *, out_dtype, tk, cols=None, tn=512):
    S, K = x.shape
    tm = min(1024, S)
    first, stride, n_tiles = _col_tiles(w, cols, tn)
    N = n_tiles * tn
    return pl.pallas_call(
        functools.partial(_mm_t_kernel, n_heads=tn // LANES, n_tiles=tm // tk, tk=tk),
        grid=(n_tiles, S // tm),
        in_specs=[pl.BlockSpec((K, tn), lambda j, i: (0, first + stride * j)),
                  pl.BlockSpec((tm, K), lambda j, i: (i, 0))],
        out_specs=pl.BlockSpec((tn // LANES, tm // tk, LANES, tk), lambda j, i: (j, i, 0, 0)),
        out_shape=jax.ShapeDtypeStruct((N // LANES, S // tk, LANES, tk), out_dtype),
        scratch_shapes=[pltpu.VMEM((tn, K), BF16)],
        compiler_params=_params("arbitrary", "arbitrary"),
        name="in_proj_t",
    )(w, x)


def _outproj_ln_kernel(a_ref, w_ref, h_ref, g_ref, b_ref, o_ref, ob_ref):
    y = jnp.dot(a_ref[...], w_ref[...], preferred_element_type=F32)
    u = DEEPNORM_ALPHA * h_ref[...] + y
    mu = jnp.mean(u, axis=-1, keepdims=True)
    xc = u - mu
    var = jnp.mean(xc * xc, axis=-1, keepdims=True)
    out = xc * lax.rsqrt(var + LN_EPS) * g_ref[...] + b_ref[...]
    o_ref[...] = out
    ob_ref[...] = out.astype(BF16)


def _outproj_ln(a, w, h, g, b):
    S, D = h.shape
    tm = min(512, S)
    row = pl.BlockSpec((tm, D), lambda i: (i, 0))
    vec = pl.BlockSpec((1, D), lambda i: (0, 0))
    return pl.pallas_call(
        _outproj_ln_kernel,
        grid=(S // tm,),
        in_specs=[row, pl.BlockSpec((D, D), lambda i: (0, 0)), row, vec, vec],
        out_specs=[row, row],
        out_shape=[jax.ShapeDtypeStruct((S, D), F32), jax.ShapeDtypeStruct((S, D), BF16)],
        compiler_params=_params("parallel"),
        name="outproj_ln",
    )(a, w, h, g.reshape(1, D), b.reshape(1, D))


def _hgrn_kernel(x_ref, wq_ref, wf_ref, wv_ref, wz_ref, llb_ref, l1m_ref, gn_ref,
                 o_ref, st_ref, u_ref, wb_ref, *, n_pieces, piece):
    C, SUB, DK = HG_CHUNK, HG_SUB, HG_HEAD_DIM
    n_lvl = (C // (2 * SUB)).bit_length()
    n_pc = piece // C

    @pl.when(pl.program_id(1) == 0)
    def _():
        st_ref[...] = jnp.zeros_like(st_ref)
        for i, w_ref in enumerate((wq_ref, wf_ref, wv_ref, wz_ref)):
            wb_ref[:, i * DK:(i + 1) * DK] = w_ref[...].astype(BF16)

    r = lax.broadcasted_iota(jnp.int32, (C, C), 0)
    c = lax.broadcasted_iota(jnp.int32, (C, C), 1)
    causal = c <= r
    halves = [SUB << lvl for lvl in range(n_lvl)]
    cum_rows = []
    for h in halves:
        same_h = (r // h) == (c // h)
        cum_rows += [causal & same_h, same_h]
    cum_one = jnp.concatenate([m.astype(F32) for m in cum_rows], axis=0).astype(BF16)
    cum_mat = jnp.concatenate([cum_one] * 3, axis=1)
    diag_mask = causal & ((r // SUB) == (c // SUB))
    log_lb = llb_ref[...]
    log_1m_lb = l1m_ref[...]
    lane_c = lax.broadcasted_iota(jnp.int32, (SUB, C), 1)
    chunks = [slice(ci * C, (ci + 1) * C) for ci in range(n_pc)]

    def rows_to_lanes(x):
        return jnp.concatenate([x[sl] for sl in chunks], axis=1)

    def lanes_to_rows(x):
        return jnp.concatenate([x[:, ci * DK:(ci + 1) * DK] for ci in range(n_pc)], axis=0)

    def placed(x, lo):
        parts = [jnp.zeros((lo, DK), F32), x, jnp.zeros((C - lo - x.shape[0], DK), F32)]
        return jnp.concatenate([p for p in parts if p.shape[0]], axis=0)

    def front(pi):
        rows = slice(pi * piece, (pi + 1) * piece)
        proj = jnp.dot(x_ref[rows, :], wb_ref[...], preferred_element_type=F32)
        q, fl, v, z = (proj[:, i * DK:(i + 1) * DK] for i in range(4))
        vb = v.astype(BF16)
        log_sig = jnp.minimum(fl, 0.0) - jnp.log(1.0 + jnp.exp(-jnp.abs(fl)))
        t = log_1m_lb + log_sig
        log_f = jnp.maximum(log_lb, t) + jnp.log(1.0 + jnp.exp(-jnp.abs(log_lb - t)))
        log2_k = (t - fl) * LOG2_E
        k = jnp.exp2(log2_k)
        lf = rows_to_lanes(log_f)
        lf_hi = lf.astype(BF16)
        lf_r1 = lf - lf_hi.astype(F32)
        lf_mid = lf_r1.astype(BF16)
        lf_lo = (lf_r1 - lf_mid.astype(F32)).astype(BF16)
        cums = jnp.dot(cum_mat, jnp.concatenate([lf_hi, lf_mid, lf_lo], axis=0),
                       preferred_element_type=F32) * LOG2_E
        lvl = [cums[i * C:(i + 1) * C] for i in range(2 * n_lvl)]
        loc_top, tot_top = lvl[-2], lvl[-1]
        tot_chunk = tot_top[:C // 2] + tot_top[C // 2:]
        cum = [lanes_to_rows(a) for a in [
            jnp.concatenate([loc_top[:C // 2], loc_top[C // 2:] + tot_top[:C // 2]], axis=0),
            jnp.concatenate([tot_chunk, tot_chunk], axis=0)] + lvl]
        b_full, b_tot = cum[0], cum[1]
        qe = (q * jnp.exp2(b_full)).astype(BF16)
        kd = (k * jnp.exp2(b_tot - b_full)).astype(BF16)
        q_lvl = [q * jnp.exp2(cum[2 + 2 * l]) for l in range(n_lvl)]
        k_lvl = [k * jnp.exp2(cum[3 + 2 * l] - cum[2 + 2 * l]) for l in range(n_lvl)]
        b_loc = cum[2]
        c_row = b_loc - log2_k

        a_offs, a_diags = [], []
        for ci, ch in enumerate(chunks):
            blocks = []
            for j in range(C // SUB):
                base = ci * C + j * SUB
                sl = slice(base, base + SUB)
                blk = jnp.zeros((SUB, C), F32)
                for s in range(SUB):
                    y = q[sl] * jnp.exp2(jnp.minimum(
                        b_loc[sl] - c_row[base + s:base + s + 1, :], log2_k[base + s:base + s + 1, :]))
                    blk = jnp.where(lane_c == j * SUB + s, jnp.sum(y, axis=-1, keepdims=True), blk)
                blocks.append(blk)
            a_diags.append(jnp.concatenate(blocks, axis=0))
            lhs, rhs = [], []
            for l, h in enumerate(halves):
                q_c, k_c = q_lvl[l][ch], k_lvl[l][ch]
                for lo in range(0, C, 2 * h):
                    lhs.append(placed(q_c[lo + h:lo + 2 * h], lo + h))
                    rhs.append(placed(k_c[lo:lo + h], lo))
            a_offs.append(lax.dot_general(jnp.concatenate(lhs, axis=1).astype(BF16),
                                          jnp.concatenate(rhs, axis=1).astype(BF16),
                                          _NT, preferred_element_type=F32))
            u_ref[pi * n_pc + ci] = lax.dot_general(vb[ch], kd[ch], _TN,
                                                    preferred_element_type=F32)
        per_chunk = []
        for ci, ch in enumerate(chunks):
            a = jnp.where(diag_mask, a_diags[ci], 0.0) + a_offs[ci]
            o_intra = jnp.dot(a.astype(BF16), vb[ch], preferred_element_type=F32)
            per_chunk.append((qe[ch], o_intra, b_tot[ci * C:ci * C + 8]))
        return per_chunk, z

    fronts = [front(pi) for pi in range(n_pieces)]

    st = st_ref[...]
    outs = []
    for pi, (per_chunk, _) in enumerate(fronts):
        for ci, (qe_c, o_intra, b_tot8) in enumerate(per_chunk):
            outs.append(o_intra + lax.dot_general(qe_c, st.astype(BF16), _NT,
                                                  preferred_element_type=F32))
            st = st * jnp.exp2(jnp.tile(b_tot8, (DK // 8, 1))) + u_ref[pi * n_pc + ci]
    st_ref[...] = st

    o = jnp.concatenate(outs, axis=0)
    z = jnp.concatenate([z_p for _, z_p in fronts], axis=0)
    ms = jnp.mean(o * o, axis=-1, keepdims=True)
    out = o * lax.rsqrt(ms + RMS_EPS) * gn_ref[...] * _silu(z)
    o_ref[...] = out.astype(o_ref.dtype)


def _hgrn_mixer(hb, w_in, log_lb, log_1m_lb, g_norm):
    S, K = hb.shape
    D, DK, H = D_MODEL, HG_HEAD_DIM, HG_HEADS
    T = min(HG_TIME_BLOCK, S)
    piece = min(HG_PIECE, T)

    def w_col(off):
        return pl.BlockSpec((K, DK), lambda h, t, off=off: (0, off + h))

    vec = pl.BlockSpec((1, DK), lambda h, t: (0, h))
    return pl.pallas_call(
        functools.partial(_hgrn_kernel, n_pieces=T // piece, piece=piece),
        grid=(H, S // T),
        in_specs=[pl.BlockSpec((T, K), lambda h, t: (t, 0)),
                  w_col(0), w_col(H), w_col(2 * H), w_col(3 * H), vec, vec, vec],
        out_specs=pl.BlockSpec((T, DK), lambda h, t: (t, h)),
        out_shape=jax.ShapeDtypeStruct((S, D), BF16),
        scratch_shapes=[pltpu.VMEM((DK, DK), F32),
                        pltpu.VMEM((T // HG_CHUNK, DK, DK), F32),
                        pltpu.VMEM((K, 4 * DK), BF16)],
        compiler_params=_params("parallel", "arbitrary"),
        name="hgrn_mixer",
    )(hb, w_in, w_in, w_in, w_in, log_lb.reshape(1, D), log_1m_lb.reshape(1, D),
      g_norm.reshape(1, D))


def _nsa_compress_kernel(x_ref, pos_ref, w1_ref, w2_ref, o_ref, *, transposed):
    DH = NSA_HEAD_DIM
    n = x_ref.shape[0] // CMP_STRIDE
    h1 = jnp.zeros((n, CMP_HIDDEN), F32)
    h2 = jnp.zeros((n, CMP_HIDDEN), F32)
    for l in range(CMP_STRIDE):
        x_l = x_ref[pl.ds(l, n, stride=CMP_STRIDE), :]
        l2 = CMP_STRIDE + l
        h1 = h1 + jnp.dot((x_l + pos_ref[l:l + 1, :]).astype(BF16),
                          w1_ref[l * DH:(l + 1) * DH, :].astype(BF16), preferred_element_type=F32)
        h2 = h2 + jnp.dot((x_l + pos_ref[l2:l2 + 1, :]).astype(BF16),
                          w1_ref[l2 * DH:(l2 + 1) * DH, :].astype(BF16), preferred_element_type=F32)
    hid = h1 + pltpu.roll(h2, n - 1, 0)
    act = _silu(hid).astype(BF16)
    if transposed:
        o_ref[...] = lax.dot_general(w2_ref[...].T.astype(BF16), act, _NT,
                                     preferred_element_type=F32).astype(o_ref.dtype)
    else:
        o_ref[...] = jnp.dot(act, w2_ref[...].astype(BF16),
                             preferred_element_type=F32).astype(o_ref.dtype)


def _nsa_compress(x, first, pos, w1, w2, *, transposed):
    G = NSA_KV_GROUPS
    _, S, DH = x.shape
    n = S // CMP_STRIDE
    whole = lambda a: pl.BlockSpec(a.shape, lambda g: (0,) * a.ndim)
    out_block = (None, DH, n) if transposed else (None, n, DH)
    return pl.pallas_call(
        functools.partial(_nsa_compress_kernel, transposed=transposed),
        grid=(G,),
        in_specs=[pl.BlockSpec((None, S, DH), lambda g: (first + g, 0, 0)),
                  whole(pos), whole(w1), whole(w2)],
        out_specs=pl.BlockSpec(out_block, lambda g: (g, 0, 0)),
        out_shape=jax.ShapeDtypeStruct((G,) + out_block[1:], BF16),
        compiler_params=_params("parallel"),
        name="nsa_compress",
    )(x, pos, w1, w2)


def _nsa_kernel(x0_ref, cos0_ref, sin0_ref, x1_ref, cos1_ref, sin1_ref, wq_ref, kc_ref, vct_ref,
                ks_ref, vst_ref, kw_ref, vwt_ref, oh_ref, aggt_ref, gate_ref, z_ref, o_ref,
                m_ref, acc_ref, s_ref, wqt_ref, q_ref, *, tq, n_sub, q_scale):
    R, DH = NSA_REP, NSA_HEAD_DIM
    ncp = kc_ref.shape[1]
    rot = cos0_ref.shape[0]

    def project_q(xr, cr, sr, sub):
        t_sl = slice(sub * tq, (sub + 1) * tq)
        q_t = lax.dot_general(wqt_ref[...], xr[t_sl, :], _NT,
                              preferred_element_type=F32) * q_scale
        cos, sin = cr[:, t_sl], sr[:, t_sl]
        heads = []
        for r in range(R):
            a = q_t[r * DH:(r + 1) * DH]
            swapped = jnp.concatenate([a[rot // 2:rot], a[:rot // 2]], axis=0)
            heads.append(jnp.concatenate([a[:rot] * cos + swapped * sin, a[rot:]],
                                         axis=0).astype(BF16))
        return jnp.concatenate(heads, axis=1)

    @pl.when(pl.program_id(1) == 0)
    def _():
        wqt_ref[...] = wq_ref[...].T.astype(BF16)
        for sub in range(n_sub):
            q_ref[sub] = project_q(x0_ref, cos0_ref, sin0_ref, sub)

    def per_head(x):
        return jnp.concatenate([x] * R, axis=1)

    def rows8(x, n):
        return jnp.tile(x, (n // 8, 1))

    def col_max8(x):
        mx = x[0:8]
        for i in range(1, x.shape[0] // 8):
            mx = jnp.maximum(mx, x[8 * i:8 * i + 8])
        for shift in (4, 2, 1):
            mx = jnp.maximum(mx, pltpu.roll(mx, shift, 0))
        return mx

    def key_rows(ref, kt):
        return ref[0, pl.ds(pl.multiple_of(kt * tq, tq), tq), :]

    c_k = lax.broadcasted_iota(jnp.int32, (tq, tq), 0)
    i_q = lax.broadcasted_iota(jnp.int32, (tq, tq), 1)
    causal = per_head(c_k <= i_q)
    aggt = aggt_ref[...]

    def front(sub):
        qb = pl.program_id(1) * n_sub + sub
        start = qb * tq
        q = q_ref[sub]

        n_c = lax.broadcasted_iota(jnp.int32, (ncp, tq), 0)
        t_c = start + lax.broadcasted_iota(jnp.int32, (ncp, tq), 1)
        cmask = per_head((n_c * CMP_STRIDE + CMP_BLOCK - 1) <= t_c)
        s = jnp.where(cmask, jnp.dot(kc_ref[0], q, preferred_element_type=F32), -jnp.inf)
        m = jnp.max(s, axis=0, keepdims=True)
        m = jnp.where(m == -jnp.inf, 0.0, m)
        e = jnp.exp2(s - m)
        p = e * (1.0 / jnp.maximum(jnp.sum(e, axis=0, keepdims=True), 1e-30))
        o_cmp = jnp.dot(vct_ref[0], p.astype(BF16), preferred_element_type=F32)
        p_sum = p[:, :tq]
        for r in range(1, R):
            p_sum = p_sum + p[:, r * tq:(r + 1) * tq]

        no_old = jnp.where(qb >= 2, 0, NSA_WINDOW)
        no_mid = jnp.where(qb >= 1, 0, NSA_WINDOW)
        old_ok = per_head((2 * tq + i_q - c_k + no_old) < NSA_WINDOW)
        mid_ok = per_head((tq + i_q - c_k + no_mid) < NSA_WINDOW)
        kt_old, kt_mid = jnp.maximum(qb - 2, 0), jnp.maximum(qb - 1, 0)
        k_win = jnp.concatenate([key_rows(kw_ref, kt_old), key_rows(kw_ref, kt_mid),
                                 key_rows(kw_ref, qb)], axis=0)
        s_w = jnp.where(jnp.concatenate([old_ok, mid_ok, causal], axis=0),
                        jnp.dot(k_win, q, preferred_element_type=F32), MASK_NEG)
        p_w = jnp.exp2(s_w - rows8(col_max8(s_w), 3 * tq)).astype(BF16)
        v_win = jnp.concatenate([vwt_ref[0, kt_old], vwt_ref[0, kt_mid], vwt_ref[0, qb]], axis=1)
        acc_w = jnp.dot(jnp.concatenate([v_win, jnp.ones((NSA_ONES_ROWS, 3 * tq), BF16)], axis=0),
                        p_w, preferred_element_type=F32)
        o_win = acc_w[:DH] * rows8(1.0 / acc_w[DH:DH + 8], DH)

        p_hi = p_sum.astype(BF16)
        p_lo = (p_sum - p_hi.astype(F32)).astype(BF16)
        imp = (jnp.dot(aggt, p_hi, preferred_element_type=F32)
               + jnp.dot(aggt, p_lo, preferred_element_type=F32))
        j_s = lax.broadcasted_iota(jnp.int32, (LANES, tq), 0)
        t_s = start + lax.broadcasted_iota(jnp.int32, (LANES, tq), 1)
        cur = lax.shift_right_logical(t_s, 6)
        forced = (j_s == 0) | (j_s == cur) | (j_s == cur - 1)
        allowed = j_s * SLC_BLOCK <= t_s
        sel = jnp.where(forced & allowed, 1.0, 0.0)
        val = jnp.where(allowed & ~forced, imp, -jnp.inf)
        row_f = j_s.astype(F32)
        for _ in range(SLC_TOPK - 3):
            mx = jnp.max(val, axis=0, keepdims=True)
            idx = jnp.min(jnp.where(val == mx, row_f, float(LANES)), axis=0, keepdims=True)
            pick = row_f == idx
            sel = jnp.where(pick, 1.0, sel)
            val = jnp.where(pick, -jnp.inf, val)
        m_neg = jnp.where(sel > 0.0, 0.0, MASK_NEG).astype(BF16)
        q_aug = jnp.concatenate([q, per_head(m_neg)], axis=0)
        return qb, q_aug, o_cmp, o_win

    ones_rows = jnp.ones((NSA_ONES_ROWS, tq), BF16)

    def flash_init():
        m_ref[...] = jnp.full(m_ref.shape, -jnp.inf, F32)
        acc_ref[...] = jnp.zeros_like(acc_ref)

    def flash_update(s, vt_tile):
        m_prev = m_ref[...]
        m_next = jnp.maximum(m_prev, col_max8(s))
        alpha = jnp.exp2(m_prev - m_next)
        p = jnp.exp2(s - rows8(m_next, tq)).astype(BF16)
        v_aug = jnp.concatenate([vt_tile, ones_rows], axis=0)
        acc_ref[...] = (acc_ref[...] * rows8(alpha, DH + NSA_ONES_ROWS)
                        + jnp.dot(v_aug, p, preferred_element_type=F32))
        m_ref[...] = m_next

    def flash_result():
        acc = acc_ref[...]
        return acc[:DH] * rows8(1.0 / acc[DH:DH + 8], DH)

    def back(sub, qb, q_aug, o_cmp, o_win):
        def slc_scores(kt):
            oh = oh_ref[pl.ds(pl.multiple_of(kt * tq, tq), tq), :]
            k_aug = jnp.concatenate([key_rows(ks_ref, kt), oh], axis=1)
            return jnp.dot(k_aug, q_aug, preferred_element_type=F32)

        flash_init()
        s_ref[...] = slc_scores(0)

        def slc_step(kt):
            s_cur = s_ref[...]
            s_ref[...] = slc_scores(kt + 1)
            flash_update(s_cur, vst_ref[0, kt])

        def slc_quad(i, carry):
            for u in range(4):
                slc_step(4 * i + u)
            return carry

        lax.fori_loop(0, lax.shift_right_logical(qb, 2), slc_quad, 0)
        done = qb & ~3

        @pl.when((qb & 2) != 0)
        def _():
            slc_step(done)
            slc_step(done + 1)

        @pl.when((qb & 1) != 0)
        def _():
            slc_step(qb - 1)

        flash_update(jnp.where(causal, s_ref[...], MASK_NEG), vst_ref[0, qb])
        o_slc = flash_result()

        head0 = pl.program_id(0) * R

        def gate(branch, r):
            logit = gate_ref[0, sub, pl.ds(branch * NSA_HEADS + head0 + r, 1), :]
            return jnp.broadcast_to(1.0 / (1.0 + jnp.exp(-logit)), (DH, tq))

        rows = slice(sub * tq, (sub + 1) * tq)
        for r in range(R):
            sl = slice(r * tq, (r + 1) * tq)
            g = [gate(b, r) for b in range(3)]
            o_t = g[0] * o_cmp[:, sl] + g[1] * o_slc[:, sl] + g[2] * o_win[:, sl]
            z = z_ref[rows, r * DH:(r + 1) * DH]
            o_ref[rows, r * DH:(r + 1) * DH] = (o_t.T * _silu(z)).astype(o_ref.dtype)

    fronts = [front(sub) for sub in range(n_sub)]
    for sub in range(n_sub):
        q_ref[sub] = project_q(x1_ref, cos1_ref, sin1_ref, sub)
    for sub in range(n_sub):
        back(sub, *fronts[sub])


def _nsa_attn(hb, w_in, rope_t, kc, vc_t, ksw, vsw_t, onehot, agg_t, gate_t, z):
    G, R, DH = NSA_KV_GROUPS, NSA_REP, NSA_HEAD_DIM
    S = ksw.shape[1]
    tq, n_sub = NSA_TQ, NSA_BLOCKS_PER_STEP
    assert S % (n_sub * tq) == 0 and NSA_WINDOW <= 2 * tq and S // SLC_BLOCK <= LANES
    ncp = kc.shape[1]
    n_t = S // tq
    K = hb.shape[1]
    cos_t, sin_t = rope_t
    rot = cos_t.shape[0]
    step = n_sub * tq
    nxt = lambda i: jnp.minimum(i + 1, S // step - 1)
    rows = lambda off: pl.BlockSpec((1, S, DH), lambda g, i, off=off: (off + g, 0, 0))
    tiles_t = lambda off: pl.BlockSpec((1, n_t, DH, tq), lambda g, i, off=off: (off + g, 0, 0, 0))
    return pl.pallas_call(
        functools.partial(_nsa_kernel, tq=tq, n_sub=n_sub, q_scale=DH ** -0.5 * LOG2_E),
        grid=(G, n_t // n_sub),
        in_specs=[pl.BlockSpec((step, K), lambda g, i: (0, 0)),
                  pl.BlockSpec((rot, step), lambda g, i: (0, 0)),
                  pl.BlockSpec((rot, step), lambda g, i: (0, 0)),
                  pl.BlockSpec((step, K), lambda g, i: (nxt(i), 0)),
                  pl.BlockSpec((rot, step), lambda g, i: (0, nxt(i))),
                  pl.BlockSpec((rot, step), lambda g, i: (0, nxt(i))),
                  pl.BlockSpec((K, R * DH), lambda g, i: (0, g)),
                  pl.BlockSpec((1, ncp, DH), lambda g, i: (g, 0, 0)),
                  pl.BlockSpec((1, DH, ncp), lambda g, i: (g, 0, 0)),
                  rows(0), tiles_t(0), rows(G), tiles_t(G),
                  pl.BlockSpec((S, LANES), lambda g, i: (0, 0)),
                  pl.BlockSpec((LANES, ncp), lambda g, i: (0, 0)),
                  pl.BlockSpec((1, n_sub, LANES, tq), lambda g, i: (0, i, 0, 0)),
                  pl.BlockSpec((n_sub * tq, R * DH), lambda g, i: (i, g))],
        out_specs=pl.BlockSpec((n_sub * tq, R * DH), lambda g, i: (i, g)),
        out_shape=jax.ShapeDtypeStruct((S, D_MODEL), BF16),
        scratch_shapes=[pltpu.VMEM((8, R * tq), F32),
                        pltpu.VMEM((DH + NSA_ONES_ROWS, R * tq), F32),
                        pltpu.VMEM((tq, R * tq), F32),
                        pltpu.VMEM((R * DH, K), BF16),
                        pltpu.VMEM((n_sub, DH, R * tq), BF16)],
        compiler_params=_params("parallel", "arbitrary"),
        name="nsa_attn",
    )(hb, cos_t, sin_t, hb, cos_t, sin_t, w_in, kc, vc_t, ksw, vsw_t, ksw, vsw_t, onehot, agg_t,
      gate_t, z)


def _swa_kernel(sink_ref, q_ref, kp_ref, kc_ref, vp_ref, vc_ref, z_ref, o_ref):
    W, R = SWA_WINDOW, SWA_REP
    n = pl.program_id(0)
    i_q = lax.broadcasted_iota(jnp.int32, (W, 2 * W), 0)
    c_k = lax.broadcasted_iota(jnp.int32, (W, 2 * W), 1)
    diff = i_q - (c_k - W)
    mask = (diff >= 0) & (diff < W) & ((n > 0) | (c_k >= W))
    lane = lax.broadcasted_iota(jnp.int32, (W, LANES), 1)
    low = lane < SWA_HEAD_DIM
    ones_v = jnp.ones((2 * W, LANES), BF16)
    for g in range(SWA_KV_HEADS):
        gl = slice(g * LANES, (g + 1) * LANES)
        kk = jnp.concatenate([kp_ref[:, gl], kc_ref[:, gl]], axis=0)
        vv = jnp.concatenate([vp_ref[:, gl], vc_ref[:, gl]], axis=0)
        qs = []
        for r in range(R):
            h = g * R + r
            q2 = q_ref[:, (h // 2) * LANES:(h // 2 + 1) * LANES]
            qs.append(jnp.where(low if h % 2 == 0 else ~low, q2, jnp.zeros_like(q2)))
        s_all = lax.dot_general(jnp.concatenate(qs, axis=0), kk, _NT,
                                preferred_element_type=F32)
        ps, sink_p = [], []
        for r in range(R):
            sink = sink_ref[g * R + r] * LOG2_E
            s = jnp.where(mask, s_all[r * W:(r + 1) * W], -jnp.inf)
            m = jnp.maximum(jnp.max(s, axis=1, keepdims=True), sink)
            ps.append(jnp.exp2(s - m).astype(BF16))
            sink_p.append(jnp.exp2(sink - m))
        o_all = jnp.dot(jnp.concatenate(ps, axis=0), jnp.concatenate([vv, ones_v], axis=1),
                        preferred_element_type=F32)

        def head_out(r):
            rows = slice(r * W, (r + 1) * W)
            return o_all[rows, :LANES] * (1.0 / (o_all[rows, LANES:] + sink_p[r]))

        for pr in range(R // 2):
            o_even, o_odd = head_out(2 * pr), head_out(2 * pr + 1)
            col = slice((g * R // 2 + pr) * LANES, (g * R // 2 + pr + 1) * LANES)
            o_ref[:, col] = (jnp.where(low, o_even, o_odd) * _silu(z_ref[:, col])).astype(o_ref.dtype)


def _swa_attn(sinks, q, kv_dup, z):
    S, D = q.shape
    W = SWA_WINDOW
    wide = SWA_KV_HEADS * LANES
    cur = lambda i: (i, 0)
    k_prev, k_cur = (lambda i: (jnp.maximum(i - 1, 0), 0)), cur
    v_prev, v_cur = (lambda i: (jnp.maximum(i - 1, 0), 1)), (lambda i: (i, 1))
    return pl.pallas_call(
        _swa_kernel,
        grid=(S // W,),
        in_specs=[pl.BlockSpec(memory_space=pltpu.SMEM),
                  pl.BlockSpec((W, D), cur),
                  pl.BlockSpec((W, wide), k_prev), pl.BlockSpec((W, wide), k_cur),
                  pl.BlockSpec((W, wide), v_prev), pl.BlockSpec((W, wide), v_cur),
                  pl.BlockSpec((W, D), cur)],
        out_specs=pl.BlockSpec((W, D), cur),
        out_shape=jax.ShapeDtypeStruct((S, D), BF16),
        compiler_params=_params("parallel"),
        name="swa_attn",
    )(sinks, q, kv_dup, kv_dup, kv_dup, kv_dup, z)


def _rope_tables(positions, head_dim, scale):
    rot = head_dim // ROPE_FRACTION
    half = rot // 2
    inv_freq = ROPE_THETA ** (-jnp.arange(0, rot, 2, dtype=F32) / rot)
    ang = positions.reshape(-1).astype(F32)[:, None] * inv_freq
    cos, sin = jnp.cos(ang), jnp.sin(ang)
    S = cos.shape[0]
    rest = head_dim - rot
    cos_h = jnp.concatenate([cos, cos, jnp.ones((S, rest), F32)], axis=1)
    sin_h = jnp.concatenate([-sin, sin, jnp.zeros((S, rest), F32)], axis=1)
    reps = LANES // head_dim
    return (jnp.tile(cos_h, (1, reps)) * scale, jnp.tile(sin_h, (1, reps)) * scale, half, head_dim)


def _nsa_constants(S):
    n_cmp = S // CMP_STRIDE - 1
    ncp = S // CMP_STRIDE
    n_slc = S // SLC_BLOCK
    ratio = SLC_BLOCK // CMP_STRIDE
    i = np.arange(ncp)[:, None]
    j = np.arange(LANES)[None, :]
    agg = ((i >= ratio * j - CMP_BLOCK // CMP_STRIDE + 1) & (i <= ratio * j + ratio - 1)
           & (i < n_cmp) & (j < n_slc))
    onehot = (np.arange(S)[:, None] // SLC_BLOCK) == j
    return jnp.asarray(onehot, dtype=BF16), jnp.asarray(agg.T, dtype=BF16)


def _rope_tables_t(positions, head_dim):
    rot = head_dim // ROPE_FRACTION
    inv_freq = ROPE_THETA ** (-jnp.arange(0, rot, 2, dtype=F32) / rot)
    ang = inv_freq[:, None] * positions.reshape(-1).astype(F32)[None, :]
    cos, sin = jnp.cos(ang), jnp.sin(ang)
    return jnp.concatenate([cos, cos], axis=0), jnp.concatenate([-sin, sin], axis=0)


def _hgrn_layer(hb, w_in, g_norm, log_lb, log_1m_lb):
    return _hgrn_mixer(hb, w_in, log_lb, log_1m_lb, g_norm)


def _nsa_layer(hb, positions, w_in, pos_k, w1_k, w2_k, pos_v, w1_v, w2_v):
    S = hb.shape[0]
    H, G, DH = NSA_HEADS, NSA_KV_GROUPS, NSA_HEAD_DIM
    gw = G * DH
    assert gw == 512 and (H * DH) % gw == 0
    t_kc = H * DH // gw
    o_gate = H * DH + 6 * gw
    o_z = o_gate + 3 * H
    rope_k = _rope_tables(positions, DH, 1.0)
    tq = NSA_TQ

    wide = 2 * gw
    rope_t = _rope_tables_t(positions, DH)
    kv_c = _mm(hb, w_in, cols=(t_kc, 1, 2), out_dtype=F32, rope=rope_k, rope_tiles=1,
               head_major=True)
    ksw = _mm(hb, w_in, cols=(t_kc + 2, 2, 2), out_dtype=BF16, rope=rope_k, head_major=True)
    vsw_t = _mm_t(hb, w_in, cols=(t_kc + 3, 2, 2), out_dtype=BF16, tk=tq)
    z = _mm(hb, w_in[:, o_z:], out_dtype=F32, tn=wide)
    w_gate = jnp.pad(w_in[:, o_gate:o_z], ((0, 0), (0, LANES - 3 * H)))
    gate_t = _mm_t(hb, w_gate, out_dtype=F32, tk=tq, tn=LANES)

    kc = _nsa_compress(kv_c, 0, pos_k, w1_k, w2_k, transposed=False)
    vc_t = _nsa_compress(kv_c, G, pos_v, w1_v, w2_v, transposed=True)
    onehot, agg_t = _nsa_constants(S)
    return _nsa_attn(hb, w_in, rope_t, kc, vc_t, ksw, vsw_t, onehot, agg_t, gate_t, z)


def _swa_layer(hb, positions, w_in, sinks):
    H, KV, DH = SWA_HEADS, SWA_KV_HEADS, SWA_HEAD_DIM
    o_k = H * DH
    o_v = o_k + KV * DH
    o_z = o_v + KV * DH
    tn = 512
    assert o_k % tn == 0 and o_z % tn == 0
    dup = np.concatenate([np.tile(np.arange(g * DH, (g + 1) * DH), 2) for g in range(KV)])
    rope_q = _rope_tables(positions, DH, DH ** -0.5 * LOG2_E)
    rope_k = _rope_tables(positions, DH, 1.0)
    q = _mm(hb, w_in, cols=(0, 1, o_k // (2 * tn)), out_dtype=BF16, rope=rope_q, tn=2 * tn)
    w_kv = jnp.concatenate([w_in[:, o_k:o_v][:, dup], w_in[:, o_v:o_z][:, dup]], axis=1)
    kv_dup = _mm(hb, w_kv, out_dtype=BF16, rope=rope_k, rope_tiles=1, tn=tn)
    z = _mm(hb, w_in, cols=(o_z // tn, 1, o_k // tn), out_dtype=F32, tn=tn)
    return _swa_attn(sinks, q, kv_dup, z)


def kernel(x, positions, hgrn_lb_logits, l0_w_in, l0_g_norm, l0_w_out, l0_ln_g, l0_ln_b, l1_w_in, l1_cmp_pos_k, l1_cmp_w1_k, l1_cmp_w2_k, l1_cmp_pos_v, l1_cmp_w1_v, l1_cmp_w2_v, l1_w_out, l1_ln_g, l1_ln_b, l2_w_in, l2_sinks, l2_w_out, l2_ln_g, l2_ln_b, l3_w_in, l3_g_norm, l3_w_out, l3_ln_g, l3_ln_b):
    B, S, D = x.shape
    assert B == 1 and D == D_MODEL
    lb = jnp.cumsum(jax.nn.softmax(hgrn_lb_logits.astype(F32), axis=0), axis=0)
    lb = lb - lb[0:1]
    log_lb, log_1m_lb = jnp.log(lb), jnp.log1p(-lb)

    h = x.reshape(S, D)
    hb = h.astype(BF16)

    a = _hgrn_layer(hb, l0_w_in, l0_g_norm, log_lb[0], log_1m_lb[0])
    h, hb = _outproj_ln(a, l0_w_out.astype(BF16), h, l0_ln_g, l0_ln_b)

    a = _nsa_layer(hb, positions, l1_w_in, l1_cmp_pos_k, l1_cmp_w1_k, l1_cmp_w2_k,
                   l1_cmp_pos_v, l1_cmp_w1_v, l1_cmp_w2_v)
    h, hb = _outproj_ln(a, l1_w_out.astype(BF16), h, l1_ln_g, l1_ln_b)

    a = _swa_layer(hb, positions, l2_w_in, l2_sinks)
    h, hb = _outproj_ln(a, l2_w_out.astype(BF16), h, l2_ln_g, l2_ln_b)

    a = _hgrn_layer(hb, l3_w_in, l3_g_norm, log_lb[1], log_1m_lb[1])
    h, hb = _outproj_ln(a, l3_w_out.astype(BF16), h, l3_ln_g, l3_ln_b)
    return h.reshape(B, S, D)
```

```python
import functools

import numpy as np
import jax
import jax.numpy as jnp
from jax import lax
from jax.experimental import pallas as pl
from jax.experimental.pallas import tpu as pltpu

F32 = jnp.float32
BF16 = jnp.bfloat16

D_MODEL = 2048
DEPTH = 4
DEEPNORM_ALPHA = (2 * DEPTH) ** 0.25
LN_EPS = 1e-5
RMS_EPS = 1e-6
ROPE_THETA = 500000.0
ROPE_FRACTION = 4

HG_HEAD_DIM = 128
HG_HEADS = D_MODEL // HG_HEAD_DIM
HG_CHUNK = 64
HG_SUB = 8
HG_TIME_BLOCK = 2048
HG_PIECE = 512
LOG2_E = 1.4426950408889634

NSA_HEAD_DIM = 128
NSA_HEADS = D_MODEL // NSA_HEAD_DIM
NSA_KV_GROUPS = 4
NSA_REP = NSA_HEADS // NSA_KV_GROUPS
CMP_BLOCK = 32
CMP_STRIDE = 16
CMP_HIDDEN = 256
SLC_BLOCK = 64
SLC_TOPK = 16
NSA_WINDOW = 512
NSA_TQ = 256
NSA_ONES_ROWS = 16
NSA_BLOCKS_PER_STEP = 4
FORCE_BONUS = 1.0e4
MASK_NEG = -1.0e30

SWA_HEAD_DIM = 64
SWA_HEADS = D_MODEL // SWA_HEAD_DIM
SWA_KV_HEADS = 4
SWA_REP = SWA_HEADS // SWA_KV_HEADS
SWA_WINDOW = 128

LANES = 128
VMEM_LIMIT_BYTES = 48 * 1024 * 1024

_NT = (((1,), (1,)), ((), ()))
_TN = (((0,), (0,)), ((), ()))


def _params(*sem):
    return pltpu.CompilerParams(dimension_semantics=sem, vmem_limit_bytes=VMEM_LIMIT_BYTES)


def _silu(x):
    return x * (1.0 / (1.0 + jnp.exp(-x)))


def _mm_kernel(*refs, rope_half, rope_period, rope_tiles, head_major, n_chunks):
    if rope_half:
        x_ref, w_ref, c_ref, s_ref, o_ref, wb_ref = refs
    else:
        x_ref, w_ref, o_ref, wb_ref = refs

    @pl.when(pl.program_id(1) == 0)
    def _():
        wb_ref[...] = w_ref[...].astype(BF16)

    acc = jnp.dot(x_ref[...], wb_ref[...], preferred_element_type=F32)

    def write(with_rope):
        if with_rope:
            cos = c_ref[...]
            sin = s_ref[...]
            lane = lax.broadcasted_iota(jnp.int32, cos.shape, 1)
            first_half = (lane & (rope_period - 1)) < rope_half
        for j in range(n_chunks):
            a = acc[:, j * LANES:(j + 1) * LANES]
            if with_rope:
                up = pltpu.roll(a, LANES - rope_half, 1)
                dn = pltpu.roll(a, rope_half, 1)
                a = a * cos + jnp.where(first_half, up, dn) * sin
            if head_major:
                o_ref[j] = a.astype(o_ref.dtype)
            else:
                o_ref[:, j * LANES:(j + 1) * LANES] = a.astype(o_ref.dtype)

    if rope_half and rope_tiles is not None:
        pl.when(pl.program_id(0) < rope_tiles)(functools.partial(write, True))
        pl.when(pl.program_id(0) >= rope_tiles)(functools.partial(write, False))
    else:
        write(bool(rope_half))


def _col_tiles(w, cols, tn):
    return (0, 1, w.shape[1] // tn) if cols is None else cols


def _mm(x, w, *, out_dtype, cols=None, rope=None, rope_tiles=None, head_major=False, tn=512):
    S, K = x.shape
    tm = min(1024, S)
    first, stride, n_tiles = _col_tiles(w, cols, tn)
    N = n_tiles * tn
    n_chunks = tn // LANES
    in_specs = [pl.BlockSpec((tm, K), lambda j, i: (i, 0)),
                pl.BlockSpec((K, tn), lambda j, i: (0, first + stride * j))]
    args = [x, w]
    half = period = 0
    if rope is not None:
        cos_t, sin_t, half, period = rope
        in_specs += [pl.BlockSpec((tm, LANES), lambda j, i: (i, 0))] * 2
        args += [cos_t, sin_t]
    if head_major:
        out_shape = jax.ShapeDtypeStruct((N // LANES, S, LANES), out_dtype)
        out_spec = pl.BlockSpec((n_chunks, tm, LANES), lambda j, i: (j, i, 0))
    else:
        out_shape = jax.ShapeDtypeStruct((S, N), out_dtype)
        out_spec = pl.BlockSpec((tm, tn), lambda j, i: (i, j))
    return pl.pallas_call(
        functools.partial(_mm_kernel, rope_half=half, rope_period=period, rope_tiles=rope_tiles,
                          head_major=head_major, n_chunks=n_chunks),
        grid=(n_tiles, S // tm),
        in_specs=in_specs,
        out_specs=out_spec,
        out_shape=out_shape,
        scratch_shapes=[pltpu.VMEM((K, tn), BF16)],
        compiler_params=_params("arbitrary", "arbitrary"),
        name="in_proj",
    )(*args)


def _mm_t_kernel(w_ref, x_ref, o_ref, wt_ref, *, n_heads, n_tiles, tk):
    @pl.when(pl.program_id(1) == 0)
    def _():
        wt_ref[...] = w_ref[...].T.astype(BF16)

    acc = lax.dot_general(wt_ref[...], x_ref[...], _NT, preferred_element_type=F32)
    for c in range(n_heads):
        for b in range(n_tiles):
            o_ref[c, b] = acc[c * LANES:(c + 1) * LANES, b * tk:(b + 1) * tk].astype(o_ref.dtype)


def _mm_t(x, w, *, out_dtype, tk, cols=None, tn=512):
    S, K = x.shape
    tm = min(1024, S)
    first, stride, n_tiles = _col_tiles(w, cols, tn)
    N = n_tiles * tn
    return pl.pallas_call(
        functools.partial(_mm_t_kernel, n_heads=tn // LANES, n_tiles=tm // tk, tk=tk),
        grid=(n_tiles, S // tm),
        in_specs=[pl.BlockSpec((K, tn), lambda j, i: (0, first + stride * j)),
                  pl.BlockSpec((tm, K), lambda j, i: (i, 0))],
        out_specs=pl.BlockSpec((tn // LANES, tm // tk, LANES, tk), lambda j, i: (j, i, 0, 0)),
        out_shape=jax.ShapeDtypeStruct((N // LANES, S // tk, LANES, tk), out_dtype),
        scratch_shapes=[pltpu.VMEM((tn, K), BF16)],
        compiler_params=_params("arbitrary", "arbitrary"),
        name="in_proj_t",
    )(w, x)


def _outproj_ln_kernel(a_ref, w_ref, h_ref, g_ref, b_ref, o_ref, ob_ref):
    y = jnp.dot(a_ref[...], w_ref[...], preferred_element_type=F32)
    u = DEEPNORM_ALPHA * h_ref[...] + y
    mu = jnp.mean(u, axis=-1, keepdims=True)
    xc = u - mu
    var = jnp.mean(xc * xc, axis=-1, keepdims=True)
    out = xc * lax.rsqrt(var + LN_EPS) * g_ref[...] + b_ref[...]
    o_ref[...] = out
    ob_ref[...] = out.astype(BF16)


def _outproj_ln(a, w, h, g, b):
    S, D = h.shape
    tm = min(512, S)
    row = pl.BlockSpec((tm, D), lambda i: (i, 0))
    vec = pl.BlockSpec((1, D), lambda i: (0, 0))
    return pl.pallas_call(
        _outproj_ln_kernel,
        grid=(S // tm,),
        in_specs=[row, pl.BlockSpec((D, D), lambda i: (0, 0)), row, vec, vec],
        out_specs=[row, row],
        out_shape=[jax.ShapeDtypeStruct((S, D), F32), jax.ShapeDtypeStruct((S, D), BF16)],
        compiler_params=_params("parallel"),
        name="outproj_ln",
    )(a, w, h, g.reshape(1, D), b.reshape(1, D))


def _hgrn_kernel(x_ref, wq_ref, wf_ref, wv_ref, wz_ref, llb_ref, l1m_ref, gn_ref,
                 o_ref, st_ref, u_ref, wb_ref, *, n_pieces, piece):
    C, SUB, DK = HG_CHUNK, HG_SUB, HG_HEAD_DIM
    n_lvl = (C // (2 * SUB)).bit_length()
    n_pc = piece // C

    @pl.when(pl.program_id(1) == 0)
    def _():
        st_ref[...] = jnp.zeros_like(st_ref)
        for i, w_ref in enumerate((wq_ref, wf_ref, wv_ref, wz_ref)):
            wb_ref[:, i * DK:(i + 1) * DK] = w_ref[...].astype(BF16)

    r = lax.broadcasted_iota(jnp.int32, (C, C), 0)
    c = lax.broadcasted_iota(jnp.int32, (C, C), 1)
    causal = c <= r
    halves = [SUB << lvl for lvl in range(n_lvl)]
    cum_rows = []
    for h in halves:
        same_h = (r // h) == (c // h)
        cum_rows += [causal & same_h, same_h]
    cum_one = jnp.concatenate([m.astype(F32) for m in cum_rows], axis=0).astype(BF16)
    cum_mat = jnp.concatenate([cum_one] * 3, axis=1)
    diag_mask = causal & ((r // SUB) == (c // SUB))
    log_lb = llb_ref[...]
    log_1m_lb = l1m_ref[...]
    lane_c = lax.broadcasted_iota(jnp.int32, (SUB, C), 1)
    chunks = [slice(ci * C, (ci + 1) * C) for ci in range(n_pc)]

    def rows_to_lanes(x):
        return jnp.concatenate([x[sl] for sl in chunks], axis=1)

    def lanes_to_rows(x):
        return jnp.concatenate([x[:, ci * DK:(ci + 1) * DK] for ci in range(n_pc)], axis=0)

    def placed(x, lo):
        parts = [jnp.zeros((lo, DK), F32), x, jnp.zeros((C - lo - x.shape[0], DK), F32)]
        return jnp.concatenate([p for p in parts if p.shape[0]], axis=0)

    def front(pi):
        rows = slice(pi * piece, (pi + 1) * piece)
        proj = jnp.dot(x_ref[rows, :], wb_ref[...], preferred_element_type=F32)
        q, fl, v, z = (proj[:, i * DK:(i + 1) * DK] for i in range(4))
        vb = v.astype(BF16)
        log_sig = jnp.minimum(fl, 0.0) - jnp.log(1.0 + jnp.exp(-jnp.abs(fl)))
        t = log_1m_lb + log_sig
        log_f = jnp.maximum(log_lb, t) + jnp.log(1.0 + jnp.exp(-jnp.abs(log_lb - t)))
        log2_k = (t - fl) * LOG2_E
        k = jnp.exp2(log2_k)
        lf = rows_to_lanes(log_f)
        lf_hi = lf.astype(BF16)
        lf_r1 = lf - lf_hi.astype(F32)
        lf_mid = lf_r1.astype(BF16)
        lf_lo = (lf_r1 - lf_mid.astype(F32)).astype(BF16)
        cums = jnp.dot(cum_mat, jnp.concatenate([lf_hi, lf_mid, lf_lo], axis=0),
                       preferred_element_type=F32) * LOG2_E
        lvl = [cums[i * C:(i + 1) * C] for i in range(2 * n_lvl)]
        loc_top, tot_top = lvl[-2], lvl[-1]
        tot_chunk = tot_top[:C // 2] + tot_top[C // 2:]
        cum = [lanes_to_rows(a) for a in [
            jnp.concatenate([loc_top[:C // 2], loc_top[C // 2:] + tot_top[:C // 2]], axis=0),
            jnp.concatenate([tot_chunk, tot_chunk], axis=0)] + lvl]
        b_full, b_tot = cum[0], cum[1]
        qe = (q * jnp.exp2(b_full)).astype(BF16)
        kd = (k * jnp.exp2(b_tot - b_full)).astype(BF16)
        q_lvl = [q * jnp.exp2(cum[2 + 2 * l]) for l in range(n_lvl)]
        k_lvl = [k * jnp.exp2(cum[3 + 2 * l] - cum[2 + 2 * l]) for l in range(n_lvl)]
        b_loc = cum[2]
        c_row = b_loc - log2_k

        a_offs, a_diags = [], []
        for ci, ch in enumerate(chunks):
            blocks = []
            for j in range(C // SUB):
                base = ci * C + j * SUB
                sl = slice(base, base + SUB)
                blk = jnp.zeros((SUB, C), F32)
                for s in range(SUB):
                    y = q[sl] * jnp.exp2(jnp.minimum(
                        b_loc[sl] - c_row[base + s:base + s + 1, :], log2_k[base + s:base + s + 1, :]))
                    blk = jnp.where(lane_c == j * SUB + s, jnp.sum(y, axis=-1, keepdims=True), blk)
                blocks.append(blk)
            a_diags.append(jnp.concatenate(blocks, axis=0))
            lhs, rhs = [], []
            for l, h in enumerate(halves):
                q_c, k_c = q_lvl[l][ch], k_lvl[l][ch]
                for lo in range(0, C, 2 * h):
                    lhs.append(placed(q_c[lo + h:lo + 2 * h], lo + h))
                    rhs.append(placed(k_c[lo:lo + h], lo))
            a_offs.append(lax.dot_general(jnp.concatenate(lhs, axis=1).astype(BF16),
                                          jnp.concatenate(rhs, axis=1).astype(BF16),
                                          _NT, preferred_element_type=F32))
            u_ref[pi * n_pc + ci] = lax.dot_general(vb[ch], kd[ch], _TN,
                                                    preferred_element_type=F32)
        per_chunk = []
        for ci, ch in enumerate(chunks):
            a = jnp.where(diag_mask, a_diags[ci], 0.0) + a_offs[ci]
            o_intra = jnp.dot(a.astype(BF16), vb[ch], preferred_element_type=F32)
            per_chunk.append((qe[ch], o_intra, b_tot[ci * C:ci * C + 8]))
        return per_chunk, z

    fronts = [front(pi) for pi in range(n_pieces)]

    st = st_ref[...]
    outs = []
    for pi, (per_chunk, _) in enumerate(fronts):
        for ci, (qe_c, o_intra, b_tot8) in enumerate(per_chunk):
            outs.append(o_intra + lax.dot_general(qe_c, st.astype(BF16), _NT,
                                                  preferred_element_type=F32))
            st = st * jnp.exp2(jnp.tile(b_tot8, (DK // 8, 1))) + u_ref[pi * n_pc + ci]
    st_ref[...] = st

    o = jnp.concatenate(outs, axis=0)
    z = jnp.concatenate([z_p for _, z_p in fronts], axis=0)
    ms = jnp.mean(o * o, axis=-1, keepdims=True)
    out = o * lax.rsqrt(ms + RMS_EPS) * gn_ref[...] * _silu(z)
    o_ref[...] = out.astype(o_ref.dtype)


def _hgrn_mixer(hb, w_in, log_lb, log_1m_lb, g_norm):
    S, K = hb.shape
    D, DK, H = D_MODEL, HG_HEAD_DIM, HG_HEADS
    T = min(HG_TIME_BLOCK, S)
    piece = min(HG_PIECE, T)

    def w_col(off):
        return pl.BlockSpec((K, DK), lambda h, t, off=off: (0, off + h))

    vec = pl.BlockSpec((1, DK), lambda h, t: (0, h))
    return pl.pallas_call(
        functools.partial(_hgrn_kernel, n_pieces=T // piece, piece=piece),
        grid=(H, S // T),
        in_specs=[pl.BlockSpec((T, K), lambda h, t: (t, 0)),
                  w_col(0), w_col(H), w_col(2 * H), w_col(3 * H), vec, vec, vec],
        out_specs=pl.BlockSpec((T, DK), lambda h, t: (t, h)),
        out_shape=jax.ShapeDtypeStruct((S, D), BF16),
        scratch_shapes=[pltpu.VMEM((DK, DK), F32),
                        pltpu.VMEM((T // HG_CHUNK, DK, DK), F32),
                        pltpu.VMEM((K, 4 * DK), BF16)],
        compiler_params=_params("parallel", "arbitrary"),
        name="hgrn_mixer",
    )(hb, w_in, w_in, w_in, w_in, log_lb.reshape(1, D), log_1m_lb.reshape(1, D),
      g_norm.reshape(1, D))


def _nsa_compress_kernel(x_ref, pos_ref, w1_ref, w2_ref, o_ref, *, transposed):
    DH = NSA_HEAD_DIM
    n = x_ref.shape[0] // CMP_STRIDE
    h1 = jnp.zeros((n, CMP_HIDDEN), F32)
    h2 = jnp.zeros((n, CMP_HIDDEN), F32)
    for l in range(CMP_STRIDE):
        x_l = x_ref[pl.ds(l, n, stride=CMP_STRIDE), :]
        l2 = CMP_STRIDE + l
        h1 = h1 + jnp.dot((x_l + pos_ref[l:l + 1, :]).astype(BF16),
                          w1_ref[l * DH:(l + 1) * DH, :].astype(BF16), preferred_element_type=F32)
        h2 = h2 + jnp.dot((x_l + pos_ref[l2:l2 + 1, :]).astype(BF16),
                          w1_ref[l2 * DH:(l2 + 1) * DH, :].astype(BF16), preferred_element_type=F32)
    hid = h1 + pltpu.roll(h2, n - 1, 0)
    act = _silu(hid).astype(BF16)
    if transposed:
        o_ref[...] = lax.dot_general(w2_ref[...].T.astype(BF16), act, _NT,
                                     preferred_element_type=F32).astype(o_ref.dtype)
    else:
        o_ref[...] = jnp.dot(act, w2_ref[...].astype(BF16),
                             preferred_element_type=F32).astype(o_ref.dtype)


def _nsa_compress(x, first, pos, w1, w2, *, transposed):
    G = NSA_KV_GROUPS
    _, S, DH = x.shape
    n = S // CMP_STRIDE
    whole = lambda a: pl.BlockSpec(a.shape, lambda g: (0,) * a.ndim)
    out_block = (None, DH, n) if transposed else (None, n, DH)
    return pl.pallas_call(
        functools.partial(_nsa_compress_kernel, transposed=transposed),
        grid=(G,),
        in_specs=[pl.BlockSpec((None, S, DH), lambda g: (first + g, 0, 0)),
                  whole(pos), whole(w1), whole(w2)],
        out_specs=pl.BlockSpec(out_block, lambda g: (g, 0, 0)),
        out_shape=jax.ShapeDtypeStruct((G,) + out_block[1:], BF16),
        compiler_params=_params("parallel"),
        name="nsa_compress",
    )(x, pos, w1, w2)


def _nsa_kernel(x0_ref, cos0_ref, sin0_ref, x1_ref, cos1_ref, sin1_ref, wq_ref, kc_ref, vct_ref,
                ks_ref, vst_ref, kw_ref, vwt_ref, oh_ref, aggt_ref, gate_ref, z_ref, o_ref,
                m_ref, acc_ref, s_ref, wqt_ref, q_ref, *, tq, n_sub, q_scale):
    R, DH = NSA_REP, NSA_HEAD_DIM
    ncp = kc_ref.shape[1]
    rot = cos0_ref.shape[0]

    def project_q(xr, cr, sr, sub):
        t_sl = slice(sub * tq, (sub + 1) * tq)
        q_t = lax.dot_general(wqt_ref[...], xr[t_sl, :], _NT,
                              preferred_element_type=F32) * q_scale
        cos, sin = cr[:, t_sl], sr[:, t_sl]
        heads = []
        for r in range(R):
            a = q_t[r * DH:(r + 1) * DH]
            swapped = jnp.concatenate([a[rot // 2:rot], a[:rot // 2]], axis=0)
            heads.append(jnp.concatenate([a[:rot] * cos + swapped * sin, a[rot:]],
                                         axis=0).astype(BF16))
        return jnp.concatenate(heads, axis=1)

    @pl.when(pl.program_id(1) == 0)
    def _():
        wqt_ref[...] = wq_ref[...].T.astype(BF16)
        for sub in range(n_sub):
            q_ref[sub] = project_q(x0_ref, cos0_ref, sin0_ref, sub)

    def per_head(x):
        return jnp.concatenate([x] * R, axis=1)

    def rows8(x, n):
        return jnp.tile(x, (n // 8, 1))

    def col_max8(x):
        mx = x[0:8]
        for i in range(1, x.shape[0] // 8):
            mx = jnp.maximum(mx, x[8 * i:8 * i + 8])
        for shift in (4, 2, 1):
            mx = jnp.maximum(mx, pltpu.roll(mx, shift, 0))
        return mx

    def key_rows(ref, kt):
        return ref[0, pl.ds(pl.multiple_of(kt * tq, tq), tq), :]

    c_k = lax.broadcasted_iota(jnp.int32, (tq, tq), 0)
    i_q = lax.broadcasted_iota(jnp.int32, (tq, tq), 1)
    causal = per_head(c_k <= i_q)
    aggt = aggt_ref[...]

    def front(sub):
        qb = pl.program_id(1) * n_sub + sub
        start = qb * tq
        q = q_ref[sub]

        n_c = lax.broadcasted_iota(jnp.int32, (ncp, tq), 0)
        t_c = start + lax.broadcasted_iota(jnp.int32, (ncp, tq), 1)
        cmask = per_head((n_c * CMP_STRIDE + CMP_BLOCK - 1) <= t_c)
        s = jnp.where(cmask, jnp.dot(kc_ref[0], q, preferred_element_type=F32), -jnp.inf)
        m = jnp.max(s, axis=0, keepdims=True)
        m = jnp.where(m == -jnp.inf, 0.0, m)
        e = jnp.exp2(s - m)
        p = e * (1.0 / jnp.maximum(jnp.sum(e, axis=0, keepdims=True), 1e-30))
        o_cmp = jnp.dot(vct_ref[0], p.astype(BF16), preferred_element_type=F32)
        p_sum = p[:, :tq]
        for r in range(1, R):
            p_sum = p_sum + p[:, r * tq:(r + 1) * tq]

        no_old = jnp.where(qb >= 2, 0, NSA_WINDOW)
        no_mid = jnp.where(qb >= 1, 0, NSA_WINDOW)
        old_ok = per_head((2 * tq + i_q - c_k + no_old) < NSA_WINDOW)
        mid_ok = per_head((tq + i_q - c_k + no_mid) < NSA_WINDOW)
        kt_old, kt_mid = jnp.maximum(qb - 2, 0), jnp.maximum(qb - 1, 0)
        k_win = jnp.concatenate([key_rows(kw_ref, kt_old), key_rows(kw_ref, kt_mid),
                                 key_rows(kw_ref, qb)], axis=0)
        s_w = jnp.where(jnp.concatenate([old_ok, mid_ok, causal], axis=0),
                        jnp.dot(k_win, q, preferred_element_type=F32), MASK_NEG)
        p_w = jnp.exp2(s_w - rows8(col_max8(s_w), 3 * tq)).astype(BF16)
        v_win = jnp.concatenate([vwt_ref[0, kt_old], vwt_ref[0, kt_mid], vwt_ref[0, qb]], axis=1)
        acc_w = jnp.dot(jnp.concatenate([v_win, jnp.ones((NSA_ONES_ROWS, 3 * tq), BF16)], axis=0),
                        p_w, preferred_element_type=F32)
        o_win = acc_w[:DH] * rows8(1.0 / acc_w[DH:DH + 8], DH)

        p_hi = p_sum.astype(BF16)
        p_lo = (p_sum - p_hi.astype(F32)).astype(BF16)
        imp = (jnp.dot(aggt, p_hi, preferred_element_type=F32)
               + jnp.dot(aggt, p_lo, preferred_element_type=F32))
        j_s = lax.broadcasted_iota(jnp.int32, (LANES, tq), 0)
        t_s = start + lax.broadcasted_iota(jnp.int32, (LANES, tq), 1)
        cur = lax.shift_right_logical(t_s, 6)
        forced = (j_s == 0) | (j_s == cur) | (j_s == cur - 1)
        allowed = j_s * SLC_BLOCK <= t_s
        sel = jnp.where(forced & allowed, 1.0, 0.0)
        val = jnp.where(allowed & ~forced, imp, -jnp.inf)
        row_f = j_s.astype(F32)
        for _ in range(SLC_TOPK - 3):
            mx = jnp.max(val, axis=0, keepdims=True)
            idx = jnp.min(jnp.where(val == mx, row_f, float(LANES)), axis=0, keepdims=True)
            pick = row_f == idx
            sel = jnp.where(pick, 1.0, sel)
            val = jnp.where(pick, -jnp.inf, val)
        m_neg = jnp.where(sel > 0.0, 0.0, MASK_NEG).astype(BF16)
        q_aug = jnp.concatenate([q, per_head(m_neg)], axis=0)
        return qb, q_aug, o_cmp, o_win

    ones_rows = jnp.ones((NSA_ONES_ROWS, tq), BF16)

    def flash_init():
        m_ref[...] = jnp.full(m_ref.shape, -jnp.inf, F32)
        acc_ref[...] = jnp.zeros_like(acc_ref)

    def flash_update(s, vt_tile):
        m_prev = m_ref[...]
        m_next = jnp.maximum(m_prev, col_max8(s))
        alpha = jnp.exp2(m_prev - m_next)
        p = jnp.exp2(s - rows8(m_next, tq)).astype(BF16)
        v_aug = jnp.concatenate([vt_tile, ones_rows], axis=0)
        acc_ref[...] = (acc_ref[...] * rows8(alpha, DH + NSA_ONES_ROWS)
                        + jnp.dot(v_aug, p, preferred_element_type=F32))
        m_ref[...] = m_next

    def flash_result():
        acc = acc_ref[...]
        return acc[:DH] * rows8(1.0 / acc[DH:DH + 8], DH)

    def back(sub, qb, q_aug, o_cmp, o_win):
        def slc_scores(kt):
            oh = oh_ref[pl.ds(pl.multiple_of(kt * tq, tq), tq), :]
            k_aug = jnp.concatenate([key_rows(ks_ref, kt), oh], axis=1)
            return jnp.dot(k_aug, q_aug, preferred_element_type=F32)

        flash_init()
        s_ref[...] = slc_scores(0)

        def slc_step(kt):
            s_cur = s_ref[...]
            s_ref[...] = slc_scores(kt + 1)
            flash_update(s_cur, vst_ref[0, kt])

        def slc_quad(i, carry):
            for u in range(4):
                slc_step(4 * i + u)
            return carry

        lax.fori_loop(0, lax.shift_right_logical(qb, 2), slc_quad, 0)
        done = qb & ~3

        @pl.when((qb & 2) != 0)
        def _():
            slc_step(done)
            slc_step(done + 1)

        @pl.when((qb & 1) != 0)
        def _():
            slc_step(qb - 1)

        flash_update(jnp.where(causal, s_ref[...], MASK_NEG), vst_ref[0, qb])
        o_slc = flash_result()

        head0 = pl.program_id(0) * R

        def gate(branch, r):
            logit = gate_ref[0, sub, pl.ds(branch * NSA_HEADS + head0 + r, 1), :]
            return jnp.broadcast_to(1.0 / (1.0 + jnp.exp(-logit)), (DH, tq))

        rows = slice(sub * tq, (sub + 1) * tq)
        for r in range(R):
            sl = slice(r * tq, (r + 1) * tq)
            g = [gate(b, r) for b in range(3)]
            o_t = g[0] * o_cmp[:, sl] + g[1] * o_slc[:, sl] + g[2] * o_win[:, sl]
            z = z_ref[rows, r * DH:(r + 1) * DH]
            o_ref[rows, r * DH:(r + 1) * DH] = (o_t.T * _silu(z)).astype(o_ref.dtype)

    fronts = [front(sub) for sub in range(n_sub)]
    for sub in range(n_sub):
        q_ref[sub] = project_q(x1_ref, cos1_ref, sin1_ref, sub)
    for sub in range(n_sub):
        back(sub, *fronts[sub])


def _nsa_attn(hb, w_in, rope_t, kc, vc_t, ksw, vsw_t, onehot, agg_t, gate_t, z):
    G, R, DH = NSA_KV_GROUPS, NSA_REP, NSA_HEAD_DIM
    S = ksw.shape[1]
    tq, n_sub = NSA_TQ, NSA_BLOCKS_PER_STEP
    assert S % (n_sub * tq) == 0 and NSA_WINDOW <= 2 * tq and S // SLC_BLOCK <= LANES
    ncp = kc.shape[1]
    n_t = S // tq
    K = hb.shape[1]
    cos_t, sin_t = rope_t
    rot = cos_t.shape[0]
    step = n_sub * tq
    nxt = lambda i: jnp.minimum(i + 1, S // step - 1)
    once = pl.Buffered(1)
    rows = lambda off: pl.BlockSpec((1, S, DH), lambda g, i, off=off: (off + g, 0, 0),
                                    pipeline_mode=once)
    tiles_t = lambda off: pl.BlockSpec((1, n_t, DH, tq), lambda g, i, off=off: (off + g, 0, 0, 0),
                                       pipeline_mode=once)
    return pl.pallas_call(
        functools.partial(_nsa_kernel, tq=tq, n_sub=n_sub, q_scale=DH ** -0.5 * LOG2_E),
        grid=(G, n_t // n_sub),
        in_specs=[pl.BlockSpec((step, K), lambda g, i: (0, 0), pipeline_mode=once),
                  pl.BlockSpec((rot, step), lambda g, i: (0, 0), pipeline_mode=once),
                  pl.BlockSpec((rot, step), lambda g, i: (0, 0), pipeline_mode=once),
                  pl.BlockSpec((step, K), lambda g, i: (nxt(i), 0)),
                  pl.BlockSpec((rot, step), lambda g, i: (0, nxt(i))),
                  pl.BlockSpec((rot, step), lambda g, i: (0, nxt(i))),
                  pl.BlockSpec((K, R * DH), lambda g, i: (0, g), pipeline_mode=once),
                  pl.BlockSpec((1, ncp, DH), lambda g, i: (g, 0, 0), pipeline_mode=once),
                  pl.BlockSpec((1, DH, ncp), lambda g, i: (g, 0, 0), pipeline_mode=once),
                  rows(0), tiles_t(0), rows(G), tiles_t(G),
                  pl.BlockSpec((S, LANES), lambda g, i: (0, 0), pipeline_mode=once),
                  pl.BlockSpec((LANES, ncp), lambda g, i: (0, 0), pipeline_mode=once),
                  pl.BlockSpec((1, n_sub, LANES, tq), lambda g, i: (0, i, 0, 0)),
                  pl.BlockSpec((n_sub * tq, R * DH), lambda g, i: (i, g))],
        out_specs=pl.BlockSpec((n_sub * tq, R * DH), lambda g, i: (i, g)),
        out_shape=jax.ShapeDtypeStruct((S, D_MODEL), BF16),
        scratch_shapes=[pltpu.VMEM((8, R * tq), F32),
                        pltpu.VMEM((DH + NSA_ONES_ROWS, R * tq), F32),
                        pltpu.VMEM((tq, R * tq), F32),
                        pltpu.VMEM((R * DH, K), BF16),
                        pltpu.VMEM((n_sub, DH, R * tq), BF16)],
        compiler_params=_params("parallel", "arbitrary"),
        name="nsa_attn",
    )(hb, cos_t, sin_t, hb, cos_t, sin_t, w_in, kc, vc_t, ksw, vsw_t, ksw, vsw_t, onehot, agg_t,
      gate_t, z)


def _swa_kernel(sink_ref, q_ref, kp_ref, kc_ref, vp_ref, vc_ref, z_ref, o_ref):
    W, R = SWA_WINDOW, SWA_REP
    n = pl.program_id(0)
    i_q = lax.broadcasted_iota(jnp.int32, (W, 2 * W), 0)
    c_k = lax.broadcasted_iota(jnp.int32, (W, 2 * W), 1)
    diff = i_q - (c_k - W)
    mask = (diff >= 0) & (diff < W) & ((n > 0) | (c_k >= W))
    lane = lax.broadcasted_iota(jnp.int32, (W, LANES), 1)
    low = lane < SWA_HEAD_DIM
    ones_v = jnp.ones((2 * W, LANES), BF16)
    for g in range(SWA_KV_HEADS):
        gl = slice(g * LANES, (g + 1) * LANES)
        kk = jnp.concatenate([kp_ref[:, gl], kc_ref[:, gl]], axis=0)
        vv = jnp.concatenate([vp_ref[:, gl], vc_ref[:, gl]], axis=0)
        qs = []
        for r in range(R):
            h = g * R + r
            q2 = q_ref[:, (h // 2) * LANES:(h // 2 + 1) * LANES]
            qs.append(jnp.where(low if h % 2 == 0 else ~low, q2, jnp.zeros_like(q2)))
        s_all = lax.dot_general(jnp.concatenate(qs, axis=0), kk, _NT,
                                preferred_element_type=F32)
        ps, sink_p = [], []
        for r in range(R):
            sink = sink_ref[g * R + r] * LOG2_E
            s = jnp.where(mask, s_all[r * W:(r + 1) * W], -jnp.inf)
            m = jnp.maximum(jnp.max(s, axis=1, keepdims=True), sink)
            ps.append(jnp.exp2(s - m).astype(BF16))
            sink_p.append(jnp.exp2(sink - m))
        o_all = jnp.dot(jnp.concatenate(ps, axis=0), jnp.concatenate([vv, ones_v], axis=1),
                        preferred_element_type=F32)

        def head_out(r):
            rows = slice(r * W, (r + 1) * W)
            return o_all[rows, :LANES] * (1.0 / (o_all[rows, LANES:] + sink_p[r]))

        for pr in range(R // 2):
            o_even, o_odd = head_out(2 * pr), head_out(2 * pr + 1)
            col = slice((g * R // 2 + pr) * LANES, (g * R // 2 + pr + 1) * LANES)
            o_ref[:, col] = (jnp.where(low, o_even, o_odd) * _silu(z_ref[:, col])).astype(o_ref.dtype)


def _swa_attn(sinks, q, kv_dup, z):
    S, D = q.shape
    W = SWA_WINDOW
    wide = SWA_KV_HEADS * LANES
    cur = lambda i: (i, 0)
    k_prev, k_cur = (lambda i: (jnp.maximum(i - 1, 0), 0)), cur
    v_prev, v_cur = (lambda i: (jnp.maximum(i - 1, 0), 1)), (lambda i: (i, 1))
    return pl.pallas_call(
        _swa_kernel,
        grid=(S // W,),
        in_specs=[pl.BlockSpec(memory_space=pltpu.SMEM),
                  pl.BlockSpec((W, D), cur),
                  pl.BlockSpec((W, wide), k_prev), pl.BlockSpec((W, wide), k_cur),
                  pl.BlockSpec((W, wide), v_prev), pl.BlockSpec((W, wide), v_cur),
                  pl.BlockSpec((W, D), cur)],
        out_specs=pl.BlockSpec((W, D), cur),
        out_shape=jax.ShapeDtypeStruct((S, D), BF16),
        compiler_params=_params("parallel"),
        name="swa_attn",
    )(sinks, q, kv_dup, kv_dup, kv_dup, kv_dup, z)


def _rope_tables(positions, head_dim, scale):
    rot = head_dim // ROPE_FRACTION
    half = rot // 2
    inv_freq = ROPE_THETA ** (-jnp.arange(0, rot, 2, dtype=F32) / rot)
    ang = positions.reshape(-1).astype(F32)[:, None] * inv_freq
    cos, sin = jnp.cos(ang), jnp.sin(ang)
    S = cos.shape[0]
    rest = head_dim - rot
    cos_h = jnp.concatenate([cos, cos, jnp.ones((S, rest), F32)], axis=1)
    sin_h = jnp.concatenate([-sin, sin, jnp.zeros((S, rest), F32)], axis=1)
    reps = LANES // head_dim
    return (jnp.tile(cos_h, (1, reps)) * scale, jnp.tile(sin_h, (1, reps)) * scale, half, head_dim)


def _nsa_constants(S):
    n_cmp = S // CMP_STRIDE - 1
    ncp = S // CMP_STRIDE
    n_slc = S // SLC_BLOCK
    ratio = SLC_BLOCK // CMP_STRIDE
    i = np.arange(ncp)[:, None]
    j = np.arange(LANES)[None, :]
    agg = ((i >= ratio * j - CMP_BLOCK // CMP_STRIDE + 1) & (i <= ratio * j + ratio - 1)
           & (i < n_cmp) & (j < n_slc))
    onehot = (np.arange(S)[:, None] // SLC_BLOCK) == j
    return jnp.asarray(onehot, dtype=BF16), jnp.asarray(agg.T, dtype=BF16)


def _rope_tables_t(positions, head_dim):
    rot = head_dim // ROPE_FRACTION
    inv_freq = ROPE_THETA ** (-jnp.arange(0, rot, 2, dtype=F32) / rot)
    ang = inv_freq[:, None] * positions.reshape(-1).astype(F32)[None, :]
    cos, sin = jnp.cos(ang), jnp.sin(ang)
    return jnp.concatenate([cos, cos], axis=0), jnp.concatenate([-sin, sin], axis=0)


def _hgrn_layer(hb, w_in, g_norm, log_lb, log_1m_lb):
    return _hgrn_mixer(hb, w_in, log_lb, log_1m_lb, g_norm)


def _nsa_layer(hb, positions, w_in, pos_k, w1_k, w2_k, pos_v, w1_v, w2_v):
    S = hb.shape[0]
    H, G, DH = NSA_HEADS, NSA_KV_GROUPS, NSA_HEAD_DIM
    gw = G * DH
    assert gw == 512 and (H * DH) % gw == 0
    t_kc = H * DH // gw
    o_gate = H * DH + 6 * gw
    o_z = o_gate + 3 * H
    rope_k = _rope_tables(positions, DH, 1.0)
    tq = NSA_TQ

    wide = 2 * gw
    rope_t = _rope_tables_t(positions, DH)
    kv_c = _mm(hb, w_in, cols=(t_kc, 1, 2), out_dtype=F32, rope=rope_k, rope_tiles=1,
               head_major=True)
    ksw = _mm(hb, w_in, cols=(t_kc + 2, 2, 2), out_dtype=BF16, rope=rope_k, head_major=True)
    vsw_t = _mm_t(hb, w_in, cols=(t_kc + 3, 2, 2), out_dtype=BF16, tk=tq)
    z = _mm(hb, w_in[:, o_z:], out_dtype=F32, tn=wide)
    w_gate = jnp.pad(w_in[:, o_gate:o_z], ((0, 0), (0, LANES - 3 * H)))
    gate_t = _mm_t(hb, w_gate, out_dtype=F32, tk=tq, tn=LANES)

    kc = _nsa_compress(kv_c, 0, pos_k, w1_k, w2_k, transposed=False)
    vc_t = _nsa_compress(kv_c, G, pos_v, w1_v, w2_v, transposed=True)
    onehot, agg_t = _nsa_constants(S)
    return _nsa_attn(hb, w_in, rope_t, kc, vc_t, ksw, vsw_t, onehot, agg_t, gate_t, z)


def _swa_layer(hb, positions, w_in, sinks):
    H, KV, DH = SWA_HEADS, SWA_KV_HEADS, SWA_HEAD_DIM
    o_k = H * DH
    o_v = o_k + KV * DH
    o_z = o_v + KV * DH
    tn = 512
    assert o_k % tn == 0 and o_z % tn == 0
    dup = np.concatenate([np.tile(np.arange(g * DH, (g + 1) * DH), 2) for g in range(KV)])
    rope_q = _rope_tables(positions, DH, DH ** -0.5 * LOG2_E)
    rope_k = _rope_tables(positions, DH, 1.0)
    q = _mm(hb, w_in, cols=(0, 1, o_k // (2 * tn)), out_dtype=BF16, rope=rope_q, tn=2 * tn)
    w_kv = jnp.concatenate([w_in[:, o_k:o_v][:, dup], w_in[:, o_v:o_z][:, dup]], axis=1)
    kv_dup = _mm(hb, w_kv, out_dtype=BF16, rope=rope_k, rope_tiles=1, tn=tn)
    z = _mm(hb, w_in, cols=(o_z // tn, 1, o_k // tn), out_dtype=F32, tn=tn)
    return _swa_attn(sinks, q, kv_dup, z)


def kernel(x, positions, hgrn_lb_logits, l0_w_in, l0_g_norm, l0_w_out, l0_ln_g, l0_ln_b, l1_w_in, l1_cmp_pos_k, l1_cmp_w1_k, l1_cmp_w2_k, l1_cmp_pos_v, l1_cmp_w1_v, l1_cmp_w2_v, l1_w_out, l1_ln_g, l1_ln_b, l2_w_in, l2_sinks, l2_w_out, l2_ln_g, l2_ln_b, l3_w_in, l3_g_norm, l3_w_out, l3_ln_g, l3_ln_b):
    B, S, D = x.shape
    assert B == 1 and D == D_MODEL
    lb = jnp.cumsum(jax.nn.softmax(hgrn_lb_logits.astype(F32), axis=0), axis=0)
    lb = lb - lb[0:1]
    log_lb, log_1m_lb = jnp.log(lb), jnp.log1p(-lb)

    h = x.reshape(S, D)
    hb = h.astype(BF16)

    a = _hgrn_layer(hb, l0_w_in, l0_g_norm, log_lb[0], log_1m_lb[0])
    h, hb = _outproj_ln(a, l0_w_out.astype(BF16), h, l0_ln_g, l0_ln_b)

    a = _nsa_layer(hb, positions, l1_w_in, l1_cmp_pos_k, l1_cmp_w1_k, l1_cmp_w2_k,
                   l1_cmp_pos_v, l1_cmp_w1_v, l1_cmp_w2_v)
    h, hb = _outproj_ln(a, l1_w_out.astype(BF16), h, l1_ln_g, l1_ln_b)

    a = _swa_layer(hb, positions, l2_w_in, l2_sinks)
    h, hb = _outproj_ln(a, l2_w_out.astype(BF16), h, l2_ln_g, l2_ln_b)

    a = _hgrn_layer(hb, l3_w_in, l3_g_norm, log_lb[1], log_1m_lb[1])
    h, hb = _outproj_ln(a, l3_w_out.astype(BF16), h, l3_ln_g, l3_ln_b)
    return h.reshape(B, S, D)
```

```python
import functools

import numpy as np
import jax
import jax.numpy as jnp
from jax import lax
from jax.experimental import pallas as pl
from jax.experimental.pallas import tpu as pltpu

F32 = jnp.float32
BF16 = jnp.bfloat16

D_MODEL = 2048
DEPTH = 4
DEEPNORM_ALPHA = (2 * DEPTH) ** 0.25
LN_EPS = 1e-5
RMS_EPS = 1e-6
ROPE_THETA = 500000.0
ROPE_FRACTION = 4

HG_HEAD_DIM = 128
HG_HEADS = D_MODEL // HG_HEAD_DIM
HG_CHUNK = 64
HG_SUB = 8
HG_TIME_BLOCK = 2048
HG_PIECE = 512
LOG2_E = 1.4426950408889634

NSA_HEAD_DIM = 128
NSA_HEADS = D_MODEL // NSA_HEAD_DIM
NSA_KV_GROUPS = 4
NSA_REP = NSA_HEADS // NSA_KV_GROUPS
CMP_BLOCK = 32
CMP_STRIDE = 16
CMP_HIDDEN = 256
SLC_BLOCK = 64
SLC_TOPK = 16
NSA_WINDOW = 512
NSA_TQ = 256
NSA_ONES_ROWS = 16
NSA_BLOCKS_PER_STEP = 2
FORCE_BONUS = 1.0e4
MASK_NEG = -1.0e30

SWA_HEAD_DIM = 64
SWA_HEADS = D_MODEL // SWA_HEAD_DIM
SWA_KV_HEADS = 4
SWA_REP = SWA_HEADS // SWA_KV_HEADS
SWA_WINDOW = 128

LANES = 128
VMEM_LIMIT_BYTES = 48 * 1024 * 1024

_NT = (((1,), (1,)), ((), ()))
_TN = (((0,), (0,)), ((), ()))


def _params(*sem):
    return pltpu.CompilerParams(dimension_semantics=sem, vmem_limit_bytes=VMEM_LIMIT_BYTES)


def _silu(x):
    return x * (1.0 / (1.0 + jnp.exp(-x)))


def _mm_kernel(*refs, rope_half, rope_period, rope_tiles, head_major, n_chunks):
    if rope_half:
        x_ref, w_ref, c_ref, s_ref, o_ref, wb_ref = refs
    else:
        x_ref, w_ref, o_ref, wb_ref = refs

    @pl.when(pl.program_id(1) == 0)
    def _():
        wb_ref[...] = w_ref[...].astype(BF16)

    acc = jnp.dot(x_ref[...], wb_ref[...], preferred_element_type=F32)

    def write(with_rope):
        if with_rope:
            cos = c_ref[...]
            sin = s_ref[...]
            lane = lax.broadcasted_iota(jnp.int32, cos.shape, 1)
            first_half = (lane & (rope_period - 1)) < rope_half
        for j in range(n_chunks):
            a = acc[:, j * LANES:(j + 1) * LANES]
            if with_rope:
                up = pltpu.roll(a, LANES - rope_half, 1)
                dn = pltpu.roll(a, rope_half, 1)
                a = a * cos + jnp.where(first_half, up, dn) * sin
            if head_major:
                o_ref[j] = a.astype(o_ref.dtype)
            else:
                o_ref[:, j * LANES:(j + 1) * LANES] = a.astype(o_ref.dtype)

    if rope_half and rope_tiles is not None:
        pl.when(pl.program_id(0) < rope_tiles)(functools.partial(write, True))
        pl.when(pl.program_id(0) >= rope_tiles)(functools.partial(write, False))
    else:
        write(bool(rope_half))


def _col_tiles(w, cols, tn):
    return (0, 1, w.shape[1] // tn) if cols is None else cols


def _mm(x, w, *, out_dtype, cols=None, rope=None, rope_tiles=None, head_major=False, tn=512):
    S, K = x.shape
    tm = min(1024, S)
    first, stride, n_tiles = _col_tiles(w, cols, tn)
    N = n_tiles * tn
    n_chunks = tn // LANES
    in_specs = [pl.BlockSpec((tm, K), lambda j, i: (i, 0)),
                pl.BlockSpec((K, tn), lambda j, i: (0, first + stride * j))]
    args = [x, w]
    half = period = 0
    if rope is not None:
        cos_t, sin_t, half, period = rope
        in_specs += [pl.BlockSpec((tm, LANES), lambda j, i: (i, 0))] * 2
        args += [cos_t, sin_t]
    if head_major:
        out_shape = jax.ShapeDtypeStruct((N // LANES, S, LANES), out_dtype)
        out_spec = pl.BlockSpec((n_chunks, tm, LANES), lambda j, i: (j, i, 0))
    else:
        out_shape = jax.ShapeDtypeStruct((S, N), out_dtype)
        out_spec = pl.BlockSpec((tm, tn), lambda j, i: (i, j))
    return pl.pallas_call(
        functools.partial(_mm_kernel, rope_half=half, rope_period=period, rope_tiles=rope_tiles,
                          head_major=head_major, n_chunks=n_chunks),
        grid=(n_tiles, S // tm),
        in_specs=in_specs,
        out_specs=out_spec,
        out_shape=out_shape,
        scratch_shapes=[pltpu.VMEM((K, tn), BF16)],
        compiler_params=_params("arbitrary", "arbitrary"),
        name="in_proj",
    )(*args)


def _mm_t_kernel(w_ref, x_ref, o_ref, wt_ref, *, n_heads, n_tiles, tk):
    @pl.when(pl.program_id(1) == 0)
    def _():
        wt_ref[...] = w_ref[...].T.astype(BF16)

    acc = lax.dot_general(wt_ref[...], x_ref[...], _NT, preferred_element_type=F32)
    for c in range(n_heads):
        for b in range(n_tiles):
            o_ref[c, b] = acc[c * LANES:(c + 1) * LANES, b * tk:(b + 1) * tk].astype(o_ref.dtype)


def _mm_t(x, w, *, out_dtype, tk, cols=None, tn=512):
    S, K = x.shape
    tm = min(1024, S)
    first, stride, n_tiles = _col_tiles(w, cols, tn)
    N = n_tiles * tn
    return pl.pallas_call(
        functools.partial(_mm_t_kernel, n_heads=tn // LANES, n_tiles=tm // tk, tk=tk),
        grid=(n_tiles, S // tm),
        in_specs=[pl.BlockSpec((K, tn), lambda j, i: (0, first + stride * j)),
                  pl.BlockSpec((tm, K), lambda j, i: (i, 0))],
        out_specs=pl.BlockSpec((tn // LANES, tm // tk, LANES, tk), lambda j, i: (j, i, 0, 0)),
        out_shape=jax.ShapeDtypeStruct((N // LANES, S // tk, LANES, tk), out_dtype),
        scratch_shapes=[pltpu.VMEM((tn, K), BF16)],
        compiler_params=_params("arbitrary", "arbitrary"),
        name="in_proj_t",
    )(w, x)


def _outproj_ln_kernel(a_ref, w_ref, h_ref, g_ref, b_ref, o_ref, ob_ref):
    y = jnp.dot(a_ref[...], w_ref[...], preferred_element_type=F32)
    u = DEEPNORM_ALPHA * h_ref[...] + y
    mu = jnp.mean(u, axis=-1, keepdims=True)
    xc = u - mu
    var = jnp.mean(xc * xc, axis=-1, keepdims=True)
    out = xc * lax.rsqrt(var + LN_EPS) * g_ref[...] + b_ref[...]
    o_ref[...] = out
    ob_ref[...] = out.astype(BF16)


def _outproj_ln(a, w, h, g, b):
    S, D = h.shape
    tm = min(512, S)
    row = pl.BlockSpec((tm, D), lambda i: (i, 0))
    vec = pl.BlockSpec((1, D), lambda i: (0, 0))
    return pl.pallas_call(
        _outproj_ln_kernel,
        grid=(S // tm,),
        in_specs=[row, pl.BlockSpec((D, D), lambda i: (0, 0)), row, vec, vec],
        out_specs=[row, row],
        out_shape=[jax.ShapeDtypeStruct((S, D), F32), jax.ShapeDtypeStruct((S, D), BF16)],
        compiler_params=_params("parallel"),
        name="outproj_ln",
    )(a, w, h, g.reshape(1, D), b.reshape(1, D))


def _hgrn_kernel(x_ref, wq_ref, wf_ref, wv_ref, wz_ref, llb_ref, l1m_ref, gn_ref,
                 o_ref, st_ref, u_ref, wb_ref, *, n_pieces, piece):
    C, SUB, DK = HG_CHUNK, HG_SUB, HG_HEAD_DIM
    n_lvl = (C // (2 * SUB)).bit_length()
    n_pc = piece // C

    @pl.when(pl.program_id(1) == 0)
    def _():
        st_ref[...] = jnp.zeros_like(st_ref)
        for i, w_ref in enumerate((wq_ref, wf_ref, wv_ref, wz_ref)):
            wb_ref[:, i * DK:(i + 1) * DK] = w_ref[...].astype(BF16)

    r = lax.broadcasted_iota(jnp.int32, (C, C), 0)
    c = lax.broadcasted_iota(jnp.int32, (C, C), 1)
    causal = c <= r
    halves = [SUB << lvl for lvl in range(n_lvl)]
    cum_rows = []
    for h in halves:
        same_h = (r // h) == (c // h)
        cum_rows += [causal & same_h, same_h]
    cum_one = jnp.concatenate([m.astype(F32) for m in cum_rows], axis=0).astype(BF16)
    cum_mat = jnp.concatenate([cum_one] * 3, axis=1)
    diag_mask = causal & ((r // SUB) == (c // SUB))
    log_lb = llb_ref[...]
    log_1m_lb = l1m_ref[...]
    lane_c = lax.broadcasted_iota(jnp.int32, (SUB, C), 1)
    chunks = [slice(ci * C, (ci + 1) * C) for ci in range(n_pc)]

    def rows_to_lanes(x):
        return jnp.concatenate([x[sl] for sl in chunks], axis=1)

    def lanes_to_rows(x):
        return jnp.concatenate([x[:, ci * DK:(ci + 1) * DK] for ci in range(n_pc)], axis=0)

    def placed(x, lo):
        parts = [jnp.zeros((lo, DK), F32), x, jnp.zeros((C - lo - x.shape[0], DK), F32)]
        return jnp.concatenate([p for p in parts if p.shape[0]], axis=0)

    def front(pi):
        rows = slice(pi * piece, (pi + 1) * piece)
        proj = jnp.dot(x_ref[rows, :], wb_ref[...], preferred_element_type=F32)
        q, fl, v, z = (proj[:, i * DK:(i + 1) * DK] for i in range(4))
        vb = v.astype(BF16)
        log_sig = jnp.minimum(fl, 0.0) - jnp.log(1.0 + jnp.exp(-jnp.abs(fl)))
        t = log_1m_lb + log_sig
        log_f = jnp.maximum(log_lb, t) + jnp.log(1.0 + jnp.exp(-jnp.abs(log_lb - t)))
        log2_k = (t - fl) * LOG2_E
        k = jnp.exp2(log2_k)
        lf = rows_to_lanes(log_f)
        lf_hi = lf.astype(BF16)
        lf_r1 = lf - lf_hi.astype(F32)
        lf_mid = lf_r1.astype(BF16)
        lf_lo = (lf_r1 - lf_mid.astype(F32)).astype(BF16)
        cums = jnp.dot(cum_mat, jnp.concatenate([lf_hi, lf_mid, lf_lo], axis=0),
                       preferred_element_type=F32) * LOG2_E
        lvl = [cums[i * C:(i + 1) * C] for i in range(2 * n_lvl)]
        loc_top, tot_top = lvl[-2], lvl[-1]
        tot_chunk = tot_top[:C // 2] + tot_top[C // 2:]
        cum = [lanes_to_rows(a) for a in [
            jnp.concatenate([loc_top[:C // 2], loc_top[C // 2:] + tot_top[:C // 2]], axis=0),
            jnp.concatenate([tot_chunk, tot_chunk], axis=0)] + lvl]
        b_full, b_tot = cum[0], cum[1]
        qe = (q * jnp.exp2(b_full)).astype(BF16)
        kd = (k * jnp.exp2(b_tot - b_full)).astype(BF16)
        q_lvl = [q * jnp.exp2(cum[2 + 2 * l]) for l in range(n_lvl)]
        k_lvl = [k * jnp.exp2(cum[3 + 2 * l] - cum[2 + 2 * l]) for l in range(n_lvl)]
        b_loc = cum[2]
        c_row = b_loc - log2_k

        a_offs, a_diags = [], []
        for ci, ch in enumerate(chunks):
            blocks = []
            for j in range(C // SUB):
                base = ci * C + j * SUB
                sl = slice(base, base + SUB)
                blk = jnp.zeros((SUB, C), F32)
                for s in range(SUB):
                    y = q[sl] * jnp.exp2(jnp.minimum(
                        b_loc[sl] - c_row[base + s:base + s + 1, :], log2_k[base + s:base + s + 1, :]))
                    blk = jnp.where(lane_c == j * SUB + s, jnp.sum(y, axis=-1, keepdims=True), blk)
                blocks.append(blk)
            a_diags.append(jnp.concatenate(blocks, axis=0))
            lhs, rhs = [], []
            for l, h in enumerate(halves):
                q_c, k_c = q_lvl[l][ch], k_lvl[l][ch]
                for lo in range(0, C, 2 * h):
                    lhs.append(placed(q_c[lo + h:lo + 2 * h], lo + h))
                    rhs.append(placed(k_c[lo:lo + h], lo))
            a_offs.append(lax.dot_general(jnp.concatenate(lhs, axis=1).astype(BF16),
                                          jnp.concatenate(rhs, axis=1).astype(BF16),
                                          _NT, preferred_element_type=F32))
            u_ref[pi * n_pc + ci] = lax.dot_general(vb[ch], kd[ch], _TN,
                                                    preferred_element_type=F32)
        per_chunk = []
        for ci, ch in enumerate(chunks):
            a = jnp.where(diag_mask, a_diags[ci], 0.0) + a_offs[ci]
            o_intra = jnp.dot(a.astype(BF16), vb[ch], preferred_element_type=F32)
            per_chunk.append((qe[ch], o_intra, b_tot[ci * C:ci * C + 8]))
        return per_chunk, z

    fronts = [front(pi) for pi in range(n_pieces)]

    st = st_ref[...]
    outs = []
    for pi, (per_chunk, _) in enumerate(fronts):
        for ci, (qe_c, o_intra, b_tot8) in enumerate(per_chunk):
            outs.append(o_intra + lax.dot_general(qe_c, st.astype(BF16), _NT,
                                                  preferred_element_type=F32))
            st = st * jnp.exp2(jnp.tile(b_tot8, (DK // 8, 1))) + u_ref[pi * n_pc + ci]
    st_ref[...] = st

    o = jnp.concatenate(outs, axis=0)
    z = jnp.concatenate([z_p for _, z_p in fronts], axis=0)
    ms = jnp.mean(o * o, axis=-1, keepdims=True)
    out = o * lax.rsqrt(ms + RMS_EPS) * gn_ref[...] * _silu(z)
    o_ref[...] = out.astype(o_ref.dtype)


def _hgrn_mixer(hb, w_in, log_lb, log_1m_lb, g_norm):
    S, K = hb.shape
    D, DK, H = D_MODEL, HG_HEAD_DIM, HG_HEADS
    T = min(HG_TIME_BLOCK, S)
    piece = min(HG_PIECE, T)

    def w_col(off):
        return pl.BlockSpec((K, DK), lambda h, t, off=off: (0, off + h))

    vec = pl.BlockSpec((1, DK), lambda h, t: (0, h))
    return pl.pallas_call(
        functools.partial(_hgrn_kernel, n_pieces=T // piece, piece=piece),
        grid=(H, S // T),
        in_specs=[pl.BlockSpec((T, K), lambda h, t: (t, 0)),
                  w_col(0), w_col(H), w_col(2 * H), w_col(3 * H), vec, vec, vec],
        out_specs=pl.BlockSpec((T, DK), lambda h, t: (t, h)),
        out_shape=jax.ShapeDtypeStruct((S, D), BF16),
        scratch_shapes=[pltpu.VMEM((DK, DK), F32),
                        pltpu.VMEM((T // HG_CHUNK, DK, DK), F32),
                        pltpu.VMEM((K, 4 * DK), BF16)],
        compiler_params=pltpu.CompilerParams(
            dimension_semantics=("parallel", "arbitrary"), vmem_limit_bytes=VMEM_LIMIT_BYTES,
            allow_input_fusion=[True] + [False] * 7),
        name="hgrn_mixer",
    )(hb, w_in, w_in, w_in, w_in, log_lb.reshape(1, D), log_1m_lb.reshape(1, D),
      g_norm.reshape(1, D))


def _nsa_compress_kernel(x_ref, pos_ref, w1_ref, w2_ref, o_ref, *, transposed):
    DH = NSA_HEAD_DIM
    n = x_ref.shape[0] // CMP_STRIDE
    h1 = jnp.zeros((n, CMP_HIDDEN), F32)
    h2 = jnp.zeros((n, CMP_HIDDEN), F32)
    for l in range(CMP_STRIDE):
        x_l = x_ref[pl.ds(l, n, stride=CMP_STRIDE), :]
        l2 = CMP_STRIDE + l
        h1 = h1 + jnp.dot((x_l + pos_ref[l:l + 1, :]).astype(BF16),
                          w1_ref[l * DH:(l + 1) * DH, :].astype(BF16), preferred_element_type=F32)
        h2 = h2 + jnp.dot((x_l + pos_ref[l2:l2 + 1, :]).astype(BF16),
                          w1_ref[l2 * DH:(l2 + 1) * DH, :].astype(BF16), preferred_element_type=F32)
    hid = h1 + pltpu.roll(h2, n - 1, 0)
    act = _silu(hid).astype(BF16)
    if transposed:
        o_ref[...] = lax.dot_general(w2_ref[...].T.astype(BF16), act, _NT,
                                     preferred_element_type=F32).astype(o_ref.dtype)
    else:
        o_ref[...] = jnp.dot(act, w2_ref[...].astype(BF16),
                             preferred_element_type=F32).astype(o_ref.dtype)


def _nsa_compress(x, first, pos, w1, w2, *, transposed):
    G = NSA_KV_GROUPS
    _, S, DH = x.shape
    n = S // CMP_STRIDE
    whole = lambda a: pl.BlockSpec(a.shape, lambda g: (0,) * a.ndim)
    out_block = (None, DH, n) if transposed else (None, n, DH)
    return pl.pallas_call(
        functools.partial(_nsa_compress_kernel, transposed=transposed),
        grid=(G,),
        in_specs=[pl.BlockSpec((None, S, DH), lambda g: (first + g, 0, 0)),
                  whole(pos), whole(w1), whole(w2)],
        out_specs=pl.BlockSpec(out_block, lambda g: (g, 0, 0)),
        out_shape=jax.ShapeDtypeStruct((G,) + out_block[1:], BF16),
        compiler_params=_params("parallel"),
        name="nsa_compress",
    )(x, pos, w1, w2)


def _nsa_kernel(x0_ref, cos0_ref, sin0_ref, x1_ref, cos1_ref, sin1_ref, wq_ref, kc_ref, vct_ref,
                ks_ref, vst_ref, kw_ref, vwt_ref, oh_ref, aggt_ref, gate_ref, z_ref, o_ref,
                m_ref, acc_ref, s_ref, wqt_ref, q_ref, *, tq, n_sub, q_scale):
    R, DH = NSA_REP, NSA_HEAD_DIM
    ncp = kc_ref.shape[1]
    rot = cos0_ref.shape[0]

    def project_q(xr, cr, sr, sub):
        t_sl = slice(sub * tq, (sub + 1) * tq)
        q_t = lax.dot_general(wqt_ref[...], xr[t_sl, :], _NT,
                              preferred_element_type=F32) * q_scale
        cos, sin = cr[:, t_sl], sr[:, t_sl]
        heads = []
        for r in range(R):
            a = q_t[r * DH:(r + 1) * DH]
            swapped = jnp.concatenate([a[rot // 2:rot], a[:rot // 2]], axis=0)
            heads.append(jnp.concatenate([a[:rot] * cos + swapped * sin, a[rot:]],
                                         axis=0).astype(BF16))
        return jnp.concatenate(heads, axis=1)

    @pl.when(pl.program_id(1) == 0)
    def _():
        wqt_ref[...] = wq_ref[...].T.astype(BF16)
        for sub in range(n_sub):
            q_ref[sub] = project_q(x0_ref, cos0_ref, sin0_ref, sub)

    def per_head(x):
        return jnp.concatenate([x] * R, axis=1)

    def rows8(x, n):
        return jnp.tile(x, (n // 8, 1))

    def col_max8(x):
        mx = x[0:8]
        for i in range(1, x.shape[0] // 8):
            mx = jnp.maximum(mx, x[8 * i:8 * i + 8])
        for shift in (4, 2, 1):
            mx = jnp.maximum(mx, pltpu.roll(mx, shift, 0))
        return mx

    def key_rows(ref, kt):
        return ref[0, pl.ds(pl.multiple_of(kt * tq, tq), tq), :]

    c_k = lax.broadcasted_iota(jnp.int32, (tq, tq), 0)
    i_q = lax.broadcasted_iota(jnp.int32, (tq, tq), 1)
    causal = per_head(c_k <= i_q)
    aggt = aggt_ref[...]

    def front(sub):
        qb = pl.program_id(1) * n_sub + sub
        start = qb * tq
        q = q_ref[sub]

        n_c = lax.broadcasted_iota(jnp.int32, (ncp, tq), 0)
        t_c = start + lax.broadcasted_iota(jnp.int32, (ncp, tq), 1)
        cmask = per_head((n_c * CMP_STRIDE + CMP_BLOCK - 1) <= t_c)
        s = jnp.where(cmask, jnp.dot(kc_ref[0], q, preferred_element_type=F32), -jnp.inf)
        m = jnp.max(s, axis=0, keepdims=True)
        m = jnp.where(m == -jnp.inf, 0.0, m)
        e = jnp.exp2(s - m)
        p = e * (1.0 / jnp.maximum(jnp.sum(e, axis=0, keepdims=True), 1e-30))
        o_cmp = jnp.dot(vct_ref[0], p.astype(BF16), preferred_element_type=F32)
        p_sum = p[:, :tq]
        for r in range(1, R):
            p_sum = p_sum + p[:, r * tq:(r + 1) * tq]

        no_old = jnp.where(qb >= 2, 0, NSA_WINDOW)
        no_mid = jnp.where(qb >= 1, 0, NSA_WINDOW)
        old_ok = per_head((2 * tq + i_q - c_k + no_old) < NSA_WINDOW)
        mid_ok = per_head((tq + i_q - c_k + no_mid) < NSA_WINDOW)
        kt_old, kt_mid = jnp.maximum(qb - 2, 0), jnp.maximum(qb - 1, 0)
        k_win = jnp.concatenate([key_rows(kw_ref, kt_old), key_rows(kw_ref, kt_mid),
                                 key_rows(kw_ref, qb)], axis=0)
        s_w = jnp.where(jnp.concatenate([old_ok, mid_ok, causal], axis=0),
                        jnp.dot(k_win, q, preferred_element_type=F32), MASK_NEG)
        p_w = jnp.exp2(s_w - rows8(col_max8(s_w), 3 * tq)).astype(BF16)
        v_win = jnp.concatenate([vwt_ref[0, kt_old], vwt_ref[0, kt_mid], vwt_ref[0, qb]], axis=1)
        acc_w = jnp.dot(jnp.concatenate([v_win, jnp.ones((NSA_ONES_ROWS, 3 * tq), BF16)], axis=0),
                        p_w, preferred_element_type=F32)
        o_win = acc_w[:DH] * rows8(1.0 / acc_w[DH:DH + 8], DH)

        p_hi = p_sum.astype(BF16)
        p_lo = (p_sum - p_hi.astype(F32)).astype(BF16)
        imp = (jnp.dot(aggt, p_hi, preferred_element_type=F32)
               + jnp.dot(aggt, p_lo, preferred_element_type=F32))
        j_s = lax.broadcasted_iota(jnp.int32, (LANES, tq), 0)
        t_s = start + lax.broadcasted_iota(jnp.int32, (LANES, tq), 1)
        cur = lax.shift_right_logical(t_s, 6)
        forced = (j_s == 0) | (j_s == cur) | (j_s == cur - 1)
        allowed = j_s * SLC_BLOCK <= t_s
        sel = jnp.where(forced & allowed, 1.0, 0.0)
        val = jnp.where(allowed & ~forced, imp, -jnp.inf)
        row_f = j_s.astype(F32)
        for _ in range(SLC_TOPK - 3):
            mx = jnp.max(val, axis=0, keepdims=True)
            idx = jnp.min(jnp.where(val == mx, row_f, float(LANES)), axis=0, keepdims=True)
            pick = row_f == idx
            sel = jnp.where(pick, 1.0, sel)
            val = jnp.where(pick, -jnp.inf, val)
        m_neg = jnp.where(sel > 0.0, 0.0, MASK_NEG).astype(BF16)
        q_aug = jnp.concatenate([q, per_head(m_neg)], axis=0)
        return qb, q_aug, o_cmp, o_win

    ones_rows = jnp.ones((NSA_ONES_ROWS, tq), BF16)

    def flash_init():
        m_ref[...] = jnp.full(m_ref.shape, -jnp.inf, F32)
        acc_ref[...] = jnp.zeros_like(acc_ref)

    def flash_update(s, vt_tile):
        m_prev = m_ref[...]
        m_next = jnp.maximum(m_prev, col_max8(s))
        alpha = jnp.exp2(m_prev - m_next)
        p = jnp.exp2(s - rows8(m_next, tq)).astype(BF16)
        v_aug = jnp.concatenate([vt_tile, ones_rows], axis=0)
        acc_ref[...] = (acc_ref[...] * rows8(alpha, DH + NSA_ONES_ROWS)
                        + jnp.dot(v_aug, p, preferred_element_type=F32))
        m_ref[...] = m_next

    def flash_result():
        acc = acc_ref[...]
        return acc[:DH] * rows8(1.0 / acc[DH:DH + 8], DH)

    def back(sub, qb, q_aug, o_cmp, o_win):
        def slc_scores(kt):
            oh = oh_ref[pl.ds(pl.multiple_of(kt * tq, tq), tq), :]
            k_aug = jnp.concatenate([key_rows(ks_ref, kt), oh], axis=1)
            return jnp.dot(k_aug, q_aug, preferred_element_type=F32)

        flash_init()
        s_ref[...] = slc_scores(0)

        def slc_step(kt):
            s_cur = s_ref[...]
            s_ref[...] = slc_scores(kt + 1)
            flash_update(s_cur, vst_ref[0, kt])

        def slc_quad(i, carry):
            for u in range(4):
                slc_step(4 * i + u)
            return carry

        lax.fori_loop(0, lax.shift_right_logical(qb, 2), slc_quad, 0)
        done = qb & ~3

        @pl.when((qb & 2) != 0)
        def _():
            slc_step(done)
            slc_step(done + 1)

        @pl.when((qb & 1) != 0)
        def _():
            slc_step(qb - 1)

        flash_update(jnp.where(causal, s_ref[...], MASK_NEG), vst_ref[0, qb])
        o_slc = flash_result()

        head0 = pl.program_id(0) * R

        def gate(branch, r):
            logit = gate_ref[0, sub, pl.ds(branch * NSA_HEADS + head0 + r, 1), :]
            return jnp.broadcast_to(1.0 / (1.0 + jnp.exp(-logit)), (DH, tq))

        rows = slice(sub * tq, (sub + 1) * tq)
        for r in range(R):
            sl = slice(r * tq, (r + 1) * tq)
            g = [gate(b, r) for b in range(3)]
            o_t = g[0] * o_cmp[:, sl] + g[1] * o_slc[:, sl] + g[2] * o_win[:, sl]
            z = z_ref[rows, r * DH:(r + 1) * DH]
            o_ref[rows, r * DH:(r + 1) * DH] = (o_t.T * _silu(z)).astype(o_ref.dtype)

    fronts = [front(sub) for sub in range(n_sub)]
    for sub in range(n_sub):
        q_ref[sub] = project_q(x1_ref, cos1_ref, sin1_ref, sub)
    for sub in range(n_sub):
        back(sub, *fronts[sub])


def _nsa_attn(hb, w_in, rope_t, kc, vc_t, ksw, vsw_t, onehot, agg_t, gate_t, z):
    G, R, DH = NSA_KV_GROUPS, NSA_REP, NSA_HEAD_DIM
    S = ksw.shape[1]
    tq, n_sub = NSA_TQ, NSA_BLOCKS_PER_STEP
    assert S % (n_sub * tq) == 0 and NSA_WINDOW <= 2 * tq and S // SLC_BLOCK <= LANES
    ncp = kc.shape[1]
    n_t = S // tq
    K = hb.shape[1]
    cos_t, sin_t = rope_t
    rot = cos_t.shape[0]
    step = n_sub * tq
    nxt = lambda i: jnp.minimum(i + 1, S // step - 1)
    rows = lambda off: pl.BlockSpec((1, S, DH), lambda g, i, off=off: (off + g, 0, 0))
    tiles_t = lambda off: pl.BlockSpec((1, n_t, DH, tq), lambda g, i, off=off: (off + g, 0, 0, 0))
    return pl.pallas_call(
        functools.partial(_nsa_kernel, tq=tq, n_sub=n_sub, q_scale=DH ** -0.5 * LOG2_E),
        grid=(G, n_t // n_sub),
        in_specs=[pl.BlockSpec((step, K), lambda g, i: (0, 0)),
                  pl.BlockSpec((rot, step), lambda g, i: (0, 0)),
                  pl.BlockSpec((rot, step), lambda g, i: (0, 0)),
                  pl.BlockSpec((step, K), lambda g, i: (nxt(i), 0)),
                  pl.BlockSpec((rot, step), lambda g, i: (0, nxt(i))),
                  pl.BlockSpec((rot, step), lambda g, i: (0, nxt(i))),
                  pl.BlockSpec((K, R * DH), lambda g, i: (0, g)),
                  pl.BlockSpec((1, ncp, DH), lambda g, i: (g, 0, 0)),
                  pl.BlockSpec((1, DH, ncp), lambda g, i: (g, 0, 0)),
                  rows(0), tiles_t(0), rows(G), tiles_t(G),
                  pl.BlockSpec((S, LANES), lambda g, i: (0, 0)),
                  pl.BlockSpec((LANES, ncp), lambda g, i: (0, 0)),
                  pl.BlockSpec((1, n_sub, LANES, tq), lambda g, i: (0, i, 0, 0)),
                  pl.BlockSpec((n_sub * tq, R * DH), lambda g, i: (i, g))],
        out_specs=pl.BlockSpec((n_sub * tq, R * DH), lambda g, i: (i, g)),
        out_shape=jax.ShapeDtypeStruct((S, D_MODEL), BF16),
        scratch_shapes=[pltpu.VMEM((8, R * tq), F32),
                        pltpu.VMEM((DH + NSA_ONES_ROWS, R * tq), F32),
                        pltpu.VMEM((tq, R * tq), F32),
                        pltpu.VMEM((R * DH, K), BF16),
                        pltpu.VMEM((n_sub, DH, R * tq), BF16)],
        compiler_params=_params("parallel", "arbitrary"),
        name="nsa_attn",
    )(hb, cos_t, sin_t, hb, cos_t, sin_t, w_in, kc, vc_t, ksw, vsw_t, ksw, vsw_t, onehot, agg_t,
      gate_t, z)


def _swa_kernel(sink_ref, q_ref, kp_ref, kc_ref, vp_ref, vc_ref, z_ref, o_ref):
    W, R = SWA_WINDOW, SWA_REP
    n = pl.program_id(0)
    i_q = lax.broadcasted_iota(jnp.int32, (W, 2 * W), 0)
    c_k = lax.broadcasted_iota(jnp.int32, (W, 2 * W), 1)
    diff = i_q - (c_k - W)
    mask = (diff >= 0) & (diff < W) & ((n > 0) | (c_k >= W))
    lane = lax.broadcasted_iota(jnp.int32, (W, LANES), 1)
    low = lane < SWA_HEAD_DIM
    ones_v = jnp.ones((2 * W, LANES), BF16)
    for g in range(SWA_KV_HEADS):
        gl = slice(g * LANES, (g + 1) * LANES)
        kk = jnp.concatenate([kp_ref[:, gl], kc_ref[:, gl]], axis=0)
        vv = jnp.concatenate([vp_ref[:, gl], vc_ref[:, gl]], axis=0)
        qs = []
        for r in range(R):
            h = g * R + r
            q2 = q_ref[:, (h // 2) * LANES:(h // 2 + 1) * LANES]
            qs.append(jnp.where(low if h % 2 == 0 else ~low, q2, jnp.zeros_like(q2)))
        s_all = lax.dot_general(jnp.concatenate(qs, axis=0), kk, _NT,
                                preferred_element_type=F32)
        ps, sink_p = [], []
        for r in range(R):
            sink = sink_ref[g * R + r] * LOG2_E
            s = jnp.where(mask, s_all[r * W:(r + 1) * W], -jnp.inf)
            m = jnp.maximum(jnp.max(s, axis=1, keepdims=True), sink)
            ps.append(jnp.exp2(s - m).astype(BF16))
            sink_p.append(jnp.exp2(sink - m))
        o_all = jnp.dot(jnp.concatenate(ps, axis=0), jnp.concatenate([vv, ones_v], axis=1),
                        preferred_element_type=F32)

        def head_out(r):
            rows = slice(r * W, (r + 1) * W)
            return o_all[rows, :LANES] * (1.0 / (o_all[rows, LANES:] + sink_p[r]))

        for pr in range(R // 2):
            o_even, o_odd = head_out(2 * pr), head_out(2 * pr + 1)
            col = slice((g * R // 2 + pr) * LANES, (g * R // 2 + pr + 1) * LANES)
            o_ref[:, col] = (jnp.where(low, o_even, o_odd) * _silu(z_ref[:, col])).astype(o_ref.dtype)


def _swa_attn(sinks, q, kv_dup, z):
    S, D = q.shape
    W = SWA_WINDOW
    wide = SWA_KV_HEADS * LANES
    cur = lambda i: (i, 0)
    k_prev, k_cur = (lambda i: (jnp.maximum(i - 1, 0), 0)), cur
    v_prev, v_cur = (lambda i: (jnp.maximum(i - 1, 0), 1)), (lambda i: (i, 1))
    return pl.pallas_call(
        _swa_kernel,
        grid=(S // W,),
        in_specs=[pl.BlockSpec(memory_space=pltpu.SMEM),
                  pl.BlockSpec((W, D), cur),
                  pl.BlockSpec((W, wide), k_prev), pl.BlockSpec((W, wide), k_cur),
                  pl.BlockSpec((W, wide), v_prev), pl.BlockSpec((W, wide), v_cur),
                  pl.BlockSpec((W, D), cur)],
        out_specs=pl.BlockSpec((W, D), cur),
        out_shape=jax.ShapeDtypeStruct((S, D), BF16),
        compiler_params=_params("parallel"),
        name="swa_attn",
    )(sinks, q, kv_dup, kv_dup, kv_dup, kv_dup, z)


def _rope_tables(positions, head_dim, scale):
    rot = head_dim // ROPE_FRACTION
    half = rot // 2
    inv_freq = ROPE_THETA ** (-jnp.arange(0, rot, 2, dtype=F32) / rot)
    ang = positions.reshape(-1).astype(F32)[:, None] * inv_freq
    cos, sin = jnp.cos(ang), jnp.sin(ang)
    S = cos.shape[0]
    rest = head_dim - rot
    cos_h = jnp.concatenate([cos, cos, jnp.ones((S, rest), F32)], axis=1)
    sin_h = jnp.concatenate([-sin, sin, jnp.zeros((S, rest), F32)], axis=1)
    reps = LANES // head_dim
    return (jnp.tile(cos_h, (1, reps)) * scale, jnp.tile(sin_h, (1, reps)) * scale, half, head_dim)


def _nsa_constants(S):
    n_cmp = S // CMP_STRIDE - 1
    ncp = S // CMP_STRIDE
    n_slc = S // SLC_BLOCK
    ratio = SLC_BLOCK // CMP_STRIDE
    i = np.arange(ncp)[:, None]
    j = np.arange(LANES)[None, :]
    agg = ((i >= ratio * j - CMP_BLOCK // CMP_STRIDE + 1) & (i <= ratio * j + ratio - 1)
           & (i < n_cmp) & (j < n_slc))
    onehot = (np.arange(S)[:, None] // SLC_BLOCK) == j
    return jnp.asarray(onehot, dtype=BF16), jnp.asarray(agg.T, dtype=BF16)


def _rope_tables_t(positions, head_dim):
    rot = head_dim // ROPE_FRACTION
    inv_freq = ROPE_THETA ** (-jnp.arange(0, rot, 2, dtype=F32) / rot)
    ang = inv_freq[:, None] * positions.reshape(-1).astype(F32)[None, :]
    cos, sin = jnp.cos(ang), jnp.sin(ang)
    return jnp.concatenate([cos, cos], axis=0), jnp.concatenate([-sin, sin], axis=0)


def _hgrn_layer(hb, w_in, g_norm, log_lb, log_1m_lb):
    return _hgrn_mixer(hb, w_in, log_lb, log_1m_lb, g_norm)


def _nsa_layer(hb, positions, w_in, pos_k, w1_k, w2_k, pos_v, w1_v, w2_v):
    S = hb.shape[0]
    H, G, DH = NSA_HEADS, NSA_KV_GROUPS, NSA_HEAD_DIM
    gw = G * DH
    assert gw == 512 and (H * DH) % gw == 0
    t_kc = H * DH // gw
    o_gate = H * DH + 6 * gw
    o_z = o_gate + 3 * H
    rope_k = _rope_tables(positions, DH, 1.0)
    tq = NSA_TQ

    wide = 2 * gw
    rope_t = _rope_tables_t(positions, DH)
    kv_c = _mm(hb, w_in, cols=(t_kc, 1, 2), out_dtype=F32, rope=rope_k, rope_tiles=1,
               head_major=True)
    ksw = _mm(hb, w_in, cols=(t_kc + 2, 2, 2), out_dtype=BF16, rope=rope_k, head_major=True)
    vsw_t = _mm_t(hb, w_in, cols=(t_kc + 3, 2, 2), out_dtype=BF16, tk=tq)
    z = _mm(hb, w_in[:, o_z:], out_dtype=F32, tn=wide)
    w_gate = jnp.pad(w_in[:, o_gate:o_z], ((0, 0), (0, LANES - 3 * H)))
    gate_t = _mm_t(hb, w_gate, out_dtype=F32, tk=tq, tn=LANES)

    kc = _nsa_compress(kv_c, 0, pos_k, w1_k, w2_k, transposed=False)
    vc_t = _nsa_compress(kv_c, G, pos_v, w1_v, w2_v, transposed=True)
    onehot, agg_t = _nsa_constants(S)
    return _nsa_attn(hb, w_in, rope_t, kc, vc_t, ksw, vsw_t, onehot, agg_t, gate_t, z)


def _swa_layer(hb, positions, w_in, sinks):
    H, KV, DH = SWA_HEADS, SWA_KV_HEADS, SWA_HEAD_DIM
    o_k = H * DH
    o_v = o_k + KV * DH
    o_z = o_v + KV * DH
    tn = 512
    assert o_k % tn == 0 and o_z % tn == 0
    dup = np.concatenate([np.tile(np.arange(g * DH, (g + 1) * DH), 2) for g in range(KV)])
    rope_q = _rope_tables(positions, DH, DH ** -0.5 * LOG2_E)
    rope_k = _rope_tables(positions, DH, 1.0)
    q = _mm(hb, w_in, cols=(0, 1, o_k // (2 * tn)), out_dtype=BF16, rope=rope_q, tn=2 * tn)
    w_kv = jnp.concatenate([w_in[:, o_k:o_v][:, dup], w_in[:, o_v:o_z][:, dup]], axis=1)
    kv_dup = _mm(hb, w_kv, out_dtype=BF16, rope=rope_k, rope_tiles=1, tn=tn)
    z = _mm(hb, w_in, cols=(o_z // tn, 1, o_k // tn), out_dtype=F32, tn=tn)
    return _swa_attn(sinks, q, kv_dup, z)


def kernel(x, positions, hgrn_lb_logits, l0_w_in, l0_g_norm, l0_w_out, l0_ln_g, l0_ln_b, l1_w_in, l1_cmp_pos_k, l1_cmp_w1_k, l1_cmp_w2_k, l1_cmp_pos_v, l1_cmp_w1_v, l1_cmp_w2_v, l1_w_out, l1_ln_g, l1_ln_b, l2_w_in, l2_sinks, l2_w_out, l2_ln_g, l2_ln_b, l3_w_in, l3_g_norm, l3_w_out, l3_ln_g, l3_ln_b):
    B, S, D = x.shape
    assert B == 1 and D == D_MODEL
    lb = jnp.cumsum(jax.nn.softmax(hgrn_lb_logits.astype(F32), axis=0), axis=0)
    lb = lb - lb[0:1]
    log_lb, log_1m_lb = jnp.log(lb), jnp.log1p(-lb)

    h = x.reshape(S, D)
    hb = h.astype(BF16)

    a = _hgrn_layer(hb, l0_w_in, l0_g_norm, log_lb[0], log_1m_lb[0])
    h, hb = _outproj_ln(a, l0_w_out.astype(BF16), h, l0_ln_g, l0_ln_b)

    a = _nsa_layer(hb, positions, l1_w_in, l1_cmp_pos_k, l1_cmp_w1_k, l1_cmp_w2_k,
                   l1_cmp_pos_v, l1_cmp_w1_v, l1_cmp_w2_v)
    h, hb = _outproj_ln(a, l1_w_out.astype(BF16), h, l1_ln_g, l1_ln_b)

    a = _swa_layer(hb, positions, l2_w_in, l2_sinks)
    h, hb = _outproj_ln(a, l2_w_out.astype(BF16), h, l2_ln_g, l2_ln_b)

    a = _hgrn_layer(hb, l3_w_in, l3_g_norm, log_lb[1], log_1m_lb[1])
    h, hb = _outproj_ln(a, l3_w_out.astype(BF16), h, l3_ln_g, l3_ln_b)
    return h.reshape(B, S, D)
```
